```python
import math
import jax, jax.numpy as jnp
from jax import lax
import numpy as np

D_MODEL = 2048
BATCH = 4
SEQ = 4096
DEPTH = 2

CHUNK = 64
D_MIX = D_MODEL
GROUP_W = D_MIX // 4
SSD_HEAD_DIM = 64
SSD_HEADS = GROUP_W // SSD_HEAD_DIM
SSD_GROUPS = 2
SSD_HPG = SSD_HEADS // SSD_GROUPS
SSD_STATE = 128
SSD_CONV = 4
SSD_XBC = GROUP_W + 2 * SSD_GROUPS * SSD_STATE
DIFF_HEADS = 4
DIFF_QK_DIM = GROUP_W // (2 * DIFF_HEADS)
DIFF_V_DIM = 2 * DIFF_QK_DIM
Q_BLOCK = 128
ROPE_THETA = 10000.0
CONF_WIDTH = GROUP_W
CONF_KERNEL = 31
SC_WIDTH = GROUP_W
SC_KERNEL = 3
IN_SPLIT = (GROUP_W, SSD_XBC, SSD_HEADS, GROUP_W, GROUP_W, GROUP_W, 2 * CONF_WIDTH, 3 * SC_WIDTH)
N_IN = sum(IN_SPLIT)
IN_IDX = [int(v) for v in np.cumsum(IN_SPLIT)[:-1]]
N_EXPERTS = 16
N_GROUPS = 4
EXPERTS_PER_GROUP = N_EXPERTS // N_GROUPS
TOP_K = 2
D_EXPERT = D_MODEL // 2
MOE_BLOCK = 256
ALPHA = (2 * DEPTH) ** 0.25
BETA = (8 * DEPTH) ** -0.25
LN_EPS = 1e-5

kernel_name = 'hybrid_ssd_diffattn_conformer_shortconv_grouped_moe_deepnorm'

F32 = jnp.float32


def layer_norm(x, g, b):
    xf = x.astype(F32)
    mu = jnp.mean(xf, axis=-1, keepdims=True)
    var = jnp.mean(jnp.square(xf - mu), axis=-1, keepdims=True)
    return ((xf - mu) * lax.rsqrt(var + LN_EPS) * g.astype(F32) + b.astype(F32)).astype(x.dtype)


def rms_norm(x, g):
    xf = x.astype(F32)
    return (xf * lax.rsqrt(jnp.mean(xf * xf, axis=-1, keepdims=True) + LN_EPS) * g.astype(F32)).astype(x.dtype)


def causal_dwconv(x, w):
    k = w.shape[0]
    return lax.conv_general_dilated(x, w[:, None, :].astype(x.dtype), window_strides=(1,),
                                    padding=[(k - 1, 0)], dimension_numbers=('NWC', 'WIO', 'NWC'),
                                    feature_group_count=x.shape[-1])


def rope_tables(seq, dim):
    inv = 1.0 / (ROPE_THETA ** (jnp.arange(0, dim, 2, dtype=F32) / dim))
    ang = jnp.arange(seq, dtype=F32)[:, None] * inv[None, :]
    return jnp.cos(ang), jnp.sin(ang)


def apply_rope(x, cos, sin):
    half = x.shape[-1] // 2
    xf = x.astype(F32)
    x1, x2 = xf[..., :half], xf[..., half:]
    c = cos[:, None, None, :]
    s = sin[:, None, None, :]
    return jnp.concatenate([x1 * c - x2 * s, x2 * c + x1 * s], axis=-1).astype(x.dtype)


def ssd_mixer(z, xbc, dt_raw, conv_w, conv_b, dt_bias, a_log, d_skip, norm_g):
    b, s, _ = z.shape
    nc = s // CHUNK
    xbc = jax.nn.silu(causal_dwconv(xbc, conv_w) + conv_b)
    xs, bm, cm = jnp.split(xbc, [GROUP_W, GROUP_W + SSD_GROUPS * SSD_STATE], axis=-1)
    X = xs.astype(F32).reshape(b, nc, CHUNK, SSD_GROUPS, SSD_HPG, SSD_HEAD_DIM)
    Bm = bm.astype(F32).reshape(b, nc, CHUNK, SSD_GROUPS, SSD_STATE)
    Cm = cm.astype(F32).reshape(b, nc, CHUNK, SSD_GROUPS, SSD_STATE)
    dt = jax.nn.softplus(dt_raw.astype(F32) + dt_bias.astype(F32))
    dt = dt.reshape(b, nc, CHUNK, SSD_GROUPS, SSD_HPG)
    A = -jnp.exp(a_log.astype(F32)).reshape(SSD_GROUPS, SSD_HPG)
    a = jnp.moveaxis(dt * A, 2, -1)
    acs = jnp.cumsum(a, axis=-1)
    causal = jnp.tril(jnp.ones((CHUNK, CHUNK), dtype=bool))
    seg = acs[..., :, None] - acs[..., None, :]
    lmat = jnp.exp(jnp.where(causal, seg, -jnp.inf))
    Xdt = X * dt[..., None]
    cb = jnp.einsum('bclgn,bcsgn->bcgls', Cm, Bm)
    w_diag = cb[:, :, :, None] * lmat
    y_diag = jnp.einsum('bcgrls,bcsgrp->bclgrp', w_diag, Xdt)
    decay = jnp.moveaxis(jnp.exp(acs[..., -1:] - acs), -1, 2)
    states = jnp.einsum('bclgn,bclgrp->bcgrpn', Bm, Xdt * decay[..., None])
    chunk_decay = jnp.exp(acs[..., -1])

    def step(h, inp):
        st, dec = inp
        return h * dec[..., None, None] + st, h

    h0 = jnp.zeros((b, SSD_GROUPS, SSD_HPG, SSD_HEAD_DIM, SSD_STATE), F32)
    _, h_prev = lax.scan(step, h0, (jnp.moveaxis(states, 1, 0), jnp.moveaxis(chunk_decay, 1, 0)))
    h_prev = jnp.moveaxis(h_prev, 0, 1)
    in_decay = jnp.moveaxis(jnp.exp(acs), -1, 2)
    y_off = jnp.einsum('bclgn,bcgrpn->bclgrp', Cm, h_prev) * in_decay[..., None]
    y = y_diag + y_off + d_skip.astype(F32).reshape(SSD_GROUPS, SSD_HPG)[..., None] * X
    y = y.reshape(b, s, GROUP_W) * jax.nn.silu(z.astype(F32))
    y = rms_norm(y.reshape(b, s, SSD_GROUPS, GROUP_W // SSD_GROUPS),
                 norm_g.reshape(SSD_GROUPS, GROUP_W // SSD_GROUPS))
    return y.reshape(b, s, GROUP_W).astype(z.dtype)


def diff_attention(q, k, v, lq1, lk1, lq2, lk2, norm_g, lambda_init, cos, sin):
    b, s, _ = q.shape
    q = apply_rope(q.reshape(b, s, DIFF_HEADS, 2, DIFF_QK_DIM), cos, sin)
    k = apply_rope(k.reshape(b, s, DIFF_HEADS, 2, DIFF_QK_DIM), cos, sin)
    v = v.reshape(b, s, DIFF_HEADS, DIFF_V_DIM)
    lam = (jnp.exp(jnp.sum(lq1.astype(F32) * lk1.astype(F32)))
           - jnp.exp(jnp.sum(lq2.astype(F32) * lk2.astype(F32))) + lambda_init)
    nqb = s // Q_BLOCK
    qb = jnp.moveaxis(q.reshape(b, nqb, Q_BLOCK, DIFF_HEADS, 2, DIFF_QK_DIM), 1, 0)
    k_chunk = jnp.arange(s) // CHUNK
    scale = DIFF_QK_DIM ** -0.5

    def block(args):
        qi, i = args
        sc = jnp.einsum('bqhmd,bkhmd->bhmqk', qi, k, preferred_element_type=F32) * scale
        q_chunk = (i * Q_BLOCK + jnp.arange(Q_BLOCK)) // CHUNK
        visible = k_chunk[None, :] <= q_chunk[:, None]
        p = jax.nn.softmax(jnp.where(visible, sc, -jnp.inf), axis=-1)
        att = p[:, :, 0] - lam * p[:, :, 1]
        return jnp.einsum('bhqk,bkhe->bqhe', att.astype(v.dtype), v)

    o = lax.map(block, (qb, jnp.arange(nqb)))
    o = jnp.moveaxis(o, 0, 1).reshape(b, s, DIFF_HEADS, DIFF_V_DIM)
    o = rms_norm(o, norm_g) * (1.0 - lambda_init)
    return o.reshape(b, s, GROUP_W).astype(q.dtype)


def conformer_conv(u, dw_w, dw_b, ln_g, ln_b, pw_w, pw_b):
    a, g = jnp.split(u, 2, axis=-1)
    h = a * jax.nn.sigmoid(g)
    h = causal_dwconv(h, dw_w) + dw_b
    h = jax.nn.silu(layer_norm(h, ln_g, ln_b))
    return h @ pw_w + pw_b


def short_conv(u, w):
    bg, cg, h = jnp.split(u, 3, axis=-1)
    return bg * causal_dwconv(cg * h, w)


def grouped_moe(x2, router_w, router_bias, w_gate, w_up, w_down):
    t, d = x2.shape
    logits = jnp.dot(x2.astype(F32), router_w.astype(F32))
    aff = jax.nn.sigmoid(logits)
    sel = (aff + router_bias.astype(F32)).reshape(t, N_GROUPS, EXPERTS_PER_GROUP)
    grp_score = jnp.sum(lax.top_k(sel, 2)[0], axis=-1)
    _, grp = lax.top_k(grp_score, 1)
    sel_in = jnp.take_along_axis(sel, grp[:, :, None], axis=1)[:, 0]
    _, loc = lax.top_k(sel_in, TOP_K)
    eid = grp * EXPERTS_PER_GROUP + loc
    gate = jnp.take_along_axis(aff, eid, axis=1)
    gate = gate / jnp.sum(gate, axis=-1, keepdims=True)
    n_slots = t * TOP_K
    e_flat = eid.reshape(-1)
    tok_flat = jnp.repeat(jnp.arange(t), TOP_K)
    order = jnp.argsort(e_flat)
    e_sorted = e_flat[order]
    tok_sorted = tok_flat[order]
    g_sorted = gate.reshape(-1)[order]
    counts = jnp.bincount(e_flat, length=N_EXPERTS)
    starts = jnp.cumsum(counts) - counts
    padded = ((counts + MOE_BLOCK - 1) // MOE_BLOCK) * MOE_BLOCK
    pends = jnp.cumsum(padded)
    pstarts = pends - padded
    dest = pstarts[e_sorted] + (jnp.arange(n_slots) - starts[e_sorted])
    n_blk = -(-n_slots // MOE_BLOCK) + N_EXPERTS
    slot_tok = jnp.full((n_blk * MOE_BLOCK,), t, dtype=jnp.int32).at[dest].set(tok_sorted)
    x_pad = jnp.concatenate([x2, jnp.zeros((1, d), x2.dtype)], axis=0)[slot_tok]
    blk_exp = jnp.minimum(jnp.searchsorted(pends, jnp.arange(n_blk) * MOE_BLOCK, side='right'),
                          N_EXPERTS - 1)

    def expert_block(args):
        xb, e = args
        h = jax.nn.silu(xb @ w_gate[e]) * (xb @ w_up[e])
        return h @ w_down[e]

    y_pad = lax.map(expert_block, (x_pad.reshape(n_blk, MOE_BLOCK, d), blk_exp)).reshape(-1, d)
    y_slot = (y_pad[dest].astype(F32) * g_sorted[:, None]).astype(x2.dtype)
    return jnp.zeros_like(x2).at[tok_sorted].add(y_slot)


def setup_inputs(seed: int = 0) -> dict:
    key = jax.random.key(seed)
    ks = iter(jax.random.split(key, 40))
    L = DEPTH

    def nrm(shape, scale):
        return jax.random.normal(next(ks), shape, F32) * scale

    x = nrm((BATCH, SEQ, D_MODEL), 1.0)
    ln_in_g = 1.0 + nrm((D_MODEL,), 0.02)
    ln_in_b = nrm((D_MODEL,), 0.02)
    w_in = nrm((L, D_MODEL, N_IN), D_MODEL ** -0.5)
    ssd_conv_w = nrm((L, SSD_CONV, SSD_XBC), SSD_CONV ** -0.5)
    ssd_conv_b = nrm((L, SSD_XBC), 0.02)
    dt = jnp.exp(jax.random.uniform(next(ks), (L, SSD_HEADS), F32,
                                    minval=math.log(1e-3), maxval=math.log(1e-1)))
    ssd_dt_bias = dt + jnp.log(-jnp.expm1(-dt))
    ssd_a_log = jnp.log(jax.random.uniform(next(ks), (L, SSD_HEADS), F32, minval=1.0, maxval=16.0))
    ssd_d = 1.0 + nrm((L, SSD_HEADS), 0.02)
    ssd_norm_g = 1.0 + nrm((L, GROUP_W), 0.02)
    diff_lq1 = nrm((L, DIFF_QK_DIM), 0.1)
    diff_lk1 = nrm((L, DIFF_QK_DIM), 0.1)
    diff_lq2 = nrm((L, DIFF_QK_DIM), 0.1)
    diff_lk2 = nrm((L, DIFF_QK_DIM), 0.1)
    diff_norm_g = 1.0 + nrm((L, DIFF_V_DIM), 0.02)
    conf_dw_w = nrm((L, CONF_KERNEL, CONF_WIDTH), CONF_KERNEL ** -0.5)
    conf_dw_b = nrm((L, CONF_WIDTH), 0.02)
    conf_ln_g = 1.0 + nrm((L, CONF_WIDTH), 0.02)
    conf_ln_b = nrm((L, CONF_WIDTH), 0.02)
    conf_pw_w = nrm((L, CONF_WIDTH, CONF_WIDTH), CONF_WIDTH ** -0.5)
    conf_pw_b = nrm((L, CONF_WIDTH), 0.02)
    sc_conv_w = nrm((L, SC_KERNEL, SC_WIDTH), SC_KERNEL ** -0.5)
    w_out = nrm((L, D_MIX, D_MODEL), BETA * D_MIX ** -0.5)
    ln1_g = 1.0 + nrm((L, D_MODEL), 0.02)
    ln1_b = nrm((L, D_MODEL), 0.02)
    router_w = nrm((D_MODEL, N_EXPERTS), D_MODEL ** -0.5)
    router_bias = nrm((N_EXPERTS,), 0.01)
    moe_w_gate = nrm((L, N_EXPERTS, D_MODEL, D_EXPERT), D_MODEL ** -0.5)
    moe_w_up = nrm((L, N_EXPERTS, D_MODEL, D_EXPERT), D_MODEL ** -0.5)
    moe_w_down = nrm((L, N_EXPERTS, D_EXPERT, D_MODEL), BETA * D_EXPERT ** -0.5)
    ln2_g = 1.0 + nrm((L, D_MODEL), 0.02)
    ln2_b = nrm((L, D_MODEL), 0.02)
    return {'x': x, 'ln_in_g': ln_in_g, 'ln_in_b': ln_in_b, 'w_in': w_in,
            'ssd_conv_w': ssd_conv_w, 'ssd_conv_b': ssd_conv_b, 'ssd_dt_bias': ssd_dt_bias,
            'ssd_a_log': ssd_a_log, 'ssd_d': ssd_d, 'ssd_norm_g': ssd_norm_g,
            'diff_lq1': diff_lq1, 'diff_lk1': diff_lk1, 'diff_lq2': diff_lq2, 'diff_lk2': diff_lk2,
            'diff_norm_g': diff_norm_g, 'conf_dw_w': conf_dw_w, 'conf_dw_b': conf_dw_b,
            'conf_ln_g': conf_ln_g, 'conf_ln_b': conf_ln_b, 'conf_pw_w': conf_pw_w,
            'conf_pw_b': conf_pw_b, 'sc_conv_w': sc_conv_w, 'w_out': w_out,
            'ln1_g': ln1_g, 'ln1_b': ln1_b, 'router_w': router_w, 'router_bias': router_bias,
            'moe_w_gate': moe_w_gate, 'moe_w_up': moe_w_up, 'moe_w_down': moe_w_down,
            'ln2_g': ln2_g, 'ln2_b': ln2_b}


def reference(x, ln_in_g, ln_in_b, w_in, ssd_conv_w, ssd_conv_b, ssd_dt_bias, ssd_a_log, ssd_d,
              ssd_norm_g, diff_lq1, diff_lk1, diff_lq2, diff_lk2, diff_norm_g, conf_dw_w,
              conf_dw_b, conf_ln_g, conf_ln_b, conf_pw_w, conf_pw_b, sc_conv_w, w_out,
              ln1_g, ln1_b, router_w, router_bias, moe_w_gate, moe_w_up, moe_w_down,
              ln2_g, ln2_b):
    b, s, d = x.shape
    cos, sin = rope_tables(s, DIFF_QK_DIM)
    x = layer_norm(x, ln_in_g, ln_in_b)
    for l in range(DEPTH):
        lambda_init = 0.8 - 0.6 * math.exp(-0.3 * l)
        h = x @ w_in[l]
        z, xbc, dt_raw, q, k, v, u_conf, u_sc = jnp.split(h, IN_IDX, axis=-1)
        y_a = ssd_mixer(z, xbc, dt_raw, ssd_conv_w[l], ssd_conv_b[l], ssd_dt_bias[l],
                        ssd_a_log[l], ssd_d[l], ssd_norm_g[l])
        y_b = diff_attention(q, k, v, diff_lq1[l], diff_lk1[l], diff_lq2[l], diff_lk2[l],
                             diff_norm_g[l], lambda_init, cos, sin)
        y_c = conformer_conv(u_conf, conf_dw_w[l], conf_dw_b[l], conf_ln_g[l], conf_ln_b[l],
                             conf_pw_w[l], conf_pw_b[l])
        y_d = short_conv(u_sc, sc_conv_w[l])
        mix = jnp.concatenate([y_a, y_b, y_c, y_d], axis=-1) @ w_out[l]
        x = layer_norm(ALPHA * x + mix, ln1_g[l], ln1_b[l])
        moe = grouped_moe(x.reshape(b * s, d), router_w, router_bias,
                          moe_w_gate[l], moe_w_up[l], moe_w_down[l]).reshape(b, s, d)
        x = layer_norm(ALPHA * x + moe, ln2_g[l], ln2_b[l])
    return x
```

```python
import functools
import math

import numpy as np
import jax
import jax.numpy as jnp
from jax import lax
from jax.experimental import pallas as pl
from jax.experimental.pallas import tpu as pltpu

F32 = jnp.float32
BF16 = jnp.bfloat16
I32 = jnp.int32

D_MODEL = 2048
DEPTH = 2
CHUNK = 64
GROUP_W = 512
SSD_HEADS = 8
SSD_STATE = 128
SSD_CONV = 4
SSD_XBC = 1024
DIFF_HEADS = 4
DIFF_QK_DIM = 64
DIFF_V_DIM = 128
ROPE_THETA = 10000.0
CONF_KERNEL = 31
SC_KERNEL = 3
N_EXPERTS = 16
N_GROUPS = 4
EXPERTS_PER_GROUP = 4
TOP_K = 2
D_EXPERT = 1024
MOE_BLOCK = 256
ALPHA = (2 * DEPTH) ** 0.25
LN_EPS = 1e-5

LANES = 128
SUBLANES = 8
VMEM_LIMIT_BYTES = 56 * 1024 * 1024

COL_XBC, COL_CONF, COL_BG, COL_CG, COL_HH, COL_Z, COL_Q, COL_K, COL_V = 0, 2, 4, 5, 6, 7, 8, 9, 10
H_COLS = 11 * 512
TN = 512
SSD_L = 128


def _cparams(*sem):
    return pltpu.CompilerParams(dimension_semantics=tuple(sem), vmem_limit_bytes=VMEM_LIMIT_BYTES)


def _sigmoid(x):
    return 1.0 / (1.0 + jnp.exp(-x))


def _silu(x):
    return x * _sigmoid(x)


def _softplus(x):
    return jnp.maximum(x, 0.0) + jnp.log(1.0 + jnp.exp(-jnp.abs(x)))


def _split3(v):
    hi = v.astype(BF16)
    r = v - hi.astype(F32)
    mid = r.astype(BF16)
    lo = (r - mid.astype(F32)).astype(BF16)
    return hi, mid, lo


def _dot_exact_rhs(v, m):
    return sum(jnp.dot(p, m, preferred_element_type=F32) for p in _split3(v))


def _dot_exact_lhs(m, v):
    return sum(jnp.dot(m, p, preferred_element_type=F32) for p in _split3(v))


def _dot_nt(a, b):
    return lax.dot_general(a, b, (((1,), (1,)), ((), ())), preferred_element_type=F32)


def _ln_kernel(x_ref, g_ref, b_ref, o_ref):
    x = x_ref[...]
    mu = jnp.mean(x, axis=-1, keepdims=True)
    xc = x - mu
    var = jnp.mean(xc * xc, axis=-1, keepdims=True)
    o_ref[...] = xc * lax.rsqrt(var + LN_EPS) * g_ref[...] + b_ref[...]


def _layer_norm(x, g, b):
    t, d = x.shape
    tm = min(512, t)
    return pl.pallas_call(
        _ln_kernel,
        out_shape=jax.ShapeDtypeStruct((t, d), F32),
        grid=(t // tm,),
        in_specs=[pl.BlockSpec((tm, d), lambda i: (i, 0)),
                  pl.BlockSpec((1, d), lambda i: (0, 0)),
                  pl.BlockSpec((1, d), lambda i: (0, 0))],
        out_specs=pl.BlockSpec((tm, d), lambda i: (i, 0)),
        compiler_params=_cparams("parallel"),
        name="entry_ln",
    )(x, g.reshape(1, d), b.reshape(1, d))


def _inproj_kernel(x_ref, w_ref, wdt_ref, wdtT_ref, cos_ref, sin_ref, h_ref, dt_ref, dtT_ref, xb_ref):
    j = pl.program_id(1)
    tm = x_ref.shape[0]

    @pl.when(j == 0)
    def _():
        xb = x_ref[...].astype(BF16)
        xb_ref[...] = xb
        dt_ref[...] = jnp.dot(xb, wdt_ref[...], preferred_element_type=F32)
        dtT_ref[...] = _dot_nt(wdtT_ref[...], xb)

    acc = jnp.dot(xb_ref[...], w_ref[...], preferred_element_type=F32)
    is_rope = jnp.logical_or(j == COL_Q, j == COL_K)

    @pl.when(jnp.logical_not(is_rope))
    def _():
        h_ref[...] = acc.astype(BF16)

    @pl.when(is_rope)
    def _():
        scale = jnp.where(j == COL_Q, DIFF_QK_DIM ** -0.5, 1.0).astype(F32)
        c = cos_ref[...] * scale
        s = sin_ref[...] * scale
        lane = lax.broadcasted_iota(I32, (tm, LANES), 1)
        first_half = (lane % DIFF_QK_DIM) < (DIFF_QK_DIM // 2)
        for hh in range(TN // LANES):
            a = acc[:, hh * LANES:(hh + 1) * LANES]
            rot = jnp.where(first_half, pltpu.roll(a, LANES - 32, 1), pltpu.roll(a, 32, 1))
            h_ref[:, hh * LANES:(hh + 1) * LANES] = (a * c + rot * s).astype(BF16)


def _in_projection(x, w_main, w_dt, w_dtT, cos_t, sin_t, seq):
    t, d = x.shape
    tm = min(1024, seq)
    nseq = seq // tm
    return pl.pallas_call(
        _inproj_kernel,
        out_shape=(jax.ShapeDtypeStruct((t, H_COLS), BF16),
                   jax.ShapeDtypeStruct((t, LANES), F32),
                   jax.ShapeDtypeStruct((16, t), F32)),
        grid=(t // tm, H_COLS // TN),
        in_specs=[pl.BlockSpec((tm, d), lambda i, j: (i, 0)),
                  pl.BlockSpec((d, TN), lambda i, j: (0, j)),
                  pl.BlockSpec((d, LANES), lambda i, j: (0, 0)),
                  pl.BlockSpec((16, d), lambda i, j: (0, 0)),
                  pl.BlockSpec((tm, LANES), lambda i, j: (i % nseq, 0)),
                  pl.BlockSpec((tm, LANES), lambda i, j: (i % nseq, 0))],
        out_specs=(pl.BlockSpec((tm, TN), lambda i, j: (i, j)),
                   pl.BlockSpec((tm, LANES), lambda i, j: (i, 0)),
                   pl.BlockSpec((16, tm), lambda i, j: (0, i))),
        scratch_shapes=[pltpu.VMEM((tm, d), BF16)],
        compiler_params=_cparams("arbitrary", "arbitrary"),
        name="in_proj",
    )(x, w_main, w_dt, w_dtT, cos_t, sin_t)


def _ssd_kernel(xbc_ref, z_ref, dt_ref, dtT_ref, cw_ref, cb_ref, dtb_ref, alog_ref, dtbT_ref, alogT_ref,
                dsk_ref, ng_ref, e64_ref, e128_ref, tri_ref, triT_ref, y_ref, xpad_ref, hst_ref):
    c = pl.program_id(1)
    L = SSD_L

    @pl.when(c == 0)
    def _():
        xpad_ref[0:SUBLANES, :] = jnp.zeros((SUBLANES, SSD_XBC), F32)
        hst_ref[...] = jnp.zeros(hst_ref.shape, F32)

    cur = xbc_ref[...].astype(F32)
    xpad_ref[SUBLANES:SUBLANES + L, :] = cur
    conv = cb_ref[...]
    for k in range(SSD_CONV):
        conv = conv + cw_ref[k:k + 1, :] * xpad_ref[pl.ds(SUBLANES - (SSD_CONV - 1) + k, L), :]
    xpad_ref[0:SUBLANES, :] = cur[L - SUBLANES:L, :]
    xbc = _silu(conv)
    X = xbc[:, 0:GROUP_W]
    Bm = xbc[:, GROUP_W:GROUP_W + 2 * SSD_STATE]
    Cm = xbc[:, GROUP_W + 2 * SSD_STATE:SSD_XBC]

    dt = _softplus(dt_ref[...] + dtb_ref[...])
    acs = _dot_exact_lhs(tri_ref[...], dt * (-jnp.exp(alog_ref[...])))
    dtT = _softplus(dtT_ref[...] + dtbT_ref[...])
    acsT = _dot_exact_rhs(dtT * (-jnp.exp(alogT_ref[...])), triT_ref[...])
    dt64 = _dot_exact_rhs(dt, e64_ref[...])
    acs64 = _dot_exact_rhs(acs, e64_ref[...])
    acs128 = _dot_exact_rhs(acs, e128_ref[...])

    last = acs64[L - 1:L, :]
    in_decay = jnp.exp(acs64)
    decay = jnp.exp(last - acs64)
    chunk_decay = jnp.exp(last)
    Xdt = X * dt64
    Xd = (Xdt * decay).astype(BF16)
    Xdt_b = Xdt.astype(BF16)

    row = lax.broadcasted_iota(I32, (L, L), 0)
    col = lax.broadcasted_iota(I32, (L, L), 1)
    causal = row >= col
    lane = lax.broadcasted_iota(I32, (L, LANES), 1)
    low_half = lane < 64

    y_parts = []
    for g in range(2):
        Bg = Bm[:, g * SSD_STATE:(g + 1) * SSD_STATE]
        Cg = Cm[:, g * SSD_STATE:(g + 1) * SSD_STATE].astype(BF16)
        cb = _dot_nt(Cg, Bg.astype(BF16))
        hprev = hst_ref[g]
        y_off = jnp.dot(Cg, hprev.astype(BF16), preferred_element_type=F32) * in_decay[:, g * 256:(g + 1) * 256]
        for pr in range(2):
            xp = Xdt_b[:, (2 * g + pr) * LANES:(2 * g + pr + 1) * LANES]
            ys = []
            for hh in range(2):
                h = 4 * g + 2 * pr + hh
                seg = acs128[:, h * LANES:(h + 1) * LANES] - acsT[h:h + 1, :]
                lmat = jnp.exp(jnp.where(causal, seg, -jnp.inf))
                w = (cb * lmat).astype(BF16)
                ys.append(jnp.dot(w, xp, preferred_element_type=F32))
            y_parts.append(jnp.where(low_half, ys[0], ys[1]) + y_off[:, pr * LANES:(pr + 1) * LANES])
        states = jnp.dot(Bg.T.astype(BF16), Xd[:, g * 256:(g + 1) * 256], preferred_element_type=F32)
        hst_ref[g] = hprev * chunk_decay[:, g * 256:(g + 1) * 256] + states

    z = z_ref[...].astype(F32)
    gate = _silu(z)
    for g in range(2):
        yg = jnp.concatenate(y_parts[2 * g:2 * g + 2], axis=1) + dsk_ref[:, g * 256:(g + 1) * 256] * X[:, g * 256:(g + 1) * 256]
        yg = yg * gate[:, g * 256:(g + 1) * 256]
        ms = jnp.mean(yg * yg, axis=-1, keepdims=True)
        y_ref[:, g * 256:(g + 1) * 256] = (yg * lax.rsqrt(ms + LN_EPS) * ng_ref[:, g * 256:(g + 1) * 256]).astype(BF16)


def _ssd_constants():
    e64 = np.zeros((LANES, 512), np.float32)
    e128 = np.zeros((LANES, 1024), np.float32)
    for h in range(SSD_HEADS):
        e64[h, 64 * h:64 * (h + 1)] = 1.0
        e128[h, 128 * h:128 * (h + 1)] = 1.0
    tri = np.tril(np.ones((SSD_L, SSD_L), np.float32))
    return (jnp.asarray(e64, BF16), jnp.asarray(e128, BF16), jnp.asarray(tri, BF16), jnp.asarray(tri.T, BF16))


def _ssd_mixer(h, dt_raw, dtT_raw, conv_w, conv_b, dt_bias, a_log, d_skip, norm_g, batch, seq):
    t = h.shape[0]
    L = SSD_L
    nc = seq // L
    e64, e128, tri, triT = _ssd_constants()
    pad_l = lambda v: jnp.pad(v.reshape(1, SSD_HEADS), ((0, 0), (0, LANES - SSD_HEADS)))
    pad_t = lambda v: jnp.pad(jnp.broadcast_to(v.reshape(SSD_HEADS, 1), (SSD_HEADS, L)), ((0, 16 - SSD_HEADS), (0, 0)))
    rb = lambda b, c: b * nc + c
    const = lambda shape: pl.BlockSpec(shape, lambda b, c: (0,) * len(shape))
    return pl.pallas_call(
        _ssd_kernel,
        out_shape=jax.ShapeDtypeStruct((t, GROUP_W), BF16),
        grid=(batch, nc),
        in_specs=[pl.BlockSpec((L, SSD_XBC), lambda b, c: (rb(b, c), COL_XBC // 2)),
                  pl.BlockSpec((L, GROUP_W), lambda b, c: (rb(b, c), COL_Z)),
                  pl.BlockSpec((L, LANES), lambda b, c: (rb(b, c), 0)),
                  pl.BlockSpec((16, L), lambda b, c: (0, rb(b, c))),
                  const((SSD_CONV, SSD_XBC)), const((1, SSD_XBC)),
                  const((1, LANES)), const((1, LANES)), const((16, L)), const((16, L)),
                  const((1, GROUP_W)), const((1, GROUP_W)),
                  const((LANES, 512)), const((LANES, 1024)), const((L, L)), const((L, L))],
        out_specs=pl.BlockSpec((L, GROUP_W), lambda b, c: (rb(b, c), 0)),
        scratch_shapes=[pltpu.VMEM((SUBLANES + L, SSD_XBC), F32),
                        pltpu.VMEM((2, SSD_STATE, 256), F32)],
        compiler_params=_cparams("arbitrary", "arbitrary"),
        name="ssd_mixer",
    )(h, h, dt_raw, dtT_raw, conv_w, conv_b.reshape(1, SSD_XBC),
      pad_l(dt_bias), pad_l(a_log), pad_t(dt_bias), pad_t(a_log),
      jnp.repeat(d_skip, 64).reshape(1, GROUP_W), norm_g.reshape(1, GROUP_W), e64, e128, tri, triT)


def _attn_kernel(q_ref, k_ref, v_ref, lq1_ref, lk1_ref, lq2_ref, lk2_ref, g_ref, o_ref,
                 m_ref, l_ref, acc_ref, *, lambda_init):
    qi = pl.program_id(2)
    ki = pl.program_id(3)
    tq = q_ref.shape[0]
    tk = k_ref.shape[0]

    @pl.when(ki == 0)
    def _():
        m_ref[...] = jnp.full(m_ref.shape, -jnp.inf, F32)
        l_ref[...] = jnp.zeros(l_ref.shape, F32)
        acc_ref[...] = jnp.zeros(acc_ref.shape, F32)

    def step(masked):
        q = q_ref[...]
        k = k_ref[...]
        v = v_ref[...]
        lane = lax.broadcasted_iota(I32, (tq, LANES), 1)
        zero = jnp.zeros_like(q)
        qs = (jnp.where(lane < DIFF_QK_DIM, q, zero), jnp.where(lane >= DIFF_QK_DIM, q, zero))
        if masked:
            row = lax.broadcasted_iota(I32, (tq, tk), 0) // CHUNK
            col = lax.broadcasted_iota(I32, (tq, tk), 1) // CHUNK
            visible = col <= row
        for m in range(2):
            s = _dot_nt(qs[m], k)
            if masked:
                s = jnp.where(visible, s, -jnp.inf)
            m_prev = m_ref[m]
            m_new = jnp.maximum(m_prev, jnp.max(s, axis=-1, keepdims=True))
            alpha = jnp.exp(m_prev - m_new)
            p = jnp.exp(s - m_new)
            l_ref[m] = alpha * l_ref[m] + jnp.sum(p, axis=-1, keepdims=True)
            acc_ref[m] = alpha * acc_ref[m] + jnp.dot(p.astype(BF16), v, preferred_element_type=F32)
            m_ref[m] = m_new

    @pl.when(ki < qi)
    def _():
        step(False)

    @pl.when(ki == qi)
    def _():
        step(True)
        lam = (jnp.exp(jnp.sum(lq1_ref[...] * lk1_ref[...], axis=-1, keepdims=True))
               - jnp.exp(jnp.sum(lq2_ref[...] * lk2_ref[...], axis=-1, keepdims=True)) + lambda_init)
        o = acc_ref[0] / l_ref[0] - lam * (acc_ref[1] / l_ref[1])
        ms = jnp.mean(o * o, axis=-1, keepdims=True)
        o_ref[...] = (o * lax.rsqrt(ms + LN_EPS) * g_ref[...] * (1.0 - lambda_init)).astype(BF16)


def _diff_attention(h, lq1, lk1, lq2, lk2, norm_g, lambda_init, batch, seq):
    t = h.shape[0]
    tq = min(512, seq)
    nq = seq // tq
    qcol = COL_Q * 4
    kcol = COL_K * 4
    vcol = COL_V * 4
    vec = lambda n: pl.BlockSpec((1, n), lambda b, hd, qi, ki: (0, 0))
    return pl.pallas_call(
        functools.partial(_attn_kernel, lambda_init=lambda_init),
        out_shape=jax.ShapeDtypeStruct((t, GROUP_W), BF16),
        grid=(batch, DIFF_HEADS, nq, nq),
        in_specs=[pl.BlockSpec((tq, LANES), lambda b, hd, qi, ki: (b * nq + qi, qcol + hd)),
                  pl.BlockSpec((tq, LANES), lambda b, hd, qi, ki: (b * nq + jnp.minimum(ki, qi), kcol + hd)),
                  pl.BlockSpec((tq, LANES), lambda b, hd, qi, ki: (b * nq + jnp.minimum(ki, qi), vcol + hd)),
                  vec(DIFF_QK_DIM), vec(DIFF_QK_DIM), vec(DIFF_QK_DIM), vec(DIFF_QK_DIM), vec(DIFF_V_DIM)],
        out_specs=pl.BlockSpec((tq, LANES), lambda b, hd, qi, ki: (b * nq + qi, hd)),
        scratch_shapes=[pltpu.VMEM((2, tq, 1), F32), pltpu.VMEM((2, tq, 1), F32),
                        pltpu.VMEM((2, tq, DIFF_V_DIM), F32)],
        compiler_params=_cparams("arbitrary", "arbitrary", "arbitrary", "arbitrary"),
        name="diff_attn",
    )(h, h, h, lq1.reshape(1, -1), lk1.reshape(1, -1), lq2.reshape(1, -1), lk2.reshape(1, -1),
      norm_g.reshape(1, -1))


CONF_HALO = 32
SC_HALO = 16
CONV_ROWS = 64


def _conv_kernel(u_ref, uh_ref, bg_ref, cg_ref, hh_ref, cgh_ref, hhh_ref,
                 dww_ref, dwb_ref, lng_ref, lnb_ref, pww_ref, pwb_ref, scw_ref,
                 yc_ref, yd_ref, hbuf_ref, pbuf_ref, cbuf_ref):
    i = pl.program_id(1)
    tm = u_ref.shape[0]

    def glu(u):
        u = u.astype(F32)
        return u[:, 0:GROUP_W] * _sigmoid(u[:, GROUP_W:2 * GROUP_W])

    first = (i == 0)
    hbuf_ref[0:CONF_HALO, :] = jnp.where(first, 0.0, glu(uh_ref[...]))
    hbuf_ref[CONF_HALO:CONF_HALO + tm, :] = glu(u_ref[...])
    pbuf_ref[0:SC_HALO, :] = jnp.where(first, 0.0, cgh_ref[...].astype(F32) * hhh_ref[...].astype(F32))
    pbuf_ref[SC_HALO:SC_HALO + tm, :] = cg_ref[...].astype(F32) * hh_ref[...].astype(F32)

    for r0 in range(0, tm, CONV_ROWS):
        acc = jnp.zeros((CONV_ROWS, GROUP_W), F32) + dwb_ref[...]
        for k in range(CONF_KERNEL):
            acc = acc + dww_ref[k:k + 1, :] * hbuf_ref[pl.ds(r0 + CONF_HALO - (CONF_KERNEL - 1) + k, CONV_ROWS), :]
        cbuf_ref[r0:r0 + CONV_ROWS, :] = acc
    hc = cbuf_ref[...]
    mu = jnp.mean(hc, axis=-1, keepdims=True)
    xc = hc - mu
    var = jnp.mean(xc * xc, axis=-1, keepdims=True)
    hn = _silu(xc * lax.rsqrt(var + LN_EPS) * lng_ref[...] + lnb_ref[...])
    yc = jnp.dot(hn.astype(BF16), pww_ref[...], preferred_element_type=F32) + pwb_ref[...]
    yc_ref[...] = yc.astype(BF16)

    sc = jnp.zeros((tm, GROUP_W), F32)
    for k in range(SC_KERNEL):
        sc = sc + scw_ref[k:k + 1, :] * pbuf_ref[pl.ds(SC_HALO - (SC_KERNEL - 1) + k, tm), :]
    yd_ref[...] = (bg_ref[...].astype(F32) * sc).astype(BF16)


def _conv_mixers(h, dw_w, dw_b, ln_g, ln_b, pw_w, pw_b, sc_w, batch, seq):
    t = h.shape[0]
    tm = min(512, seq)
    nt = seq // tm
    rb = lambda b, i: b * nt + i
    halo = lambda rows: (lambda b, i: jnp.maximum(rb(b, i) * (tm // rows) - 1, 0))
    hc, hs = halo(CONF_HALO), halo(SC_HALO)
    const = lambda shape: pl.BlockSpec(shape, lambda b, i: (0,) * len(shape))
    return pl.pallas_call(
        _conv_kernel,
        out_shape=(jax.ShapeDtypeStruct((t, GROUP_W), BF16), jax.ShapeDtypeStruct((t, GROUP_W), BF16)),
        grid=(batch, nt),
        in_specs=[pl.BlockSpec((tm, 2 * GROUP_W), lambda b, i: (rb(b, i), COL_CONF // 2)),
                  pl.BlockSpec((CONF_HALO, 2 * GROUP_W), lambda b, i: (hc(b, i), COL_CONF // 2)),
                  pl.BlockSpec((tm, GROUP_W), lambda b, i: (rb(b, i), COL_BG)),
                  pl.BlockSpec((tm, GROUP_W), lambda b, i: (rb(b, i), COL_CG)),
                  pl.BlockSpec((tm, GROUP_W), lambda b, i: (rb(b, i), COL_HH)),
                  pl.BlockSpec((SC_HALO, GROUP_W), lambda b, i: (hs(b, i), COL_CG)),
                  pl.BlockSpec((SC_HALO, GROUP_W), lambda b, i: (hs(b, i), COL_HH)),
                  const((CONF_KERNEL, GROUP_W)), const((1, GROUP_W)), const((1, GROUP_W)), const((1, GROUP_W)),
                  const((GROUP_W, GROUP_W)), const((1, GROUP_W)), const((SC_KERNEL, GROUP_W))],
        out_specs=(pl.BlockSpec((tm, GROUP_W), lambda b, i: (rb(b, i), 0)),
                   pl.BlockSpec((tm, GROUP_W), lambda b, i: (rb(b, i), 0))),
        scratch_shapes=[pltpu.VMEM((CONF_HALO + tm, GROUP_W), F32),
                        pltpu.VMEM((SC_HALO + tm, GROUP_W), F32),
                        pltpu.VMEM((tm, GROUP_W), F32)],
        compiler_params=_cparams("arbitrary", "arbitrary"),
        name="conv_mixers",
    )(h, h, h, h, h, h, h, dw_w, dw_b.reshape(1, -1), ln_g.reshape(1, -1), ln_b.reshape(1, -1),
      pw_w.astype(BF16), pw_b.reshape(1, -1), sc_w)


def _first_max4(vals):
    m1 = jnp.maximum(jnp.maximum(vals[0], vals[1]), jnp.maximum(vals[2], vals[3]))
    i1 = jnp.where(vals[0] == m1, 0, jnp.where(vals[1] == m1, 1, jnp.where(vals[2] == m1, 2, 3)))
    rest = [jnp.where(i1 == j, -jnp.inf, vals[j]) for j in range(4)]
    m2 = jnp.maximum(jnp.maximum(rest[0], rest[1]), jnp.maximum(rest[2], rest[3]))
    i2 = jnp.where(rest[0] == m2, 0, jnp.where(rest[1] == m2, 1, jnp.where(rest[2] == m2, 2, 3)))
    return m1, i1, m2, i2


def _outproj_kernel(ya_ref, yb_ref, yc_ref, yd_ref, x_ref, w_ref, g_ref, b_ref, rwT_ref, rb_ref,
                    x1_ref, eid_ref, gate_ref):
    tm = x_ref.shape[0]
    mix = jnp.dot(ya_ref[...], w_ref[0:GROUP_W, :], preferred_element_type=F32)
    mix = mix + jnp.dot(yb_ref[...], w_ref[GROUP_W:2 * GROUP_W, :], preferred_element_type=F32)
    mix = mix + jnp.dot(yc_ref[...], w_ref[2 * GROUP_W:3 * GROUP_W, :], preferred_element_type=F32)
    mix = mix + jnp.dot(yd_ref[...], w_ref[3 * GROUP_W:4 * GROUP_W, :], preferred_element_type=F32)
    y = ALPHA * x_ref[...] + mix
    mu = jnp.mean(y, axis=-1, keepdims=True)
    yc = y - mu
    var = jnp.mean(yc * yc, axis=-1, keepdims=True)
    x1 = yc * lax.rsqrt(var + LN_EPS) * g_ref[...] + b_ref[...]
    x1_ref[...] = x1

    xs = _split3(x1)
    ws = _split3(rwT_ref[...])
    logits = (_dot_nt(ws[0], xs[0]) + _dot_nt(ws[0], xs[1]) + _dot_nt(ws[1], xs[0])
              + _dot_nt(ws[0], xs[2]) + _dot_nt(ws[2], xs[0]) + _dot_nt(ws[1], xs[1]))
    aff = _sigmoid(logits)
    sel = aff + jnp.concatenate([rb_ref[...]] * (tm // LANES), axis=1)
    rows = [sel[e:e + 1, :] for e in range(N_EXPERTS)]
    arow = [aff[e:e + 1, :] for e in range(N_EXPERTS)]
    tops = [_first_max4(rows[4 * g:4 * g + 4]) for g in range(N_GROUPS)]
    score = [tp[0] + tp[2] for tp in tops]
    best = jnp.maximum(jnp.maximum(score[0], score[1]), jnp.maximum(score[2], score[3]))
    grp = jnp.where(score[0] == best, 0, jnp.where(score[1] == best, 1, jnp.where(score[2] == best, 2, 3)))
    pick = lambda idx: jnp.where(grp == 0, tops[0][idx], jnp.where(grp == 1, tops[1][idx],
                                 jnp.where(grp == 2, tops[2][idx], tops[3][idx])))
    e1 = grp * EXPERTS_PER_GROUP + pick(1)
    e2 = grp * EXPERTS_PER_GROUP + pick(3)
    a1 = jnp.zeros_like(best)
    a2 = jnp.zeros_like(best)
    for e in range(N_EXPERTS):
        a1 = jnp.where(e1 == e, arow[e], a1)
        a2 = jnp.where(e2 == e, arow[e], a2)
    den = a1 + a2
    zi = jnp.zeros((SUBLANES - 2, tm), I32)
    zf = jnp.zeros((SUBLANES - 2, tm), F32)
    eid_ref[...] = jnp.concatenate([e1.astype(I32), e2.astype(I32), zi], axis=0)
    gate_ref[...] = jnp.concatenate([a1 / den, a2 / den, zf], axis=0)


def _out_projection(ya, yb, yc, yd, x, w_out, ln_g, ln_b, router_wT, router_b):
    t, d = x.shape
    tm = min(256, t)
    act = lambda: pl.BlockSpec((tm, GROUP_W), lambda i: (i, 0))
    const = lambda shape: pl.BlockSpec(shape, lambda i: (0,) * len(shape))
    return pl.pallas_call(
        _outproj_kernel,
        out_shape=(jax.ShapeDtypeStruct((t, d), F32),
                   jax.ShapeDtypeStruct((SUBLANES, t), I32),
                   jax.ShapeDtypeStruct((SUBLANES, t), F32)),
        grid=(t // tm,),
        in_specs=[act(), act(), act(), act(),
                  pl.BlockSpec((tm, d), lambda i: (i, 0)),
                  const((d, d)), const((1, d)), const((1, d)),
                  const((N_EXPERTS, d)), const((N_EXPERTS, LANES))],
        out_specs=(pl.BlockSpec((tm, d), lambda i: (i, 0)),
                   pl.BlockSpec((SUBLANES, tm), lambda i: (0, i)),
                   pl.BlockSpec((SUBLANES, tm), lambda i: (0, i))),
        compiler_params=_cparams("parallel"),
        name="out_proj_router",
    )(ya, yb, yc, yd, x, w_out, ln_g.reshape(1, d), ln_b.reshape(1, d), router_wT, router_b)


def _row_copy(src_hbm, tok_ref, dst_ref, sem, r, base=0):
    tok = tok_ref[0, 0, base + r]
    return pltpu.make_async_copy(src_hbm.at[pl.ds(tok, 1), :], dst_ref.at[pl.ds(r, 1), :], sem)


def _moe_kernel(blk_exp_ref, nused_ref, tok_ref, tokn_ref, x_hbm, wg_ref, wu_ref, wd_ref, y_ref,
                xbuf_ref, sem_ref):
    i = pl.program_id(0)
    nused = nused_ref[0]
    slot = i % 2

    def issue(tref, s):
        def body(r, carry):
            _row_copy(x_hbm, tref, xbuf_ref.at[s], sem_ref.at[s], r).start()
            return carry
        lax.fori_loop(0, MOE_BLOCK, body, 0)

    def wait(tref, s):
        def body(r, carry):
            _row_copy(x_hbm, tref, xbuf_ref.at[s], sem_ref.at[s], r).wait()
            return carry
        lax.fori_loop(0, MOE_BLOCK, body, 0)

    @pl.when(i == 0)
    def _():
        issue(tok_ref, 0)

    @pl.when(i + 1 < nused)
    def _():
        issue(tokn_ref, 1 - slot)

    @pl.when(i < nused)
    def _():
        wait(tok_ref, slot)
        x = xbuf_ref[slot].astype(BF16)
        hg = jnp.dot(x, wg_ref[...], preferred_element_type=F32)
        hu = jnp.dot(x, wu_ref[...], preferred_element_type=F32)
        hid = (_silu(hg) * hu).astype(BF16)
        y_ref[...] = jnp.dot(hid, wd_ref[...], preferred_element_type=F32)

    @pl.when(i >= nused)
    def _():
        y_ref[...] = jnp.zeros(y_ref.shape, F32)


def _moe_experts(x1, slot_tok, blk_exp, nused, wg, wu, wd):
    t, d = x1.shape
    n_blk = slot_tok.shape[0]
    tok_spec = lambda off: pl.BlockSpec((1, 1, MOE_BLOCK),
                                        lambda i, be, nu: (jnp.minimum(i + off, n_blk - 1), 0, 0),
                                        memory_space=pltpu.SMEM)
    return pl.pallas_call(
        _moe_kernel,
        out_shape=jax.ShapeDtypeStruct((n_blk * MOE_BLOCK, d), F32),
        grid_spec=pltpu.PrefetchScalarGridSpec(
            num_scalar_prefetch=2,
            grid=(n_blk,),
            in_specs=[tok_spec(0), tok_spec(1),
                      pl.BlockSpec(memory_space=pl.ANY),
                      pl.BlockSpec((None, d, D_EXPERT), lambda i, be, nu: (be[i], 0, 0)),
                      pl.BlockSpec((None, d, D_EXPERT), lambda i, be, nu: (be[i], 0, 0)),
                      pl.BlockSpec((None, D_EXPERT, d), lambda i, be, nu: (be[i], 0, 0))],
            out_specs=pl.BlockSpec((MOE_BLOCK, d), lambda i, be, nu: (i, 0)),
            scratch_shapes=[pltpu.VMEM((2, MOE_BLOCK, d), F32), pltpu.SemaphoreType.DMA((2,))]),
        compiler_params=_cparams("arbitrary"),
        name="moe_experts",
    )(blk_exp, nused, slot_tok, slot_tok, x1, wg, wu, wd)


COMBINE_ROWS = 256


def _combine_kernel(pos_ref, posn_ref, y_hbm, x_ref, gt_ref, g_ref, b_ref, o_ref, ybuf_ref, sem_ref):
    i = pl.program_id(0)
    n = pl.num_programs(0)
    slot = i % 2
    tm = x_ref.shape[0]

    def issue(pref, s):
        def body(r, carry):
            for k in range(TOP_K):
                _row_copy(y_hbm, pref, ybuf_ref.at[s, k], sem_ref.at[s], r, base=k * tm).start()
            return carry
        lax.fori_loop(0, tm, body, 0)

    def wait(pref, s):
        def body(r, carry):
            for k in range(TOP_K):
                _row_copy(y_hbm, pref, ybuf_ref.at[s, k], sem_ref.at[s], r, base=k * tm).wait()
            return carry
        lax.fori_loop(0, tm, body, 0)

    @pl.when(i == 0)
    def _():
        issue(pos_ref, 0)

    @pl.when(i + 1 < n)
    def _():
        issue(posn_ref, 1 - slot)

    wait(pos_ref, slot)
    gt = gt_ref[...]
    moe = ybuf_ref[slot, 0] * gt[:, 0:1] + ybuf_ref[slot, 1] * gt[:, 1:2]
    y = ALPHA * x_ref[...] + moe
    mu = jnp.mean(y, axis=-1, keepdims=True)
    yc = y - mu
    var = jnp.mean(yc * yc, axis=-1, keepdims=True)
    o_ref[...] = yc * lax.rsqrt(var + LN_EPS) * g_ref[...] + b_ref[...]


def _moe_combine(y_pad, pos_tiles, x1, gate_tok, ln_g, ln_b):
    t, d = x1.shape
    tm = min(COMBINE_ROWS, t)
    n = t // tm
    pos_spec = lambda off: pl.BlockSpec((1, 1, TOP_K * tm), lambda i: (jnp.minimum(i + off, n - 1), 0, 0),
                                        memory_space=pltpu.SMEM)
    return pl.pallas_call(
        _combine_kernel,
        out_shape=jax.ShapeDtypeStruct((t, d), F32),
        grid=(n,),
        in_specs=[pos_spec(0), pos_spec(1),
                  pl.BlockSpec(memory_space=pl.ANY),
                  pl.BlockSpec((tm, d), lambda i: (i, 0)),
                  pl.BlockSpec((tm, TOP_K), lambda i: (i, 0)),
                  pl.BlockSpec((1, d), lambda i: (0, 0)),
                  pl.BlockSpec((1, d), lambda i: (0, 0))],
        out_specs=pl.BlockSpec((tm, d), lambda i: (i, 0)),
        scratch_shapes=[pltpu.VMEM((2, TOP_K, tm, d), F32), pltpu.SemaphoreType.DMA((2,))],
        compiler_params=_cparams("arbitrary"),
        name="moe_combine",
    )(pos_tiles, pos_tiles, y_pad, x1, gate_tok, ln_g.reshape(1, d), ln_b.reshape(1, d))


def _routing_tables(eid, t):
    n_slots = t * TOP_K
    e_flat = eid.T.reshape(-1)
    onehot = (e_flat[:, None] == jnp.arange(N_EXPERTS, dtype=I32)[None, :]).astype(I32)
    csum = jnp.cumsum(onehot, axis=0)
    counts = csum[-1]
    rank = jnp.take_along_axis(csum, e_flat[:, None], axis=1)[:, 0] - 1
    padded = ((counts + MOE_BLOCK - 1) // MOE_BLOCK) * MOE_BLOCK
    pends = jnp.cumsum(padded)
    pstarts = pends - padded
    pos = (pstarts[e_flat] + rank).astype(I32)
    n_blk = -(-n_slots // MOE_BLOCK) + N_EXPERTS
    slot_tok = jnp.zeros((n_blk * MOE_BLOCK,), I32).at[pos].set(jnp.arange(n_slots, dtype=I32) // TOP_K)
    blk_exp = jnp.minimum(jnp.searchsorted(pends, jnp.arange(n_blk, dtype=I32) * MOE_BLOCK, side='right'),
                          N_EXPERTS - 1).astype(I32)
    nused = (pends[-1] // MOE_BLOCK).astype(I32).reshape(1)
    return slot_tok.reshape(n_blk, 1, MOE_BLOCK), blk_exp, nused, pos


def _rope_tables(seq):
    half = DIFF_QK_DIM // 2
    inv = 1.0 / (ROPE_THETA ** (jnp.arange(0, DIFF_QK_DIM, 2, dtype=F32) / DIFF_QK_DIM))
    ang = jnp.arange(seq, dtype=F32)[:, None] * inv[None, :]
    cos, sin = jnp.cos(ang), jnp.sin(ang)
    cos_t = jnp.concatenate([cos, cos, cos, cos], axis=1)
    sin_t = jnp.concatenate([-sin, sin, -sin, sin], axis=1)
    return cos_t, sin_t


def _split_w_in(w):
    z, xbc, dt, q, k, v, conf, sc = jnp.split(w, [512, 1536, 1544, 2056, 2568, 3080, 4104], axis=1)
    w_main = jnp.concatenate([xbc, conf, sc, z, q, k, v], axis=1).astype(BF16)
    w_dt = jnp.pad(dt, ((0, 0), (0, LANES - SSD_HEADS))).astype(BF16)
    w_dtT = jnp.pad(dt.T, ((0, 16 - SSD_HEADS), (0, 0))).astype(BF16)
    return w_main, w_dt, w_dtT


def kernel(x, ln_in_g, ln_in_b, w_in, ssd_conv_w, ssd_conv_b, ssd_dt_bias, ssd_a_log, ssd_d, ssd_norm_g, diff_lq1, diff_lk1, diff_lq2, diff_lk2, diff_norm_g, conf_dw_w, conf_dw_b, conf_ln_g, conf_ln_b, conf_pw_w, conf_pw_b, sc_conv_w, w_out, ln1_g, ln1_b, router_w, router_bias, moe_w_gate, moe_w_up, moe_w_down, ln2_g, ln2_b):
    b, s, d = x.shape
    t = b * s
    cos_t, sin_t = _rope_tables(s)
    router_wT = router_w.T
    router_b = jnp.broadcast_to(router_bias.reshape(N_EXPERTS, 1), (N_EXPERTS, LANES))
    xf = _layer_norm(x.reshape(t, d), ln_in_g, ln_in_b)
    for l in range(DEPTH):
        lambda_init = 0.8 - 0.6 * math.exp(-0.3 * l)
        w_main, w_dt, w_dtT = _split_w_in(w_in[l])
        h, dt_raw, dtT_raw = _in_projection(xf, w_main, w_dt, w_dtT, cos_t, sin_t, s)
        ya = _ssd_mixer(h, dt_raw, dtT_raw, ssd_conv_w[l], ssd_conv_b[l], ssd_dt_bias[l], ssd_a_log[l],
                        ssd_d[l], ssd_norm_g[l], b, s)
        yb = _diff_attention(h, diff_lq1[l], diff_lk1[l], diff_lq2[l], diff_lk2[l], diff_norm_g[l],
                             lambda_init, b, s)
        yc, yd = _conv_mixers(h, conf_dw_w[l], conf_dw_b[l], conf_ln_g[l], conf_ln_b[l], conf_pw_w[l],
                              conf_pw_b[l], sc_conv_w[l], b, s)
        x1, eid, gate = _out_projection(ya, yb, yc, yd, xf, w_out[l].astype(BF16), ln1_g[l], ln1_b[l],
                                        router_wT, router_b)
        slot_tok, blk_exp, nused, pos = _routing_tables(eid[0:TOP_K], t)
        y_pad = _moe_experts(x1, slot_tok, blk_exp, nused, moe_w_gate[l].astype(BF16),
                             moe_w_up[l].astype(BF16), moe_w_down[l].astype(BF16))
        tmc = min(COMBINE_ROWS, t)
        pos_tiles = pos.reshape(t // tmc, tmc, TOP_K).transpose(0, 2, 1).reshape(t // tmc, 1, TOP_K * tmc)
        xf = _moe_combine(y_pad, pos_tiles, x1, gate[0:TOP_K].T, ln2_g[l], ln2_b[l])
    return xf.reshape(b, s, d)
```

```python
import functools
import math

import numpy as np
import jax
import jax.numpy as jnp
from jax import lax
from jax.experimental import pallas as pl
from jax.experimental.pallas import tpu as pltpu

F32 = jnp.float32
BF16 = jnp.bfloat16
I32 = jnp.int32

D_MODEL = 2048
DEPTH = 2
CHUNK = 64
GROUP_W = 512
SSD_HEADS = 8
SSD_STATE = 128
SSD_CONV = 4
SSD_XBC = 1024
DIFF_HEADS = 4
DIFF_QK_DIM = 64
DIFF_V_DIM = 128
ROPE_THETA = 10000.0
CONF_KERNEL = 31
SC_KERNEL = 3
N_EXPERTS = 16
N_GROUPS = 4
EXPERTS_PER_GROUP = 4
TOP_K = 2
D_EXPERT = 1024
MOE_BLOCK = 256
ALPHA = (2 * DEPTH) ** 0.25
LN_EPS = 1e-5
LOG2E = 1.4426950408889634

LANES = 128
SUBLANES = 8
VMEM_LIMIT_BYTES = 56 * 1024 * 1024

COL_XBC, COL_CONF, COL_BG, COL_CG, COL_HH, COL_Z, COL_Q, COL_K, COL_V, COL_DT = 0, 2, 4, 5, 6, 7, 8, 9, 10, 11
H_COLS = 12 * 512
TN = 512
SSD_L = 128


def _cparams(*sem):
    return pltpu.CompilerParams(dimension_semantics=tuple(sem), vmem_limit_bytes=VMEM_LIMIT_BYTES)


def _sigmoid(x):
    return 1.0 / (1.0 + jnp.exp(-x))


def _silu(x):
    return x * _sigmoid(x)


def _softplus(x):
    return jnp.maximum(x, 0.0) + jnp.log(1.0 + jnp.exp(-jnp.abs(x)))


def _split3(v):
    hi = v.astype(BF16)
    r = v - hi.astype(F32)
    mid = r.astype(BF16)
    lo = (r - mid.astype(F32)).astype(BF16)
    return hi, mid, lo


def _dot_exact_rhs(v, m):
    return sum(jnp.dot(p, m, preferred_element_type=F32) for p in _split3(v))


def _dot_exact_lhs(m, v):
    return sum(jnp.dot(m, p, preferred_element_type=F32) for p in _split3(v))


def _dot_nt(a, b):
    return lax.dot_general(a, b, (((1,), (1,)), ((), ())), preferred_element_type=F32)


def _ln_kernel(x_ref, g_ref, b_ref, o_ref):
    x = x_ref[...]
    mu = jnp.mean(x, axis=-1, keepdims=True)
    xc = x - mu
    var = jnp.mean(xc * xc, axis=-1, keepdims=True)
    o_ref[...] = xc * lax.rsqrt(var + LN_EPS) * g_ref[...] + b_ref[...]


def _layer_norm(x, g, b):
    t, d = x.shape
    tm = min(512, t)
    return pl.pallas_call(
        _ln_kernel,
        out_shape=jax.ShapeDtypeStruct((t, d), F32),
        grid=(t // tm,),
        in_specs=[pl.BlockSpec((tm, d), lambda i: (i, 0)),
                  pl.BlockSpec((1, d), lambda i: (0, 0)),
                  pl.BlockSpec((1, d), lambda i: (0, 0))],
        out_specs=pl.BlockSpec((tm, d), lambda i: (i, 0)),
        compiler_params=_cparams("parallel"),
        name="entry_ln",
    )(x, g.reshape(1, d), b.reshape(1, d))


def _inproj_kernel(x_ref, w_ref, cos_ref, sin_ref, h_ref, dt_ref, xb_ref):
    j = pl.program_id(1)
    tm = x_ref.shape[0]

    @pl.when(j == 0)
    def _():
        xb_ref[...] = x_ref[...].astype(BF16)

    acc = jnp.dot(xb_ref[...], w_ref[...], preferred_element_type=F32)
    is_rope = jnp.logical_or(j == COL_Q, j == COL_K)

    @pl.when(jnp.logical_not(is_rope))
    def _():
        h_ref[...] = acc.astype(BF16)

    @pl.when(j == COL_DT)
    def _():
        dt_ref[...] = acc[:, 0:LANES]

    @pl.when(is_rope)
    def _():
        scale = jnp.where(j == COL_Q, LOG2E * DIFF_QK_DIM ** -0.5, 1.0).astype(F32)
        c = cos_ref[...] * scale
        s = sin_ref[...] * scale
        lane = lax.broadcasted_iota(I32, (tm, LANES), 1)
        first_half = (lane % DIFF_QK_DIM) < (DIFF_QK_DIM // 2)
        for hh in range(TN // LANES):
            a = acc[:, hh * LANES:(hh + 1) * LANES]
            rot = jnp.where(first_half, pltpu.roll(a, LANES - 32, 1), pltpu.roll(a, 32, 1))
            h_ref[:, hh * LANES:(hh + 1) * LANES] = (a * c + rot * s).astype(BF16)


def _in_projection(x, w_main, cos_t, sin_t, seq):
    t, d = x.shape
    tm = min(1024, seq)
    nseq = seq // tm
    return pl.pallas_call(
        _inproj_kernel,
        out_shape=(jax.ShapeDtypeStruct((t, H_COLS), BF16),
                   jax.ShapeDtypeStruct((t, LANES), F32)),
        grid=(t // tm, H_COLS // TN),
        in_specs=[pl.BlockSpec((tm, d), lambda i, j: (i, 0)),
                  pl.BlockSpec((d, TN), lambda i, j: (0, j)),
                  pl.BlockSpec((tm, LANES), lambda i, j: (i % nseq, 0)),
                  pl.BlockSpec((tm, LANES), lambda i, j: (i % nseq, 0))],
        out_specs=(pl.BlockSpec((tm, TN), lambda i, j: (i, j)),
                   pl.BlockSpec((tm, LANES), lambda i, j: (i, 0))),
        scratch_shapes=[pltpu.VMEM((tm, d), BF16)],
        compiler_params=_cparams("arbitrary", "arbitrary"),
        name="in_proj",
    )(x, w_main, cos_t, sin_t)


def _ssd_kernel(xbc_ref, z_ref, dt_ref, cw_ref, cb_ref, dtb_ref, alog_ref, dtbT_ref, alogT_ref,
                dsk_ref, ng_ref, e64_ref, e128_ref, tri_ref, triT_ref, y_ref, xpad_ref, hst_ref):
    c = pl.program_id(1)
    L = SSD_L

    @pl.when(c == 0)
    def _():
        xpad_ref[0:SUBLANES, :] = jnp.zeros((SUBLANES, SSD_XBC), F32)
        hst_ref[...] = jnp.zeros(hst_ref.shape, F32)

    cur = xbc_ref[...].astype(F32)
    xpad_ref[SUBLANES:SUBLANES + L, :] = cur
    conv = cb_ref[...]
    for k in range(SSD_CONV):
        conv = conv + cw_ref[k:k + 1, :] * xpad_ref[pl.ds(SUBLANES - (SSD_CONV - 1) + k, L), :]
    xpad_ref[0:SUBLANES, :] = cur[L - SUBLANES:L, :]
    xbc = _silu(conv)
    X = xbc[:, 0:GROUP_W]
    Bm = xbc[:, GROUP_W:GROUP_W + 2 * SSD_STATE]
    Cm = xbc[:, GROUP_W + 2 * SSD_STATE:SSD_XBC]

    dt_raw = dt_ref[...]
    dt = _softplus(dt_raw + dtb_ref[...])
    acs = _dot_exact_lhs(tri_ref[...], dt * (-jnp.exp(alog_ref[...])))
    dtT = _softplus(dt_raw.T[0:16, :] + dtbT_ref[...])
    acsT = _dot_exact_rhs(dtT * (-jnp.exp(alogT_ref[...])), triT_ref[...])
    dt64 = _dot_exact_rhs(dt, e64_ref[...])
    acs64 = _dot_exact_rhs(acs, e64_ref[...])
    acs128 = _dot_exact_rhs(acs, e128_ref[...])

    last = acs64[L - 1:L, :]
    in_decay = jnp.exp(acs64)
    decay = jnp.exp(last - acs64)
    chunk_decay = jnp.exp(last)
    Xdt = X * dt64
    Xd = (Xdt * decay).astype(BF16)
    Xdt_b = Xdt.astype(BF16)

    row = lax.broadcasted_iota(I32, (L, L), 0)
    col = lax.broadcasted_iota(I32, (L, L), 1)
    causal = row >= col
    lane = lax.broadcasted_iota(I32, (L, LANES), 1)
    low_half = lane < 64

    y_parts = []
    for g in range(2):
        Bg = Bm[:, g * SSD_STATE:(g + 1) * SSD_STATE]
        Cg = Cm[:, g * SSD_STATE:(g + 1) * SSD_STATE].astype(BF16)
        cb = _dot_nt(Cg, Bg.astype(BF16))
        hprev = hst_ref[g]
        y_off = jnp.dot(Cg, hprev.astype(BF16), preferred_element_type=F32) * in_decay[:, g * 256:(g + 1) * 256]
        for pr in range(2):
            xp = Xdt_b[:, (2 * g + pr) * LANES:(2 * g + pr + 1) * LANES]
            ys = []
            for hh in range(2):
                h = 4 * g + 2 * pr + hh
                seg = acs128[:, h * LANES:(h + 1) * LANES] - acsT[h:h + 1, :]
                lmat = jnp.exp(jnp.where(causal, seg, -jnp.inf))
                w = (cb * lmat).astype(BF16)
                ys.append(jnp.dot(w, xp, preferred_element_type=F32))
            y_parts.append(jnp.where(low_half, ys[0], ys[1]) + y_off[:, pr * LANES:(pr + 1) * LANES])
        states = jnp.dot(Bg.T.astype(BF16), Xd[:, g * 256:(g + 1) * 256], preferred_element_type=F32)
        hst_ref[g] = hprev * chunk_decay[:, g * 256:(g + 1) * 256] + states

    z = z_ref[...].astype(F32)
    gate = _silu(z)
    for g in range(2):
        yg = jnp.concatenate(y_parts[2 * g:2 * g + 2], axis=1) + dsk_ref[:, g * 256:(g + 1) * 256] * X[:, g * 256:(g + 1) * 256]
        yg = yg * gate[:, g * 256:(g + 1) * 256]
        ms = jnp.mean(yg * yg, axis=-1, keepdims=True)
        y_ref[:, g * 256:(g + 1) * 256] = (yg * lax.rsqrt(ms + LN_EPS) * ng_ref[:, g * 256:(g + 1) * 256]).astype(BF16)


def _ssd_constants():
    e64 = np.zeros((LANES, 512), np.float32)
    e128 = np.zeros((LANES, 1024), np.float32)
    for h in range(SSD_HEADS):
        e64[h, 64 * h:64 * (h + 1)] = 1.0
        e128[h, 128 * h:128 * (h + 1)] = 1.0
    tri = np.tril(np.ones((SSD_L, SSD_L), np.float32))
    return (jnp.asarray(e64, BF16), jnp.asarray(e128, BF16), jnp.asarray(tri, BF16), jnp.asarray(tri.T, BF16))


def _ssd_mixer(h, dt_raw, conv_w, conv_b, dt_bias, a_log, d_skip, norm_g, batch, seq):
    t = h.shape[0]
    L = SSD_L
    nc = seq // L
    e64, e128, tri, triT = _ssd_constants()
    pad_l = lambda v: jnp.pad(v.reshape(1, SSD_HEADS), ((0, 0), (0, LANES - SSD_HEADS)))
    pad_t = lambda v: jnp.pad(jnp.broadcast_to(v.reshape(SSD_HEADS, 1), (SSD_HEADS, L)), ((0, 16 - SSD_HEADS), (0, 0)))
    rb = lambda b, c: b * nc + c
    const = lambda shape: pl.BlockSpec(shape, lambda b, c: (0,) * len(shape))
    return pl.pallas_call(
        _ssd_kernel,
        out_shape=jax.ShapeDtypeStruct((t, GROUP_W), BF16),
        grid=(batch, nc),
        in_specs=[pl.BlockSpec((L, SSD_XBC), lambda b, c: (rb(b, c), COL_XBC // 2)),
                  pl.BlockSpec((L, GROUP_W), lambda b, c: (rb(b, c), COL_Z)),
                  pl.BlockSpec((L, LANES), lambda b, c: (rb(b, c), 0)),
                  const((SSD_CONV, SSD_XBC)), const((1, SSD_XBC)),
                  const((1, LANES)), const((1, LANES)), const((16, L)), const((16, L)),
                  const((1, GROUP_W)), const((1, GROUP_W)),
                  const((LANES, 512)), const((LANES, 1024)), const((L, L)), const((L, L))],
        out_specs=pl.BlockSpec((L, GROUP_W), lambda b, c: (rb(b, c), 0)),
        scratch_shapes=[pltpu.VMEM((SUBLANES + L, SSD_XBC), F32),
                        pltpu.VMEM((2, SSD_STATE, 256), F32)],
        compiler_params=_cparams("arbitrary", "arbitrary"),
        name="ssd_mixer",
    )(h, h, dt_raw, conv_w, conv_b.reshape(1, SSD_XBC),
      pad_l(dt_bias), pad_l(a_log), pad_t(dt_bias), pad_t(a_log),
      jnp.repeat(d_skip, 64).reshape(1, GROUP_W), norm_g.reshape(1, GROUP_W), e64, e128, tri, triT)


def _attn_kernel(q_ref, k_ref, v_ref, bias_ref, lq1_ref, lk1_ref, lq2_ref, lk2_ref, g_ref, o_ref,
                 m_ref, l_ref, acc_ref, *, lambda_init):
    qi = pl.program_id(2)
    tq = q_ref.shape[0]
    tk = tq
    q = q_ref[...]
    lane = lax.broadcasted_iota(I32, (tq, LANES), 1)
    zero = jnp.zeros_like(q)
    qs = (jnp.where(lane < DIFF_QK_DIM, q, zero), jnp.where(lane >= DIFF_QK_DIM, q, zero))
    m_ref[...] = jnp.full(m_ref.shape, -jnp.inf, F32)
    l_ref[...] = jnp.zeros(l_ref.shape, F32)
    acc_ref[...] = jnp.zeros(acc_ref.shape, F32)

    def tile(off, bias):
        k = k_ref[pl.ds(off, tk), :]
        v = v_ref[pl.ds(off, tk), :]
        for m in range(2):
            s = _dot_nt(qs[m], k)
            if bias is not None:
                s = s + bias
            m_prev = m_ref[m]
            m_new = jnp.maximum(m_prev, jnp.max(s, axis=-1, keepdims=True))
            alpha = jnp.exp2(m_prev - m_new)
            p = jnp.exp2(s - jnp.concatenate([m_new] * (tk // LANES), axis=1))
            psum = p[:, 0:LANES]
            for c in range(1, tk // LANES):
                psum = psum + p[:, c * LANES:(c + 1) * LANES]
            l_ref[m] = alpha * l_ref[m] + psum
            acc_ref[m] = alpha * acc_ref[m] + jnp.dot(p.astype(BF16), v, preferred_element_type=F32)
            m_ref[m] = m_new

    def body(j, carry):
        tile(pl.multiple_of(j * tk, tk), None)
        return carry

    lax.fori_loop(0, qi, body, 0)
    tile(pl.multiple_of(qi * tk, tk), bias_ref[...])

    lam = (jnp.exp(jnp.sum(lq1_ref[...] * lk1_ref[...], axis=-1, keepdims=True))
           - jnp.exp(jnp.sum(lq2_ref[...] * lk2_ref[...], axis=-1, keepdims=True)) + lambda_init)
    l0 = jnp.sum(l_ref[0], axis=-1, keepdims=True)
    l1 = jnp.sum(l_ref[1], axis=-1, keepdims=True)
    o = acc_ref[0] / l0 - lam * (acc_ref[1] / l1)
    ms = jnp.mean(o * o, axis=-1, keepdims=True)
    o_ref[...] = (o * lax.rsqrt(ms + LN_EPS) * g_ref[...] * (1.0 - lambda_init)).astype(BF16)


def _chunk_mask_bias(tq):
    r = np.arange(tq)[:, None] // CHUNK
    c = np.arange(tq)[None, :] // CHUNK
    return jnp.asarray(np.where(c <= r, 0.0, -np.inf), F32)


def _diff_attention(h, lq1, lk1, lq2, lk2, norm_g, lambda_init, batch, seq):
    t = h.shape[0]
    tq = min(512, seq)
    nq = seq // tq
    qcol = COL_Q * 4
    kcol = COL_K * 4
    vcol = COL_V * 4
    vec = lambda n: pl.BlockSpec((1, n), lambda b, hd, qi: (0, 0))
    return pl.pallas_call(
        functools.partial(_attn_kernel, lambda_init=lambda_init),
        out_shape=jax.ShapeDtypeStruct((t, GROUP_W), BF16),
        grid=(batch, DIFF_HEADS, nq),
        in_specs=[pl.BlockSpec((tq, LANES), lambda b, hd, qi: (b * nq + qi, qcol + hd)),
                  pl.BlockSpec((seq, LANES), lambda b, hd, qi: (b, kcol + hd)),
                  pl.BlockSpec((seq, LANES), lambda b, hd, qi: (b, vcol + hd)),
                  pl.BlockSpec((tq, tq), lambda b, hd, qi: (0, 0)),
                  vec(DIFF_QK_DIM), vec(DIFF_QK_DIM), vec(DIFF_QK_DIM), vec(DIFF_QK_DIM), vec(DIFF_V_DIM)],
        out_specs=pl.BlockSpec((tq, LANES), lambda b, hd, qi: (b * nq + qi, hd)),
        scratch_shapes=[pltpu.VMEM((2, tq, LANES), F32), pltpu.VMEM((2, tq, LANES), F32),
                        pltpu.VMEM((2, tq, DIFF_V_DIM), F32)],
        compiler_params=_cparams("arbitrary", "arbitrary", "arbitrary"),
        name="diff_attn",
    )(h, h, h, _chunk_mask_bias(tq), lq1.reshape(1, -1), lk1.reshape(1, -1), lq2.reshape(1, -1),
      lk2.reshape(1, -1), norm_g.reshape(1, -1))


CONF_HALO = 32
SC_HALO = 16
CONV_ROWS = 64


def _conv_kernel(u_ref, uh_ref, bg_ref, cg_ref, hh_ref, cgh_ref, hhh_ref,
                 dww_ref, dwb_ref, lng_ref, lnb_ref, pww_ref, pwb_ref, scw_ref,
                 yc_ref, yd_ref, hbuf_ref, pbuf_ref, cbuf_ref):
    i = pl.program_id(1)
    tm = u_ref.shape[0]

    def glu(u):
        u = u.astype(F32)
        return u[:, 0:GROUP_W] * _sigmoid(u[:, GROUP_W:2 * GROUP_W])

    first = (i == 0)
    hbuf_ref[0:CONF_HALO, :] = jnp.where(first, 0.0, glu(uh_ref[...]))
    hbuf_ref[CONF_HALO:CONF_HALO + tm, :] = glu(u_ref[...])
    pbuf_ref[0:SC_HALO, :] = jnp.where(first, 0.0, cgh_ref[...].astype(F32) * hhh_ref[...].astype(F32))
    pbuf_ref[SC_HALO:SC_HALO + tm, :] = cg_ref[...].astype(F32) * hh_ref[...].astype(F32)

    for r0 in range(0, tm, CONV_ROWS):
        acc = jnp.zeros((CONV_ROWS, GROUP_W), F32) + dwb_ref[...]
        for k in range(CONF_KERNEL):
            acc = acc + dww_ref[k:k + 1, :] * hbuf_ref[pl.ds(r0 + CONF_HALO - (CONF_KERNEL - 1) + k, CONV_ROWS), :]
        cbuf_ref[r0:r0 + CONV_ROWS, :] = acc
    hc = cbuf_ref[...]
    mu = jnp.mean(hc, axis=-1, keepdims=True)
    xc = hc - mu
    var = jnp.mean(xc * xc, axis=-1, keepdims=True)
    hn = _silu(xc * lax.rsqrt(var + LN_EPS) * lng_ref[...] + lnb_ref[...])
    yc = jnp.dot(hn.astype(BF16), pww_ref[...], preferred_element_type=F32) + pwb_ref[...]
    yc_ref[...] = yc.astype(BF16)

    sc = jnp.zeros((tm, GROUP_W), F32)
    for k in range(SC_KERNEL):
        sc = sc + scw_ref[k:k + 1, :] * pbuf_ref[pl.ds(SC_HALO - (SC_KERNEL - 1) + k, tm), :]
    yd_ref[...] = (bg_ref[...].astype(F32) * sc).astype(BF16)


def _conv_mixers(h, dw_w, dw_b, ln_g, ln_b, pw_w, pw_b, sc_w, batch, seq):
    t = h.shape[0]
    tm = min(512, seq)
    nt = seq // tm
    rb = lambda b, i: b * nt + i
    halo = lambda rows: (lambda b, i: jnp.maximum(rb(b, i) * (tm // rows) - 1, 0))
    hc, hs = halo(CONF_HALO), halo(SC_HALO)
    const = lambda shape: pl.BlockSpec(shape, lambda b, i: (0,) * len(shape))
    return pl.pallas_call(
        _conv_kernel,
        out_shape=(jax.ShapeDtypeStruct((t, GROUP_W), BF16), jax.ShapeDtypeStruct((t, GROUP_W), BF16)),
        grid=(batch, nt),
        in_specs=[pl.BlockSpec((tm, 2 * GROUP_W), lambda b, i: (rb(b, i), COL_CONF // 2)),
                  pl.BlockSpec((CONF_HALO, 2 * GROUP_W), lambda b, i: (hc(b, i), COL_CONF // 2)),
                  pl.BlockSpec((tm, GROUP_W), lambda b, i: (rb(b, i), COL_BG)),
                  pl.BlockSpec((tm, GROUP_W), lambda b, i: (rb(b, i), COL_CG)),
                  pl.BlockSpec((tm, GROUP_W), lambda b, i: (rb(b, i), COL_HH)),
                  pl.BlockSpec((SC_HALO, GROUP_W), lambda b, i: (hs(b, i), COL_CG)),
                  pl.BlockSpec((SC_HALO, GROUP_W), lambda b, i: (hs(b, i), COL_HH)),
                  const((CONF_KERNEL, GROUP_W)), const((1, GROUP_W)), const((1, GROUP_W)), const((1, GROUP_W)),
                  const((GROUP_W, GROUP_W)), const((1, GROUP_W)), const((SC_KERNEL, GROUP_W))],
        out_specs=(pl.BlockSpec((tm, GROUP_W), lambda b, i: (rb(b, i), 0)),
                   pl.BlockSpec((tm, GROUP_W), lambda b, i: (rb(b, i), 0))),
        scratch_shapes=[pltpu.VMEM((CONF_HALO + tm, GROUP_W), F32),
                        pltpu.VMEM((SC_HALO + tm, GROUP_W), F32),
                        pltpu.VMEM((tm, GROUP_W), F32)],
        compiler_params=_cparams("arbitrary", "arbitrary"),
        name="conv_mixers",
    )(h, h, h, h, h, h, h, dw_w, dw_b.reshape(1, -1), ln_g.reshape(1, -1), ln_b.reshape(1, -1),
      pw_w.astype(BF16), pw_b.reshape(1, -1), sc_w)


def _first_max4(vals):
    m1 = jnp.maximum(jnp.maximum(vals[0], vals[1]), jnp.maximum(vals[2], vals[3]))
    i1 = jnp.where(vals[0] == m1, 0, jnp.where(vals[1] == m1, 1, jnp.where(vals[2] == m1, 2, 3)))
    rest = [jnp.where(i1 == j, -jnp.inf, vals[j]) for j in range(4)]
    m2 = jnp.maximum(jnp.maximum(rest[0], rest[1]), jnp.maximum(rest[2], rest[3]))
    i2 = jnp.where(rest[0] == m2, 0, jnp.where(rest[1] == m2, 1, jnp.where(rest[2] == m2, 2, 3)))
    return m1, i1, m2, i2


def _outproj_kernel(ya_ref, yb_ref, yc_ref, yd_ref, x_ref, w_ref, g_ref, b_ref, x1_ref):
    mix = jnp.dot(ya_ref[...], w_ref[0:GROUP_W, :], preferred_element_type=F32)
    mix = mix + jnp.dot(yb_ref[...], w_ref[GROUP_W:2 * GROUP_W, :], preferred_element_type=F32)
    mix = mix + jnp.dot(yc_ref[...], w_ref[2 * GROUP_W:3 * GROUP_W, :], preferred_element_type=F32)
    mix = mix + jnp.dot(yd_ref[...], w_ref[3 * GROUP_W:4 * GROUP_W, :], preferred_element_type=F32)
    y = ALPHA * x_ref[...] + mix
    mu = jnp.mean(y, axis=-1, keepdims=True)
    yc = y - mu
    var = jnp.mean(yc * yc, axis=-1, keepdims=True)
    x1_ref[...] = yc * lax.rsqrt(var + LN_EPS) * g_ref[...] + b_ref[...]


def _router_kernel(x_ref, w2_ref, w1_ref, rb_ref, eid_ref, gate_ref):
    tm = x_ref.shape[0]
    x = x_ref[...]
    xh = x.astype(BF16)
    xm = (x - xh.astype(F32)).astype(BF16)
    r = (jnp.dot(xh, w2_ref[...], preferred_element_type=F32)
         + jnp.dot(xm, w1_ref[...], preferred_element_type=F32))
    r = r + pltpu.roll(r, LANES - N_EXPERTS, 1)
    logits = r.T[0:N_EXPERTS, :]
    aff = _sigmoid(logits)
    sel = aff + jnp.concatenate([rb_ref[...]] * (tm // LANES), axis=1)
    rows = [sel[e:e + 1, :] for e in range(N_EXPERTS)]
    arow = [aff[e:e + 1, :] for e in range(N_EXPERTS)]
    tops = [_first_max4(rows[4 * g:4 * g + 4]) for g in range(N_GROUPS)]
    score = [tp[0] + tp[2] for tp in tops]
    best = jnp.maximum(jnp.maximum(score[0], score[1]), jnp.maximum(score[2], score[3]))
    grp = jnp.where(score[0] == best, 0, jnp.where(score[1] == best, 1, jnp.where(score[2] == best, 2, 3)))
    pick = lambda idx: jnp.where(grp == 0, tops[0][idx], jnp.where(grp == 1, tops[1][idx],
                                 jnp.where(grp == 2, tops[2][idx], tops[3][idx])))
    e1 = grp * EXPERTS_PER_GROUP + pick(1)
    e2 = grp * EXPERTS_PER_GROUP + pick(3)
    a1 = jnp.zeros_like(best)
    a2 = jnp.zeros_like(best)
    for e in range(N_EXPERTS):
        a1 = jnp.where(e1 == e, arow[e], a1)
        a2 = jnp.where(e2 == e, arow[e], a2)
    den = a1 + a2
    zi = jnp.zeros((SUBLANES - 2, tm), I32)
    zf = jnp.zeros((SUBLANES - 2, tm), F32)
    eid_ref[...] = jnp.concatenate([e1.astype(I32), e2.astype(I32), zi], axis=0)
    gate_ref[...] = jnp.concatenate([a1 / den, a2 / den, zf], axis=0)


def _out_projection(ya, yb, yc, yd, x, w_out, ln_g, ln_b):
    t, d = x.shape
    tm = min(512, t)
    act = lambda: pl.BlockSpec((tm, GROUP_W), lambda i: (i, 0))
    const = lambda shape: pl.BlockSpec(shape, lambda i: (0,) * len(shape))
    return pl.pallas_call(
        _outproj_kernel,
        out_shape=jax.ShapeDtypeStruct((t, d), F32),
        grid=(t // tm,),
        in_specs=[act(), act(), act(), act(),
                  pl.BlockSpec((tm, d), lambda i: (i, 0)),
                  const((d, d)), const((1, d)), const((1, d))],
        out_specs=pl.BlockSpec((tm, d), lambda i: (i, 0)),
        compiler_params=_cparams("parallel"),
        name="out_proj",
    )(ya, yb, yc, yd, x, w_out, ln_g.reshape(1, d), ln_b.reshape(1, d))


def _router(x1, router_w, router_bias):
    t, d = x1.shape
    tm = min(512, t)
    w_hi = router_w.astype(BF16)
    w_mid = (router_w - w_hi.astype(F32)).astype(BF16)
    w2 = jnp.concatenate([w_hi, w_mid, jnp.zeros((d, LANES - 2 * N_EXPERTS), BF16)], axis=1)
    w1 = jnp.concatenate([w_hi, jnp.zeros((d, LANES - N_EXPERTS), BF16)], axis=1)
    router_b = jnp.broadcast_to(router_bias.reshape(N_EXPERTS, 1), (N_EXPERTS, LANES))
    const = lambda shape: pl.BlockSpec(shape, lambda i: (0,) * len(shape))
    return pl.pallas_call(
        _router_kernel,
        out_shape=(jax.ShapeDtypeStruct((SUBLANES, t), I32), jax.ShapeDtypeStruct((SUBLANES, t), F32)),
        grid=(t // tm,),
        in_specs=[pl.BlockSpec((tm, d), lambda i: (i, 0)),
                  const((d, LANES)), const((d, LANES)), const((N_EXPERTS, LANES))],
        out_specs=(pl.BlockSpec((SUBLANES, tm), lambda i: (0, i)),
                   pl.BlockSpec((SUBLANES, tm), lambda i: (0, i))),
        compiler_params=_cparams("parallel"),
        name="router",
    )(x1, w2, w1, router_b)


def _row_copy(src_hbm, tok_ref, dst_ref, sem, r, base=0):
    tok = tok_ref[0, 0, base + r]
    return pltpu.make_async_copy(src_hbm.at[pl.ds(tok, 1), :], dst_ref.at[pl.ds(r, 1), :], sem)


def _moe_kernel(blk_exp_ref, nused_ref, tok_ref, tokn_ref, x_hbm, wg_ref, wu_ref, wd_ref, y_ref,
                xbuf0_ref, xbuf1_ref, sem_ref):
    i = pl.program_id(0)
    nused = nused_ref[0]
    bufs = (xbuf0_ref, xbuf1_ref)

    def issue(tref, s):
        for r in range(MOE_BLOCK):
            _row_copy(x_hbm, tref, bufs[s], sem_ref.at[s], r).start()

    def wait(tref, s):
        for r in range(MOE_BLOCK):
            _row_copy(x_hbm, tref, bufs[s], sem_ref.at[s], r).wait()

    @pl.when(i == 0)
    def _():
        issue(tok_ref, 0)

    for s in range(2):
        @pl.when(jnp.logical_and(i < nused, i % 2 == s))
        def _():
            wait(tok_ref, s)
            issue(tokn_ref, 1 - s)
            x = bufs[s][...].astype(BF16)
            hg = jnp.dot(x, wg_ref[...], preferred_element_type=F32)
            hu = jnp.dot(x, wu_ref[...], preferred_element_type=F32)
            hid = (_silu(hg) * hu).astype(BF16)
            y_ref[...] = jnp.dot(hid, wd_ref[...], preferred_element_type=F32)

        @pl.when(jnp.logical_and(i + 1 == nused, i % 2 == s))
        def _():
            wait(tokn_ref, 1 - s)

    @pl.when(i >= nused)
    def _():
        y_ref[...] = jnp.zeros(y_ref.shape, F32)


def _moe_experts(x1, slot_tok, blk_exp, nused, wg, wu, wd):
    t, d = x1.shape
    n_blk = slot_tok.shape[0]
    tok_spec = lambda off: pl.BlockSpec((1, 1, MOE_BLOCK),
                                        lambda i, be, nu: (jnp.minimum(i + off, nu[0] - 1), 0, 0),
                                        memory_space=pltpu.SMEM)
    return pl.pallas_call(
        _moe_kernel,
        out_shape=jax.ShapeDtypeStruct((n_blk * MOE_BLOCK, d), F32),
        grid_spec=pltpu.PrefetchScalarGridSpec(
            num_scalar_prefetch=2,
            grid=(n_blk,),
            in_specs=[tok_spec(0), tok_spec(1),
                      pl.BlockSpec(memory_space=pl.ANY),
                      pl.BlockSpec((None, d, D_EXPERT), lambda i, be, nu: (be[i], 0, 0)),
                      pl.BlockSpec((None, d, D_EXPERT), lambda i, be, nu: (be[i], 0, 0)),
                      pl.BlockSpec((None, D_EXPERT, d), lambda i, be, nu: (be[i], 0, 0))],
            out_specs=pl.BlockSpec((MOE_BLOCK, d), lambda i, be, nu: (i, 0)),
            scratch_shapes=[pltpu.VMEM((MOE_BLOCK, d), F32), pltpu.VMEM((MOE_BLOCK, d), F32),
                            pltpu.SemaphoreType.DMA((2,))]),
        compiler_params=_cparams("arbitrary"),
        name="moe_experts",
    )(blk_exp, nused, slot_tok, slot_tok, x1, wg, wu, wd)


COMBINE_ROWS = 256


def _combine_kernel(pos_ref, posn_ref, y_hbm, x_ref, gt_ref, g_ref, b_ref, o_ref, ybuf_ref, sem_ref):
    i = pl.program_id(0)
    n = pl.num_programs(0)
    slot = i % 2
    tm = x_ref.shape[0]

    def issue(pref, s):
        def body(r8, carry):
            for u in range(SUBLANES):
                for k in range(TOP_K):
                    _row_copy(y_hbm, pref, ybuf_ref.at[s, k], sem_ref.at[s], r8 * SUBLANES + u, base=k * tm).start()
            return carry
        lax.fori_loop(0, tm // SUBLANES, body, 0)

    def wait(pref, s):
        def body(r8, carry):
            for u in range(SUBLANES):
                for k in range(TOP_K):
                    _row_copy(y_hbm, pref, ybuf_ref.at[s, k], sem_ref.at[s], r8 * SUBLANES + u, base=k * tm).wait()
            return carry
        lax.fori_loop(0, tm // SUBLANES, body, 0)

    @pl.when(i == 0)
    def _():
        issue(pos_ref, 0)

    @pl.when(i + 1 < n)
    def _():
        issue(posn_ref, 1 - slot)

    wait(pos_ref, slot)
    gt = gt_ref[...]
    moe = ybuf_ref[slot, 0] * gt[:, 0:1] + ybuf_ref[slot, 1] * gt[:, 1:2]
    y = ALPHA * x_ref[...] + moe
    mu = jnp.mean(y, axis=-1, keepdims=True)
    yc = y - mu
    var = jnp.mean(yc * yc, axis=-1, keepdims=True)
    o_ref[...] = yc * lax.rsqrt(var + LN_EPS) * g_ref[...] + b_ref[...]


def _moe_combine(y_pad, pos_tiles, x1, gate_tok, ln_g, ln_b):
    t, d = x1.shape
    tm = min(COMBINE_ROWS, t)
    n = t // tm
    pos_spec = lambda off: pl.BlockSpec((1, 1, TOP_K * tm), lambda i: (jnp.minimum(i + off, n - 1), 0, 0),
                                        memory_space=pltpu.SMEM)
    return pl.pallas_call(
        _combine_kernel,
        out_shape=jax.ShapeDtypeStruct((t, d), F32),
        grid=(n,),
        in_specs=[pos_spec(0), pos_spec(1),
                  pl.BlockSpec(memory_space=pl.ANY),
                  pl.BlockSpec((tm, d), lambda i: (i, 0)),
                  pl.BlockSpec((tm, TOP_K), lambda i: (i, 0)),
                  pl.BlockSpec((1, d), lambda i: (0, 0)),
                  pl.BlockSpec((1, d), lambda i: (0, 0))],
        out_specs=pl.BlockSpec((tm, d), lambda i: (i, 0)),
        scratch_shapes=[pltpu.VMEM((2, TOP_K, tm, d), F32), pltpu.SemaphoreType.DMA((2,))],
        compiler_params=_cparams("arbitrary"),
        name="moe_combine",
    )(pos_tiles, pos_tiles, y_pad, x1, gate_tok, ln_g.reshape(1, d), ln_b.reshape(1, d))


def _routing_tables(eid, t):
    n_slots = t * TOP_K
    e_flat = eid.T.reshape(-1)
    onehot = (e_flat[:, None] == jnp.arange(N_EXPERTS, dtype=I32)[None, :]).astype(I32)
    csum = jnp.cumsum(onehot, axis=0)
    counts = csum[-1]
    rank = jnp.take_along_axis(csum, e_flat[:, None], axis=1)[:, 0] - 1
    padded = ((counts + MOE_BLOCK - 1) // MOE_BLOCK) * MOE_BLOCK
    pends = jnp.cumsum(padded)
    pstarts = pends - padded
    pos = (pstarts[e_flat] + rank).astype(I32)
    n_blk = -(-n_slots // MOE_BLOCK) + N_EXPERTS
    slot_tok = jnp.zeros((n_blk * MOE_BLOCK,), I32).at[pos].set(jnp.arange(n_slots, dtype=I32) // TOP_K)
    blk_exp = jnp.minimum(jnp.searchsorted(pends, jnp.arange(n_blk, dtype=I32) * MOE_BLOCK, side='right'),
                          N_EXPERTS - 1).astype(I32)
    nused = (pends[-1] // MOE_BLOCK).astype(I32).reshape(1)
    return slot_tok.reshape(n_blk, 1, MOE_BLOCK), blk_exp, nused, pos


def _rope_tables(seq):
    half = DIFF_QK_DIM // 2
    inv = 1.0 / (ROPE_THETA ** (jnp.arange(0, DIFF_QK_DIM, 2, dtype=F32) / DIFF_QK_DIM))
    ang = jnp.arange(seq, dtype=F32)[:, None] * inv[None, :]
    cos, sin = jnp.cos(ang), jnp.sin(ang)
    cos_t = jnp.concatenate([cos, cos, cos, cos], axis=1)
    sin_t = jnp.concatenate([-sin, sin, -sin, sin], axis=1)
    return cos_t, sin_t


def _split_w_in(w):
    z, xbc, dt, q, k, v, conf, sc = jnp.split(w, [512, 1536, 1544, 2056, 2568, 3080, 4104], axis=1)
    dt_tile = jnp.pad(dt, ((0, 0), (0, TN - SSD_HEADS)))
    return jnp.concatenate([xbc, conf, sc, z, q, k, v, dt_tile], axis=1).astype(BF16)


def kernel(x, ln_in_g, ln_in_b, w_in, ssd_conv_w, ssd_conv_b, ssd_dt_bias, ssd_a_log, ssd_d, ssd_norm_g, diff_lq1, diff_lk1, diff_lq2, diff_lk2, diff_norm_g, conf_dw_w, conf_dw_b, conf_ln_g, conf_ln_b, conf_pw_w, conf_pw_b, sc_conv_w, w_out, ln1_g, ln1_b, router_w, router_bias, moe_w_gate, moe_w_up, moe_w_down, ln2_g, ln2_b):
    b, s, d = x.shape
    t = b * s
    cos_t, sin_t = _rope_tables(s)
    xf = _layer_norm(x.reshape(t, d), ln_in_g, ln_in_b)
    for l in range(DEPTH):
        lambda_init = 0.8 - 0.6 * math.exp(-0.3 * l)
        h, dt_raw = _in_projection(xf, _split_w_in(w_in[l]), cos_t, sin_t, s)
        ya = _ssd_mixer(h, dt_raw, ssd_conv_w[l], ssd_conv_b[l], ssd_dt_bias[l], ssd_a_log[l],
                        ssd_d[l], ssd_norm_g[l], b, s)
        yb = _diff_attention(h, diff_lq1[l], diff_lk1[l], diff_lq2[l], diff_lk2[l], diff_norm_g[l],
                             lambda_init, b, s)
        yc, yd = _conv_mixers(h, conf_dw_w[l], conf_dw_b[l], conf_ln_g[l], conf_ln_b[l], conf_pw_w[l],
                              conf_pw_b[l], sc_conv_w[l], b, s)
        x1 = _out_projection(ya, yb, yc, yd, xf, w_out[l].astype(BF16), ln1_g[l], ln1_b[l])
        eid, gate = _router(x1, router_w, router_bias)
        slot_tok, blk_exp, nused, pos = _routing_tables(eid[0:TOP_K], t)
        y_pad = _moe_experts(x1, slot_tok, blk_exp, nused, moe_w_gate[l].astype(BF16),
                             moe_w_up[l].astype(BF16), moe_w_down[l].astype(BF16))
        tmc = min(COMBINE_ROWS, t)
        pos_tiles = pos.reshape(t // tmc, tmc, TOP_K).transpose(0, 2, 1).reshape(t // tmc, 1, TOP_K * tmc)
        xf = _moe_combine(y_pad, pos_tiles, x1, gate[0:TOP_K].T, ln2_g[l], ln2_b[l])
    return xf.reshape(b, s, d)
```

```python
import functools
import math

import numpy as np
import jax
import jax.numpy as jnp
from jax import lax
from jax.experimental import pallas as pl
from jax.experimental.pallas import tpu as pltpu

F32 = jnp.float32
BF16 = jnp.bfloat16
I32 = jnp.int32

D_MODEL = 2048
DEPTH = 2
CHUNK = 64
GROUP_W = 512
SSD_HEADS = 8
SSD_STATE = 128
SSD_CONV = 4
SSD_XBC = 1024
DIFF_HEADS = 4
DIFF_QK_DIM = 64
DIFF_V_DIM = 128
ROPE_THETA = 10000.0
CONF_KERNEL = 31
SC_KERNEL = 3
N_EXPERTS = 16
N_GROUPS = 4
EXPERTS_PER_GROUP = 4
TOP_K = 2
D_EXPERT = 1024
MOE_BLOCK = 256
ALPHA = (2 * DEPTH) ** 0.25
LN_EPS = 1e-5
LOG2E = 1.4426950408889634

LANES = 128
SUBLANES = 8
VMEM_LIMIT_BYTES = 56 * 1024 * 1024

COL_XBC, COL_CONF, COL_BG, COL_CG, COL_HH, COL_Z, COL_Q, COL_K, COL_V, COL_DT = 0, 2, 4, 5, 6, 7, 8, 9, 10, 11
H_COLS = 12 * 512
TN = 512
INPROJ_TN = 2048
INPROJ_TILES = H_COLS // INPROJ_TN
SSD_L = 128


def _cparams(*sem):
    return pltpu.CompilerParams(dimension_semantics=tuple(sem), vmem_limit_bytes=VMEM_LIMIT_BYTES)


def _sigmoid(x):
    return 1.0 / (1.0 + jnp.exp(-x))


def _silu(x):
    return x * _sigmoid(x)


def _softplus(x):
    return jnp.maximum(x, 0.0) + jnp.log(1.0 + jnp.exp(-jnp.abs(x)))


def _split3(v):
    hi = v.astype(BF16)
    r = v - hi.astype(F32)
    mid = r.astype(BF16)
    lo = (r - mid.astype(F32)).astype(BF16)
    return hi, mid, lo


def _dot_exact_rhs(v, m):
    return sum(jnp.dot(p, m, preferred_element_type=F32) for p in _split3(v))


def _dot_exact_lhs(m, v):
    return sum(jnp.dot(m, p, preferred_element_type=F32) for p in _split3(v))


def _dot_nt(a, b):
    return lax.dot_general(a, b, (((1,), (1,)), ((), ())), preferred_element_type=F32)


def _ln_kernel(x_ref, g_ref, b_ref, o_ref, ob_ref):
    x = x_ref[...]
    mu = jnp.mean(x, axis=-1, keepdims=True)
    xc = x - mu
    var = jnp.mean(xc * xc, axis=-1, keepdims=True)
    y = xc * lax.rsqrt(var + LN_EPS) * g_ref[...] + b_ref[...]
    o_ref[...] = y
    ob_ref[...] = y.astype(BF16)


def _layer_norm(x, g, b):
    t, d = x.shape
    tm = min(512, t)
    return pl.pallas_call(
        _ln_kernel,
        out_shape=(jax.ShapeDtypeStruct((t, d), F32), jax.ShapeDtypeStruct((t, d), BF16)),
        grid=(t // tm,),
        in_specs=[pl.BlockSpec((tm, d), lambda i: (i, 0)),
                  pl.BlockSpec((1, d), lambda i: (0, 0)),
                  pl.BlockSpec((1, d), lambda i: (0, 0))],
        out_specs=(pl.BlockSpec((tm, d), lambda i: (i, 0)), pl.BlockSpec((tm, d), lambda i: (i, 0))),
        compiler_params=_cparams("parallel"),
        name="entry_ln",
    )(x, g.reshape(1, d), b.reshape(1, d))


def _inproj_kernel(x_ref, w_ref, cos_ref, sin_ref, h_ref, dt_ref):
    j = pl.program_id(1)
    tm = x_ref.shape[0]
    x = x_ref[...]
    last = INPROJ_TILES - 1

    def rope_store(c, acc, scale):
        cs = cos_ref[...] * scale
        sn = sin_ref[...] * scale
        lane = lax.broadcasted_iota(I32, (tm, LANES), 1)
        first_half = (lane % DIFF_QK_DIM) < (DIFF_QK_DIM // 2)
        for hh in range(TN // LANES):
            a = acc[:, hh * LANES:(hh + 1) * LANES]
            rot = jnp.where(first_half, pltpu.roll(a, LANES - 32, 1), pltpu.roll(a, 32, 1))
            h_ref[:, c * TN + hh * LANES:c * TN + (hh + 1) * LANES] = (a * cs + rot * sn).astype(BF16)

    for c in range(INPROJ_TN // TN):
        acc = jnp.dot(x, w_ref[:, c * TN:(c + 1) * TN], preferred_element_type=F32)
        if c * TN == (COL_Q * TN) % INPROJ_TN or c * TN == (COL_K * TN) % INPROJ_TN:
            scale = LOG2E * DIFF_QK_DIM ** -0.5 if c * TN == (COL_Q * TN) % INPROJ_TN else 1.0

            @pl.when(j == last)
            def _():
                rope_store(c, acc, scale)

            @pl.when(j != last)
            def _():
                h_ref[:, c * TN:(c + 1) * TN] = acc.astype(BF16)
        else:
            h_ref[:, c * TN:(c + 1) * TN] = acc.astype(BF16)
        if c * TN == (COL_DT * TN) % INPROJ_TN:
            @pl.when(j == last)
            def _():
                dt_ref[...] = acc[:, 0:LANES]


def _in_projection(xb, w_tiles, cos_t, sin_t, seq):
    t, d = xb.shape
    tm = min(1024, seq)
    nseq = seq // tm
    return pl.pallas_call(
        _inproj_kernel,
        out_shape=(jax.ShapeDtypeStruct((t, H_COLS), BF16),
                   jax.ShapeDtypeStruct((t, LANES), F32)),
        grid=(t // tm, INPROJ_TILES),
        in_specs=[pl.BlockSpec((tm, d), lambda i, j: (i, 0)),
                  pl.BlockSpec((None, d, INPROJ_TN), lambda i, j: (j, 0, 0)),
                  pl.BlockSpec((tm, LANES), lambda i, j: (i % nseq, 0)),
                  pl.BlockSpec((tm, LANES), lambda i, j: (i % nseq, 0))],
        out_specs=(pl.BlockSpec((tm, INPROJ_TN), lambda i, j: (i, j)),
                   pl.BlockSpec((tm, LANES), lambda i, j: (i, 0))),
        compiler_params=_cparams("arbitrary", "arbitrary"),
        name="in_proj",
    )(xb, w_tiles, cos_t, sin_t)


def _ssd_kernel(xbc_ref, z_ref, dt_ref, cw_ref, cb_ref, dtb_ref, alog_ref, dtbT_ref, alogT_ref,
                dsk_ref, ng_ref, e64_ref, e128_ref, tri_ref, triT_ref, y_ref, xpad_ref, hst_ref):
    c = pl.program_id(1)
    L = SSD_L

    @pl.when(c == 0)
    def _():
        xpad_ref[0:SUBLANES, :] = jnp.zeros((SUBLANES, SSD_XBC), F32)
        hst_ref[...] = jnp.zeros(hst_ref.shape, F32)

    cur = xbc_ref[...].astype(F32)
    xpad_ref[SUBLANES:SUBLANES + L, :] = cur
    conv = cb_ref[...]
    for k in range(SSD_CONV):
        conv = conv + cw_ref[k:k + 1, :] * xpad_ref[pl.ds(SUBLANES - (SSD_CONV - 1) + k, L), :]
    xpad_ref[0:SUBLANES, :] = cur[L - SUBLANES:L, :]
    xbc = _silu(conv)
    X = xbc[:, 0:GROUP_W]
    Bm = xbc[:, GROUP_W:GROUP_W + 2 * SSD_STATE]
    Cm = xbc[:, GROUP_W + 2 * SSD_STATE:SSD_XBC]

    dt_raw = dt_ref[...]
    dt = _softplus(dt_raw + dtb_ref[...])
    acs = _dot_exact_lhs(tri_ref[...], dt * (-jnp.exp(alog_ref[...])))
    dtT = _softplus(dt_raw.T[0:16, :] + dtbT_ref[...])
    acsT = _dot_exact_rhs(dtT * (-jnp.exp(alogT_ref[...])), triT_ref[...])
    dt64 = _dot_exact_rhs(dt, e64_ref[...])
    acs64 = _dot_exact_rhs(acs, e64_ref[...])
    acs128 = _dot_exact_rhs(acs, e128_ref[...])

    last = acs64[L - 1:L, :]
    in_decay = jnp.exp(acs64)
    decay = jnp.exp(last - acs64)
    chunk_decay = jnp.exp(last)
    Xdt = X * dt64
    Xd = (Xdt * decay).astype(BF16)
    Xdt_b = Xdt.astype(BF16)

    row = lax.broadcasted_iota(I32, (L, L), 0)
    col = lax.broadcasted_iota(I32, (L, L), 1)
    causal = row >= col
    lane = lax.broadcasted_iota(I32, (L, LANES), 1)
    low_half = lane < 64

    y_parts = []
    for g in range(2):
        Bg = Bm[:, g * SSD_STATE:(g + 1) * SSD_STATE]
        Cg = Cm[:, g * SSD_STATE:(g + 1) * SSD_STATE].astype(BF16)
        cb = _dot_nt(Cg, Bg.astype(BF16))
        hprev = hst_ref[g]
        y_off = jnp.dot(Cg, hprev.astype(BF16), preferred_element_type=F32) * in_decay[:, g * 256:(g + 1) * 256]
        for pr in range(2):
            xp = Xdt_b[:, (2 * g + pr) * LANES:(2 * g + pr + 1) * LANES]
            ys = []
            for hh in range(2):
                h = 4 * g + 2 * pr + hh
                seg = acs128[:, h * LANES:(h + 1) * LANES] - acsT[h:h + 1, :]
                lmat = jnp.exp(jnp.where(causal, seg, -jnp.inf))
                w = (cb * lmat).astype(BF16)
                ys.append(jnp.dot(w, xp, preferred_element_type=F32))
            y_parts.append(jnp.where(low_half, ys[0], ys[1]) + y_off[:, pr * LANES:(pr + 1) * LANES])
        states = jnp.dot(Bg.T.astype(BF16), Xd[:, g * 256:(g + 1) * 256], preferred_element_type=F32)
        hst_ref[g] = hprev * chunk_decay[:, g * 256:(g + 1) * 256] + states

    z = z_ref[...].astype(F32)
    gate = _silu(z)
    for g in range(2):
        yg = jnp.concatenate(y_parts[2 * g:2 * g + 2], axis=1) + dsk_ref[:, g * 256:(g + 1) * 256] * X[:, g * 256:(g + 1) * 256]
        yg = yg * gate[:, g * 256:(g + 1) * 256]
        ms = jnp.mean(yg * yg, axis=-1, keepdims=True)
        y_ref[:, g * 256:(g + 1) * 256] = (yg * lax.rsqrt(ms + LN_EPS) * ng_ref[:, g * 256:(g + 1) * 256]).astype(BF16)


def _ssd_constants():
    e64 = np.zeros((LANES, 512), np.float32)
    e128 = np.zeros((LANES, 1024), np.float32)
    for h in range(SSD_HEADS):
        e64[h, 64 * h:64 * (h + 1)] = 1.0
        e128[h, 128 * h:128 * (h + 1)] = 1.0
    tri = np.tril(np.ones((SSD_L, SSD_L), np.float32))
    return (jnp.asarray(e64, BF16), jnp.asarray(e128, BF16), jnp.asarray(tri, BF16), jnp.asarray(tri.T, BF16))


def _ssd_mixer(h, dt_raw, conv_w, conv_b, dt_bias, a_log, d_skip, norm_g, batch, seq):
    t = h.shape[0]
    L = SSD_L
    nc = seq // L
    e64, e128, tri, triT = _ssd_constants()
    pad_l = lambda v: jnp.pad(v.reshape(1, SSD_HEADS), ((0, 0), (0, LANES - SSD_HEADS)))
    pad_t = lambda v: jnp.pad(jnp.broadcast_to(v.reshape(SSD_HEADS, 1), (SSD_HEADS, L)), ((0, 16 - SSD_HEADS), (0, 0)))
    rb = lambda b, c: b * nc + c
    const = lambda shape: pl.BlockSpec(shape, lambda b, c: (0,) * len(shape))
    return pl.pallas_call(
        _ssd_kernel,
        out_shape=jax.ShapeDtypeStruct((t, GROUP_W), BF16),
        grid=(batch, nc),
        in_specs=[pl.BlockSpec((L, SSD_XBC), lambda b, c: (rb(b, c), COL_XBC // 2)),
                  pl.BlockSpec((L, GROUP_W), lambda b, c: (rb(b, c), COL_Z)),
                  pl.BlockSpec((L, LANES), lambda b, c: (rb(b, c), 0)),
                  const((SSD_CONV, SSD_XBC)), const((1, SSD_XBC)),
                  const((1, LANES)), const((1, LANES)), const((16, L)), const((16, L)),
                  const((1, GROUP_W)), const((1, GROUP_W)),
                  const((LANES, 512)), const((LANES, 1024)), const((L, L)), const((L, L))],
        out_specs=pl.BlockSpec((L, GROUP_W), lambda b, c: (rb(b, c), 0)),
        scratch_shapes=[pltpu.VMEM((SUBLANES + L, SSD_XBC), F32),
                        pltpu.VMEM((2, SSD_STATE, 256), F32)],
        compiler_params=_cparams("arbitrary", "arbitrary"),
        name="ssd_mixer",
    )(h, h, dt_raw, conv_w, conv_b.reshape(1, SSD_XBC),
      pad_l(dt_bias), pad_l(a_log), pad_t(dt_bias), pad_t(a_log),
      jnp.repeat(d_skip, 64).reshape(1, GROUP_W), norm_g.reshape(1, GROUP_W), e64, e128, tri, triT)


def _attn_kernel(q_ref, k_ref, v_ref, bias_ref, lq1_ref, lk1_ref, lq2_ref, lk2_ref, g_ref, o_ref,
                 m_ref, l_ref, acc_ref, *, lambda_init):
    qi = pl.program_id(2)
    tq = q_ref.shape[0]
    tk = tq
    q = q_ref[...]
    lane = lax.broadcasted_iota(I32, (tq, LANES), 1)
    zero = jnp.zeros_like(q)
    qs = (jnp.where(lane < DIFF_QK_DIM, q, zero), jnp.where(lane >= DIFF_QK_DIM, q, zero))
    m_ref[...] = jnp.full(m_ref.shape, -jnp.inf, F32)
    l_ref[...] = jnp.zeros(l_ref.shape, F32)
    acc_ref[...] = jnp.zeros(acc_ref.shape, F32)

    def tile(off, bias):
        k = k_ref[pl.ds(off, tk), :]
        v = v_ref[pl.ds(off, tk), :]
        for m in range(2):
            s = _dot_nt(qs[m], k)
            if bias is not None:
                s = s + bias
            m_prev = m_ref[m]
            m_new = jnp.maximum(m_prev, jnp.max(s, axis=-1, keepdims=True))
            alpha = jnp.exp2(m_prev - m_new)
            p = jnp.exp2(s - jnp.concatenate([m_new] * (tk // LANES), axis=1))
            psum = p[:, 0:LANES]
            for c in range(1, tk // LANES):
                psum = psum + p[:, c * LANES:(c + 1) * LANES]
            l_ref[m] = alpha * l_ref[m] + psum
            acc_ref[m] = alpha * acc_ref[m] + jnp.dot(p.astype(BF16), v, preferred_element_type=F32)
            m_ref[m] = m_new

    def body(j, carry):
        tile(pl.multiple_of(j * tk, tk), None)
        return carry

    lax.fori_loop(0, qi, body, 0)
    tile(pl.multiple_of(qi * tk, tk), bias_ref[...])

    lam = (jnp.exp(jnp.sum(lq1_ref[...] * lk1_ref[...], axis=-1, keepdims=True))
           - jnp.exp(jnp.sum(lq2_ref[...] * lk2_ref[...], axis=-1, keepdims=True)) + lambda_init)
    l0 = jnp.sum(l_ref[0], axis=-1, keepdims=True)
    l1 = jnp.sum(l_ref[1], axis=-1, keepdims=True)
    o = acc_ref[0] / l0 - lam * (acc_ref[1] / l1)
    ms = jnp.mean(o * o, axis=-1, keepdims=True)
    o_ref[...] = (o * lax.rsqrt(ms + LN_EPS) * g_ref[...] * (1.0 - lambda_init)).astype(BF16)


def _chunk_mask_bias(tq):
    r = np.arange(tq)[:, None] // CHUNK
    c = np.arange(tq)[None, :] // CHUNK
    return jnp.asarray(np.where(c <= r, 0.0, -np.inf), F32)


def _diff_attention(h, lq1, lk1, lq2, lk2, norm_g, lambda_init, batch, seq):
    t = h.shape[0]
    tq = min(512, seq)
    nq = seq // tq
    qcol = COL_Q * 4
    kcol = COL_K * 4
    vcol = COL_V * 4
    vec = lambda n: pl.BlockSpec((1, n), lambda b, hd, qi: (0, 0))
    return pl.pallas_call(
        functools.partial(_attn_kernel, lambda_init=lambda_init),
        out_shape=jax.ShapeDtypeStruct((t, GROUP_W), BF16),
        grid=(batch, DIFF_HEADS, nq),
        in_specs=[pl.BlockSpec((tq, LANES), lambda b, hd, qi: (b * nq + qi, qcol + hd)),
                  pl.BlockSpec((seq, LANES), lambda b, hd, qi: (b, kcol + hd)),
                  pl.BlockSpec((seq, LANES), lambda b, hd, qi: (b, vcol + hd)),
                  pl.BlockSpec((tq, tq), lambda b, hd, qi: (0, 0)),
                  vec(DIFF_QK_DIM), vec(DIFF_QK_DIM), vec(DIFF_QK_DIM), vec(DIFF_QK_DIM), vec(DIFF_V_DIM)],
        out_specs=pl.BlockSpec((tq, LANES), lambda b, hd, qi: (b * nq + qi, hd)),
        scratch_shapes=[pltpu.VMEM((2, tq, LANES), F32), pltpu.VMEM((2, tq, LANES), F32),
                        pltpu.VMEM((2, tq, DIFF_V_DIM), F32)],
        compiler_params=_cparams("arbitrary", "arbitrary", "arbitrary"),
        name="diff_attn",
    )(h, h, h, _chunk_mask_bias(tq), lq1.reshape(1, -1), lk1.reshape(1, -1), lq2.reshape(1, -1),
      lk2.reshape(1, -1), norm_g.reshape(1, -1))


CONF_HALO = 32
SC_HALO = 16
CONV_ROWS = 64


def _conv_kernel(u_ref, uh_ref, bg_ref, cg_ref, hh_ref, cgh_ref, hhh_ref,
                 dww_ref, dwb_ref, lng_ref, lnb_ref, pww_ref, pwb_ref, scw_ref,
                 yc_ref, yd_ref, hbuf_ref, pbuf_ref, cbuf_ref, sbuf_ref):
    i = pl.program_id(1)
    tm = u_ref.shape[0]

    def glu(u):
        u = u.astype(F32)
        return u[:, 0:GROUP_W] * _sigmoid(u[:, GROUP_W:2 * GROUP_W])

    first = (i == 0)
    hbuf_ref[0:CONF_HALO, :] = jnp.where(first, 0.0, glu(uh_ref[...]))
    hbuf_ref[CONF_HALO:CONF_HALO + tm, :] = glu(u_ref[...])
    pbuf_ref[0:SC_HALO, :] = jnp.where(first, 0.0, cgh_ref[...].astype(F32) * hhh_ref[...].astype(F32))
    pbuf_ref[SC_HALO:SC_HALO + tm, :] = cg_ref[...].astype(F32) * hh_ref[...].astype(F32)

    for b in range(1, SUBLANES):
        sbuf_ref[b - 1, 0:tm + CONF_HALO - SUBLANES, :] = hbuf_ref[pl.ds(b, tm + CONF_HALO - SUBLANES), :]
    for r0 in range(0, tm, CONV_ROWS):
        acc = jnp.zeros((CONV_ROWS, GROUP_W), F32) + dwb_ref[...]
        for k in range(CONF_KERNEL):
            off = CONF_HALO - (CONF_KERNEL - 1) + k
            phase, base = off % SUBLANES, r0 + off - off % SUBLANES
            src = hbuf_ref[pl.ds(base, CONV_ROWS), :] if phase == 0 else sbuf_ref[phase - 1, pl.ds(base, CONV_ROWS), :]
            acc = acc + dww_ref[k:k + 1, :] * src
        cbuf_ref[r0:r0 + CONV_ROWS, :] = acc
    hc = cbuf_ref[...]
    mu = jnp.mean(hc, axis=-1, keepdims=True)
    xc = hc - mu
    var = jnp.mean(xc * xc, axis=-1, keepdims=True)
    hn = _silu(xc * lax.rsqrt(var + LN_EPS) * lng_ref[...] + lnb_ref[...])
    yc = jnp.dot(hn.astype(BF16), pww_ref[...], preferred_element_type=F32) + pwb_ref[...]
    yc_ref[...] = yc.astype(BF16)

    sc = jnp.zeros((tm, GROUP_W), F32)
    for k in range(SC_KERNEL):
        sc = sc + scw_ref[k:k + 1, :] * pbuf_ref[pl.ds(SC_HALO - (SC_KERNEL - 1) + k, tm), :]
    yd_ref[...] = (bg_ref[...].astype(F32) * sc).astype(BF16)


def _conv_mixers(h, dw_w, dw_b, ln_g, ln_b, pw_w, pw_b, sc_w, batch, seq):
    t = h.shape[0]
    tm = min(512, seq)
    nt = seq // tm
    rb = lambda b, i: b * nt + i
    halo = lambda rows: (lambda b, i: jnp.maximum(rb(b, i) * (tm // rows) - 1, 0))
    hc, hs = halo(CONF_HALO), halo(SC_HALO)
    const = lambda shape: pl.BlockSpec(shape, lambda b, i: (0,) * len(shape))
    return pl.pallas_call(
        _conv_kernel,
        out_shape=(jax.ShapeDtypeStruct((t, GROUP_W), BF16), jax.ShapeDtypeStruct((t, GROUP_W), BF16)),
        grid=(batch, nt),
        in_specs=[pl.BlockSpec((tm, 2 * GROUP_W), lambda b, i: (rb(b, i), COL_CONF // 2)),
                  pl.BlockSpec((CONF_HALO, 2 * GROUP_W), lambda b, i: (hc(b, i), COL_CONF // 2)),
                  pl.BlockSpec((tm, GROUP_W), lambda b, i: (rb(b, i), COL_BG)),
                  pl.BlockSpec((tm, GROUP_W), lambda b, i: (rb(b, i), COL_CG)),
                  pl.BlockSpec((tm, GROUP_W), lambda b, i: (rb(b, i), COL_HH)),
                  pl.BlockSpec((SC_HALO, GROUP_W), lambda b, i: (hs(b, i), COL_CG)),
                  pl.BlockSpec((SC_HALO, GROUP_W), lambda b, i: (hs(b, i), COL_HH)),
                  const((CONF_KERNEL, GROUP_W)), const((1, GROUP_W)), const((1, GROUP_W)), const((1, GROUP_W)),
                  const((GROUP_W, GROUP_W)), const((1, GROUP_W)), const((SC_KERNEL, GROUP_W))],
        out_specs=(pl.BlockSpec((tm, GROUP_W), lambda b, i: (rb(b, i), 0)),
                   pl.BlockSpec((tm, GROUP_W), lambda b, i: (rb(b, i), 0))),
        scratch_shapes=[pltpu.VMEM((CONF_HALO + tm, GROUP_W), F32),
                        pltpu.VMEM((SC_HALO + tm, GROUP_W), F32),
                        pltpu.VMEM((tm, GROUP_W), F32),
                        pltpu.VMEM((SUBLANES - 1, CONF_HALO + tm, GROUP_W), F32)],
        compiler_params=_cparams("arbitrary", "arbitrary"),
        name="conv_mixers",
    )(h, h, h, h, h, h, h, dw_w, dw_b.reshape(1, -1), ln_g.reshape(1, -1), ln_b.reshape(1, -1),
      pw_w.astype(BF16), pw_b.reshape(1, -1), sc_w)


def _first_max4(vals):
    m1 = jnp.maximum(jnp.maximum(vals[0], vals[1]), jnp.maximum(vals[2], vals[3]))
    i1 = jnp.where(vals[0] == m1, 0, jnp.where(vals[1] == m1, 1, jnp.where(vals[2] == m1, 2, 3)))
    rest = [jnp.where(i1 == j, -jnp.inf, vals[j]) for j in range(4)]
    m2 = jnp.maximum(jnp.maximum(rest[0], rest[1]), jnp.maximum(rest[2], rest[3]))
    i2 = jnp.where(rest[0] == m2, 0, jnp.where(rest[1] == m2, 1, jnp.where(rest[2] == m2, 2, 3)))
    return m1, i1, m2, i2


def _outproj_kernel(ya_ref, yb_ref, yc_ref, yd_ref, x_ref, w_ref, g_ref, b_ref, x1_ref):
    mix = jnp.dot(ya_ref[...], w_ref[0:GROUP_W, :], preferred_element_type=F32)
    mix = mix + jnp.dot(yb_ref[...], w_ref[GROUP_W:2 * GROUP_W, :], preferred_element_type=F32)
    mix = mix + jnp.dot(yc_ref[...], w_ref[2 * GROUP_W:3 * GROUP_W, :], preferred_element_type=F32)
    mix = mix + jnp.dot(yd_ref[...], w_ref[3 * GROUP_W:4 * GROUP_W, :], preferred_element_type=F32)
    y = ALPHA * x_ref[...] + mix
    mu = jnp.mean(y, axis=-1, keepdims=True)
    yc = y - mu
    var = jnp.mean(yc * yc, axis=-1, keepdims=True)
    x1_ref[...] = yc * lax.rsqrt(var + LN_EPS) * g_ref[...] + b_ref[...]


def _router_kernel(x_ref, w2_ref, w1_ref, rb_ref, eid_ref, gate_ref):
    tm = x_ref.shape[0]
    x = x_ref[...]
    xh = x.astype(BF16)
    xm = (x - xh.astype(F32)).astype(BF16)
    r = (jnp.dot(xh, w2_ref[...], preferred_element_type=F32)
         + jnp.dot(xm, w1_ref[...], preferred_element_type=F32))
    r = r + pltpu.roll(r, LANES - N_EXPERTS, 1)
    logits = r.T[0:N_EXPERTS, :]
    aff = _sigmoid(logits)
    sel = aff + jnp.concatenate([rb_ref[...]] * (tm // LANES), axis=1)
    rows = [sel[e:e + 1, :] for e in range(N_EXPERTS)]
    arow = [aff[e:e + 1, :] for e in range(N_EXPERTS)]
    tops = [_first_max4(rows[4 * g:4 * g + 4]) for g in range(N_GROUPS)]
    score = [tp[0] + tp[2] for tp in tops]
    best = jnp.maximum(jnp.maximum(score[0], score[1]), jnp.maximum(score[2], score[3]))
    grp = jnp.where(score[0] == best, 0, jnp.where(score[1] == best, 1, jnp.where(score[2] == best, 2, 3)))
    pick = lambda idx: jnp.where(grp == 0, tops[0][idx], jnp.where(grp == 1, tops[1][idx],
                                 jnp.where(grp == 2, tops[2][idx], tops[3][idx])))
    e1 = grp * EXPERTS_PER_GROUP + pick(1)
    e2 = grp * EXPERTS_PER_GROUP + pick(3)
    a1 = jnp.zeros_like(best)
    a2 = jnp.zeros_like(best)
    for e in range(N_EXPERTS):
        a1 = jnp.where(e1 == e, arow[e], a1)
        a2 = jnp.where(e2 == e, arow[e], a2)
    den = a1 + a2
    zi = jnp.zeros((SUBLANES - 2, tm), I32)
    zf = jnp.zeros((SUBLANES - 2, tm), F32)
    eid_ref[...] = jnp.concatenate([e1.astype(I32), e2.astype(I32), zi], axis=0)
    gate_ref[...] = jnp.concatenate([a1 / den, a2 / den, zf], axis=0)


def _out_projection(ya, yb, yc, yd, x, w_out, ln_g, ln_b):
    t, d = x.shape
    tm = min(512, t)
    act = lambda: pl.BlockSpec((tm, GROUP_W), lambda i: (i, 0))
    const = lambda shape: pl.BlockSpec(shape, lambda i: (0,) * len(shape))
    return pl.pallas_call(
        _outproj_kernel,
        out_shape=jax.ShapeDtypeStruct((t, d), F32),
        grid=(t // tm,),
        in_specs=[act(), act(), act(), act(),
                  pl.BlockSpec((tm, d), lambda i: (i, 0)),
                  const((d, d)), const((1, d)), const((1, d))],
        out_specs=pl.BlockSpec((tm, d), lambda i: (i, 0)),
        compiler_params=_cparams("parallel"),
        name="out_proj",
    )(ya, yb, yc, yd, x, w_out, ln_g.reshape(1, d), ln_b.reshape(1, d))


def _router(x1, router_w, router_bias):
    t, d = x1.shape
    tm = min(512, t)
    w_hi = router_w.astype(BF16)
    w_mid = (router_w - w_hi.astype(F32)).astype(BF16)
    w2 = jnp.concatenate([w_hi, w_mid, jnp.zeros((d, LANES - 2 * N_EXPERTS), BF16)], axis=1)
    w1 = jnp.concatenate([w_hi, jnp.zeros((d, LANES - N_EXPERTS), BF16)], axis=1)
    router_b = jnp.broadcast_to(router_bias.reshape(N_EXPERTS, 1), (N_EXPERTS, LANES))
    const = lambda shape: pl.BlockSpec(shape, lambda i: (0,) * len(shape))
    return pl.pallas_call(
        _router_kernel,
        out_shape=(jax.ShapeDtypeStruct((SUBLANES, t), I32), jax.ShapeDtypeStruct((SUBLANES, t), F32)),
        grid=(t // tm,),
        in_specs=[pl.BlockSpec((tm, d), lambda i: (i, 0)),
                  const((d, LANES)), const((d, LANES)), const((N_EXPERTS, LANES))],
        out_specs=(pl.BlockSpec((SUBLANES, tm), lambda i: (0, i)),
                   pl.BlockSpec((SUBLANES, tm), lambda i: (0, i))),
        compiler_params=_cparams("parallel"),
        name="router",
    )(x1, w2, w1, router_b)


def _row_copy(src_hbm, tok_ref, dst_ref, sem, r, base=0):
    tok = tok_ref[0, 0, base + r]
    return pltpu.make_async_copy(src_hbm.at[pl.ds(tok, 1), :], dst_ref.at[pl.ds(r, 1), :], sem)


def _moe_kernel(blk_exp_ref, nused_ref, tok_ref, tokn_ref, x_hbm, wg_ref, wu_ref, wd_ref, y_ref,
                xbuf0_ref, xbuf1_ref, sem_ref):
    i = pl.program_id(0)
    nused = nused_ref[0]
    bufs = (xbuf0_ref, xbuf1_ref)

    def issue(tref, s):
        carry = tref[0, 0, 0] >> 31
        for r in range(MOE_BLOCK):
            tok = tref[0, 0, r] + carry
            pltpu.make_async_copy(x_hbm.at[pl.ds(tok, 1), :], bufs[s].at[pl.ds(r, 1), :],
                                  sem_ref.at[s]).start(priority=r % 2)
            z = tok >> 31
            carry = (z * 5 + z) >> 1

    def wait(tref, s):
        for r in range(MOE_BLOCK):
            _row_copy(x_hbm, tref, bufs[s], sem_ref.at[s], r).wait()

    @pl.when(i == 0)
    def _():
        issue(tok_ref, 0)

    for s in range(2):
        @pl.when(jnp.logical_and(i < nused, i % 2 == s))
        def _():
            wait(tok_ref, s)
            issue(tokn_ref, 1 - s)
            x = bufs[s][...].astype(BF16)
            hg = jnp.dot(x, wg_ref[...], preferred_element_type=F32)
            hu = jnp.dot(x, wu_ref[...], preferred_element_type=F32)
            hid = (_silu(hg) * hu).astype(BF16)
            y_ref[...] = jnp.dot(hid, wd_ref[...], preferred_element_type=F32)

        @pl.when(jnp.logical_and(i + 1 == nused, i % 2 == s))
        def _():
            wait(tokn_ref, 1 - s)

    @pl.when(i >= nused)
    def _():
        y_ref[...] = jnp.zeros(y_ref.shape, F32)


def _moe_experts(x1, slot_tok, blk_exp, nused, wg, wu, wd):
    t, d = x1.shape
    n_blk = slot_tok.shape[0]
    tok_spec = lambda off: pl.BlockSpec((1, 1, MOE_BLOCK),
                                        lambda i, be, nu: (jnp.minimum(i + off, nu[0] - 1), 0, 0),
                                        memory_space=pltpu.SMEM)
    return pl.pallas_call(
        _moe_kernel,
        out_shape=jax.ShapeDtypeStruct((n_blk * MOE_BLOCK, d), F32),
        grid_spec=pltpu.PrefetchScalarGridSpec(
            num_scalar_prefetch=2,
            grid=(n_blk,),
            in_specs=[tok_spec(0), tok_spec(1),
                      pl.BlockSpec(memory_space=pl.ANY),
                      pl.BlockSpec((None, d, D_EXPERT), lambda i, be, nu: (be[i], 0, 0)),
                      pl.BlockSpec((None, d, D_EXPERT), lambda i, be, nu: (be[i], 0, 0)),
                      pl.BlockSpec((None, D_EXPERT, d), lambda i, be, nu: (be[i], 0, 0))],
            out_specs=pl.BlockSpec((MOE_BLOCK, d), lambda i, be, nu: (i, 0)),
            scratch_shapes=[pltpu.VMEM((MOE_BLOCK, d), F32), pltpu.VMEM((MOE_BLOCK, d), F32),
                            pltpu.SemaphoreType.DMA((2,))]),
        compiler_params=_cparams("arbitrary"),
        name="moe_experts",
    )(blk_exp, nused, slot_tok, slot_tok, x1, wg, wu, wd)


COMBINE_ROWS = 256


def _combine_kernel(pos_ref, posn_ref, y_hbm, x_ref, gt_ref, g_ref, b_ref, o_ref, ob_ref, ybuf_ref, sem_ref):
    i = pl.program_id(0)
    n = pl.num_programs(0)
    slot = i % 2
    tm = x_ref.shape[0]

    def issue(pref, s):
        def body(r8, carry):
            for u in range(SUBLANES):
                for k in range(TOP_K):
                    _row_copy(y_hbm, pref, ybuf_ref.at[s, k], sem_ref.at[s], r8 * SUBLANES + u,
                              base=k * tm).start(priority=k)
            return carry
        lax.fori_loop(0, tm // SUBLANES, body, 0)

    def wait(pref, s):
        def body(r8, carry):
            for u in range(SUBLANES):
                for k in range(TOP_K):
                    _row_copy(y_hbm, pref, ybuf_ref.at[s, k], sem_ref.at[s], r8 * SUBLANES + u, base=k * tm).wait()
            return carry
        lax.fori_loop(0, tm // SUBLANES, body, 0)

    @pl.when(i == 0)
    def _():
        issue(pos_ref, 0)

    @pl.when(i + 1 < n)
    def _():
        issue(posn_ref, 1 - slot)

    wait(pos_ref, slot)
    gt = gt_ref[...]
    moe = ybuf_ref[slot, 0] * gt[:, 0:1] + ybuf_ref[slot, 1] * gt[:, 1:2]
    y = ALPHA * x_ref[...] + moe
    mu = jnp.mean(y, axis=-1, keepdims=True)
    yc = y - mu
    var = jnp.mean(yc * yc, axis=-1, keepdims=True)
    out = yc * lax.rsqrt(var + LN_EPS) * g_ref[...] + b_ref[...]
    o_ref[...] = out
    ob_ref[...] = out.astype(BF16)


def _moe_combine(y_pad, pos_tiles, x1, gate_tok, ln_g, ln_b):
    t, d = x1.shape
    tm = min(COMBINE_ROWS, t)
    n = t // tm
    pos_spec = lambda off: pl.BlockSpec((1, 1, TOP_K * tm), lambda i: (jnp.minimum(i + off, n - 1), 0, 0),
                                        memory_space=pltpu.SMEM)
    return pl.pallas_call(
        _combine_kernel,
        out_shape=(jax.ShapeDtypeStruct((t, d), F32), jax.ShapeDtypeStruct((t, d), BF16)),
        grid=(n,),
        in_specs=[pos_spec(0), pos_spec(1),
                  pl.BlockSpec(memory_space=pl.ANY),
                  pl.BlockSpec((tm, d), lambda i: (i, 0)),
                  pl.BlockSpec((tm, TOP_K), lambda i: (i, 0)),
                  pl.BlockSpec((1, d), lambda i: (0, 0)),
                  pl.BlockSpec((1, d), lambda i: (0, 0))],
        out_specs=(pl.BlockSpec((tm, d), lambda i: (i, 0)), pl.BlockSpec((tm, d), lambda i: (i, 0))),
        scratch_shapes=[pltpu.VMEM((2, TOP_K, tm, d), F32), pltpu.SemaphoreType.DMA((2,))],
        compiler_params=_cparams("arbitrary"),
        name="moe_combine",
    )(pos_tiles, pos_tiles, y_pad, x1, gate_tok, ln_g.reshape(1, d), ln_b.reshape(1, d))


def _routing_tables(eid, t):
    n_slots = t * TOP_K
    e_flat = eid.T.reshape(-1)
    onehot = (e_flat[:, None] == jnp.arange(N_EXPERTS, dtype=I32)[None, :]).astype(I32)
    csum = jnp.cumsum(onehot, axis=0)
    counts = csum[-1]
    rank = jnp.take_along_axis(csum, e_flat[:, None], axis=1)[:, 0] - 1
    padded = ((counts + MOE_BLOCK - 1) // MOE_BLOCK) * MOE_BLOCK
    pends = jnp.cumsum(padded)
    pstarts = pends - padded
    pos = (pstarts[e_flat] + rank).astype(I32)
    n_blk = -(-n_slots // MOE_BLOCK) + N_EXPERTS
    slot_tok = jnp.zeros((n_blk * MOE_BLOCK,), I32).at[pos].set(jnp.arange(n_slots, dtype=I32) // TOP_K)
    blk_exp = jnp.minimum(jnp.searchsorted(pends, jnp.arange(n_blk, dtype=I32) * MOE_BLOCK, side='right'),
                          N_EXPERTS - 1).astype(I32)
    nused = (pends[-1] // MOE_BLOCK).astype(I32).reshape(1)
    return slot_tok.reshape(n_blk, 1, MOE_BLOCK), blk_exp, nused, pos


def _rope_tables(seq):
    half = DIFF_QK_DIM // 2
    inv = 1.0 / (ROPE_THETA ** (jnp.arange(0, DIFF_QK_DIM, 2, dtype=F32) / DIFF_QK_DIM))
    ang = jnp.arange(seq, dtype=F32)[:, None] * inv[None, :]
    cos, sin = jnp.cos(ang), jnp.sin(ang)
    cos_t = jnp.concatenate([cos, cos, cos, cos], axis=1)
    sin_t = jnp.concatenate([-sin, sin, -sin, sin], axis=1)
    return cos_t, sin_t


W_IN_SPLITS = (0, 512, 1536, 1544, 2056, 2568, 3080, 4104, 5640)


def _wprep_kernel(w_ref, o_ref):
    w = w_ref[...]
    z, xbc, dt, q, k, v, conf, sc = [w[:, a:b] for a, b in zip(W_IN_SPLITS[:-1], W_IN_SPLITS[1:])]
    rows = w.shape[0]
    o_ref[0] = jnp.concatenate([xbc, conf], axis=1).astype(BF16)
    o_ref[1] = jnp.concatenate([sc, z], axis=1).astype(BF16)
    o_ref[2] = jnp.concatenate([q, k, v, dt, jnp.zeros((rows, TN - SSD_HEADS), F32)], axis=1).astype(BF16)


def _prep_w_in(w):
    d, n = w.shape
    rows = 256
    return pl.pallas_call(
        _wprep_kernel,
        out_shape=jax.ShapeDtypeStruct((INPROJ_TILES, d, INPROJ_TN), BF16),
        grid=(d // rows,),
        in_specs=[pl.BlockSpec((rows, n), lambda i: (i, 0))],
        out_specs=pl.BlockSpec((INPROJ_TILES, rows, INPROJ_TN), lambda i: (0, i, 0)),
        compiler_params=_cparams("parallel"),
        name="w_in_prep",
    )(w)


def kernel(x, ln_in_g, ln_in_b, w_in, ssd_conv_w, ssd_conv_b, ssd_dt_bias, ssd_a_log, ssd_d, ssd_norm_g, diff_lq1, diff_lk1, diff_lq2, diff_lk2, diff_norm_g, conf_dw_w, conf_dw_b, conf_ln_g, conf_ln_b, conf_pw_w, conf_pw_b, sc_conv_w, w_out, ln1_g, ln1_b, router_w, router_bias, moe_w_gate, moe_w_up, moe_w_down, ln2_g, ln2_b):
    b, s, d = x.shape
    t = b * s
    cos_t, sin_t = _rope_tables(s)
    xf, xb = _layer_norm(x.reshape(t, d), ln_in_g, ln_in_b)
    for l in range(DEPTH):
        lambda_init = 0.8 - 0.6 * math.exp(-0.3 * l)
        h, dt_raw = _in_projection(xb, _prep_w_in(w_in[l]), cos_t, sin_t, s)
        ya = _ssd_mixer(h, dt_raw, ssd_conv_w[l], ssd_conv_b[l], ssd_dt_bias[l], ssd_a_log[l],
                        ssd_d[l], ssd_norm_g[l], b, s)
        yb = _diff_attention(h, diff_lq1[l], diff_lk1[l], diff_lq2[l], diff_lk2[l], diff_norm_g[l],
                             lambda_init, b, s)
        yc, yd = _conv_mixers(h, conf_dw_w[l], conf_dw_b[l], conf_ln_g[l], conf_ln_b[l], conf_pw_w[l],
                              conf_pw_b[l], sc_conv_w[l], b, s)
        x1 = _out_projection(ya, yb, yc, yd, xf, w_out[l].astype(BF16), ln1_g[l], ln1_b[l])
        eid, gate = _router(x1, router_w, router_bias)
        slot_tok, blk_exp, nused, pos = _routing_tables(eid[0:TOP_K], t)
        y_pad = _moe_experts(x1, slot_tok, blk_exp, nused, moe_w_gate[l].astype(BF16),
                             moe_w_up[l].astype(BF16), moe_w_down[l].astype(BF16))
        tmc = min(COMBINE_ROWS, t)
        pos_tiles = pos.reshape(t // tmc, tmc, TOP_K).transpose(0, 2, 1).reshape(t // tmc, 1, TOP_K * tmc)
        xf, xb = _moe_combine(y_pad, pos_tiles, x1, gate[0:TOP_K].T, ln2_g[l], ln2_b[l])
    return xf.reshape(b, s, d)
```

```python
import functools
import math

import numpy as np
import jax
import jax.numpy as jnp
from jax import lax
from jax.experimental import pallas as pl
from jax.experimental.pallas import tpu as pltpu

F32 = jnp.float32
BF16 = jnp.bfloat16
I32 = jnp.int32

D_MODEL = 2048
DEPTH = 2
CHUNK = 64
GROUP_W = 512
SSD_HEADS = 8
SSD_STATE = 128
SSD_CONV = 4
SSD_XBC = 1024
DIFF_HEADS = 4
DIFF_QK_DIM = 64
DIFF_V_DIM = 128
ROPE_THETA = 10000.0
CONF_KERNEL = 31
SC_KERNEL = 3
N_EXPERTS = 16
N_GROUPS = 4
EXPERTS_PER_GROUP = 4
TOP_K = 2
D_EXPERT = 1024
MOE_BLOCK = 256
ALPHA = (2 * DEPTH) ** 0.25
LN_EPS = 1e-5
LOG2E = 1.4426950408889634

LANES = 128
SUBLANES = 8
ROW_CHUNKS = D_MODEL // LANES
VMEM_LIMIT_BYTES = 56 * 1024 * 1024

COL_XBC, COL_CONF, COL_BG, COL_CG, COL_HH, COL_Z, COL_Q, COL_K, COL_V, COL_DT = 0, 2, 4, 5, 6, 7, 8, 9, 10, 11
H_COLS = 12 * 512
TN = 512
INPROJ_TN = 2048
INPROJ_TILES = H_COLS // INPROJ_TN
SSD_L = 128


def _cparams(*sem):
    return pltpu.CompilerParams(dimension_semantics=tuple(sem), vmem_limit_bytes=VMEM_LIMIT_BYTES)


def _sigmoid(x):
    return 1.0 / (1.0 + jnp.exp(-x))


def _silu(x):
    return x * _sigmoid(x)


def _softplus(x):
    return jnp.maximum(x, 0.0) + jnp.log(1.0 + jnp.exp(-jnp.abs(x)))


def _split3(v):
    hi = v.astype(BF16)
    r = v - hi.astype(F32)
    mid = r.astype(BF16)
    lo = (r - mid.astype(F32)).astype(BF16)
    return hi, mid, lo


def _dot_exact_rhs(v, m):
    return sum(jnp.dot(p, m, preferred_element_type=F32) for p in _split3(v))


def _dot_exact_lhs(m, v):
    return sum(jnp.dot(m, p, preferred_element_type=F32) for p in _split3(v))


def _dot_nt(a, b):
    return lax.dot_general(a, b, (((1,), (1,)), ((), ())), preferred_element_type=F32)


def _ln_kernel(x_ref, g_ref, b_ref, o_ref, ob_ref):
    x = x_ref[...]
    mu = jnp.mean(x, axis=-1, keepdims=True)
    xc = x - mu
    var = jnp.mean(xc * xc, axis=-1, keepdims=True)
    y = xc * lax.rsqrt(var + LN_EPS) * g_ref[...] + b_ref[...]
    o_ref[...] = y
    ob_ref[...] = y.astype(BF16)


def _layer_norm(x, g, b):
    t, d = x.shape
    tm = min(512, t)
    return pl.pallas_call(
        _ln_kernel,
        out_shape=(jax.ShapeDtypeStruct((t, d), F32), jax.ShapeDtypeStruct((t, d), BF16)),
        grid=(t // tm,),
        in_specs=[pl.BlockSpec((tm, d), lambda i: (i, 0)),
                  pl.BlockSpec((1, d), lambda i: (0, 0)),
                  pl.BlockSpec((1, d), lambda i: (0, 0))],
        out_specs=(pl.BlockSpec((tm, d), lambda i: (i, 0)), pl.BlockSpec((tm, d), lambda i: (i, 0))),
        compiler_params=_cparams("parallel"),
        name="entry_ln",
    )(x, g.reshape(1, d), b.reshape(1, d))


def _inproj_kernel(x_ref, w_ref, cos_ref, sin_ref, h_ref, dt_ref):
    j = pl.program_id(1)
    tm = x_ref.shape[0]
    x = x_ref[...]
    last = INPROJ_TILES - 1

    def rope_store(c, acc, scale):
        cs = cos_ref[...] * scale
        sn = sin_ref[...] * scale
        lane = lax.broadcasted_iota(I32, (tm, LANES), 1)
        first_half = (lane % DIFF_QK_DIM) < (DIFF_QK_DIM // 2)
        for hh in range(TN // LANES):
            a = acc[:, hh * LANES:(hh + 1) * LANES]
            rot = jnp.where(first_half, pltpu.roll(a, LANES - 32, 1), pltpu.roll(a, 32, 1))
            h_ref[:, c * TN + hh * LANES:c * TN + (hh + 1) * LANES] = (a * cs + rot * sn).astype(BF16)

    for c in range(INPROJ_TN // TN):
        acc = jnp.dot(x, w_ref[:, c * TN:(c + 1) * TN], preferred_element_type=F32)
        if c * TN == (COL_Q * TN) % INPROJ_TN or c * TN == (COL_K * TN) % INPROJ_TN:
            scale = LOG2E * DIFF_QK_DIM ** -0.5 if c * TN == (COL_Q * TN) % INPROJ_TN else 1.0

            @pl.when(j == last)
            def _():
                rope_store(c, acc, scale)

            @pl.when(j != last)
            def _():
                h_ref[:, c * TN:(c + 1) * TN] = acc.astype(BF16)
        else:
            h_ref[:, c * TN:(c + 1) * TN] = acc.astype(BF16)
        if c * TN == (COL_DT * TN) % INPROJ_TN:
            @pl.when(j == last)
            def _():
                dt_ref[...] = acc[:, 0:LANES]


def _in_projection(xb, w_tiles, cos_t, sin_t, seq):
    t, d = xb.shape
    tm = min(1024, seq)
    nseq = seq // tm
    return pl.pallas_call(
        _inproj_kernel,
        out_shape=(jax.ShapeDtypeStruct((t, H_COLS), BF16),
                   jax.ShapeDtypeStruct((t, LANES), F32)),
        grid=(t // tm, INPROJ_TILES),
        in_specs=[pl.BlockSpec((tm, d), lambda i, j: (i, 0)),
                  pl.BlockSpec((None, d, INPROJ_TN), lambda i, j: (j, 0, 0)),
                  pl.BlockSpec((tm, LANES), lambda i, j: (i % nseq, 0)),
                  pl.BlockSpec((tm, LANES), lambda i, j: (i % nseq, 0))],
        out_specs=(pl.BlockSpec((tm, INPROJ_TN), lambda i, j: (i, j)),
                   pl.BlockSpec((tm, LANES), lambda i, j: (i, 0))),
        compiler_params=_cparams("arbitrary", "arbitrary"),
        name="in_proj",
    )(xb, w_tiles, cos_t, sin_t)


def _ssd_kernel(xbc_ref, z_ref, dt_ref, cw_ref, cb_ref, dtb_ref, alog_ref, dtbT_ref, alogT_ref,
                dsk_ref, ng_ref, e64_ref, e128_ref, tri_ref, triT_ref, y_ref, xpad_ref, hst_ref):
    c = pl.program_id(1)
    L = SSD_L

    @pl.when(c == 0)
    def _():
        xpad_ref[0:SUBLANES, :] = jnp.zeros((SUBLANES, SSD_XBC), F32)
        hst_ref[...] = jnp.zeros(hst_ref.shape, F32)

    cur = xbc_ref[...].astype(F32)
    xpad_ref[SUBLANES:SUBLANES + L, :] = cur
    conv = cb_ref[...]
    for k in range(SSD_CONV):
        conv = conv + cw_ref[k:k + 1, :] * xpad_ref[pl.ds(SUBLANES - (SSD_CONV - 1) + k, L), :]
    xpad_ref[0:SUBLANES, :] = cur[L - SUBLANES:L, :]
    xbc = _silu(conv)
    X = xbc[:, 0:GROUP_W]
    Bm = xbc[:, GROUP_W:GROUP_W + 2 * SSD_STATE]
    Cm = xbc[:, GROUP_W + 2 * SSD_STATE:SSD_XBC]

    dt_raw = dt_ref[...]
    dt = _softplus(dt_raw + dtb_ref[...])
    acs = _dot_exact_lhs(tri_ref[...], dt * (-jnp.exp(alog_ref[...])))
    dtT = _softplus(dt_raw.T[0:16, :] + dtbT_ref[...])
    acsT = _dot_exact_rhs(dtT * (-jnp.exp(alogT_ref[...])), triT_ref[...])
    dt64 = _dot_exact_rhs(dt, e64_ref[...])
    acs64 = _dot_exact_rhs(acs, e64_ref[...])
    acs128 = _dot_exact_rhs(acs, e128_ref[...])

    last = acs64[L - 1:L, :]
    in_decay = jnp.exp(acs64)
    decay = jnp.exp(last - acs64)
    chunk_decay = jnp.exp(last)
    Xdt = X * dt64
    Xd = (Xdt * decay).astype(BF16)
    Xdt_b = Xdt.astype(BF16)

    row = lax.broadcasted_iota(I32, (L, L), 0)
    col = lax.broadcasted_iota(I32, (L, L), 1)
    causal = row >= col
    lane = lax.broadcasted_iota(I32, (L, LANES), 1)
    low_half = lane < 64

    y_parts = []
    for g in range(2):
        Bg = Bm[:, g * SSD_STATE:(g + 1) * SSD_STATE]
        Cg = Cm[:, g * SSD_STATE:(g + 1) * SSD_STATE].astype(BF16)
        cb = _dot_nt(Cg, Bg.astype(BF16))
        hprev = hst_ref[g]
        y_off = jnp.dot(Cg, hprev.astype(BF16), preferred_element_type=F32) * in_decay[:, g * 256:(g + 1) * 256]
        for pr in range(2):
            xp = Xdt_b[:, (2 * g + pr) * LANES:(2 * g + pr + 1) * LANES]
            ys = []
            for hh in range(2):
                h = 4 * g + 2 * pr + hh
                seg = acs128[:, h * LANES:(h + 1) * LANES] - acsT[h:h + 1, :]
                lmat = jnp.exp(jnp.where(causal, seg, -jnp.inf))
                w = (cb * lmat).astype(BF16)
                ys.append(jnp.dot(w, xp, preferred_element_type=F32))
            y_parts.append(jnp.where(low_half, ys[0], ys[1]) + y_off[:, pr * LANES:(pr + 1) * LANES])
        states = jnp.dot(Bg.T.astype(BF16), Xd[:, g * 256:(g + 1) * 256], preferred_element_type=F32)
        hst_ref[g] = hprev * chunk_decay[:, g * 256:(g + 1) * 256] + states

    z = z_ref[...].astype(F32)
    gate = _silu(z)
    for g in range(2):
        yg = jnp.concatenate(y_parts[2 * g:2 * g + 2], axis=1) + dsk_ref[:, g * 256:(g + 1) * 256] * X[:, g * 256:(g + 1) * 256]
        yg = yg * gate[:, g * 256:(g + 1) * 256]
        ms = jnp.mean(yg * yg, axis=-1, keepdims=True)
        y_ref[:, g * 256:(g + 1) * 256] = (yg * lax.rsqrt(ms + LN_EPS) * ng_ref[:, g * 256:(g + 1) * 256]).astype(BF16)


def _ssd_constants():
    e64 = np.zeros((LANES, 512), np.float32)
    e128 = np.zeros((LANES, 1024), np.float32)
    for h in range(SSD_HEADS):
        e64[h, 64 * h:64 * (h + 1)] = 1.0
        e128[h, 128 * h:128 * (h + 1)] = 1.0
    tri = np.tril(np.ones((SSD_L, SSD_L), np.float32))
    return (jnp.asarray(e64, BF16), jnp.asarray(e128, BF16), jnp.asarray(tri, BF16), jnp.asarray(tri.T, BF16))


def _ssd_mixer(h, dt_raw, conv_w, conv_b, dt_bias, a_log, d_skip, norm_g, batch, seq):
    t = h.shape[0]
    L = SSD_L
    nc = seq // L
    e64, e128, tri, triT = _ssd_constants()
    pad_l = lambda v: jnp.pad(v.reshape(1, SSD_HEADS), ((0, 0), (0, LANES - SSD_HEADS)))
    pad_t = lambda v: jnp.pad(jnp.broadcast_to(v.reshape(SSD_HEADS, 1), (SSD_HEADS, L)), ((0, 16 - SSD_HEADS), (0, 0)))
    rb = lambda b, c: b * nc + c
    const = lambda shape: pl.BlockSpec(shape, lambda b, c: (0,) * len(shape))
    return pl.pallas_call(
        _ssd_kernel,
        out_shape=jax.ShapeDtypeStruct((t, GROUP_W), BF16),
        grid=(batch, nc),
        in_specs=[pl.BlockSpec((L, SSD_XBC), lambda b, c: (rb(b, c), COL_XBC // 2)),
                  pl.BlockSpec((L, GROUP_W), lambda b, c: (rb(b, c), COL_Z)),
                  pl.BlockSpec((L, LANES), lambda b, c: (rb(b, c), 0)),
                  const((SSD_CONV, SSD_XBC)), const((1, SSD_XBC)),
                  const((1, LANES)), const((1, LANES)), const((16, L)), const((16, L)),
                  const((1, GROUP_W)), const((1, GROUP_W)),
                  const((LANES, 512)), const((LANES, 1024)), const((L, L)), const((L, L))],
        out_specs=pl.BlockSpec((L, GROUP_W), lambda b, c: (rb(b, c), 0)),
        scratch_shapes=[pltpu.VMEM((SUBLANES + L, SSD_XBC), F32),
                        pltpu.VMEM((2, SSD_STATE, 256), F32)],
        compiler_params=_cparams("arbitrary", "arbitrary"),
        name="ssd_mixer",
    )(h, h, dt_raw, conv_w, conv_b.reshape(1, SSD_XBC),
      pad_l(dt_bias), pad_l(a_log), pad_t(dt_bias), pad_t(a_log),
      jnp.repeat(d_skip, 64).reshape(1, GROUP_W), norm_g.reshape(1, GROUP_W), e64, e128, tri, triT)


def _attn_kernel(q_ref, k_ref, v_ref, bias_ref, lq1_ref, lk1_ref, lq2_ref, lk2_ref, g_ref, o_ref,
                 m_ref, l_ref, acc_ref, *, lambda_init):
    qi = pl.program_id(2)
    tq = q_ref.shape[0]
    tk = tq
    q = q_ref[...]
    lane = lax.broadcasted_iota(I32, (tq, LANES), 1)
    zero = jnp.zeros_like(q)
    qs = (jnp.where(lane < DIFF_QK_DIM, q, zero), jnp.where(lane >= DIFF_QK_DIM, q, zero))
    m_ref[...] = jnp.full(m_ref.shape, -jnp.inf, F32)
    l_ref[...] = jnp.zeros(l_ref.shape, F32)
    acc_ref[...] = jnp.zeros(acc_ref.shape, F32)

    def tile(off, bias):
        k = k_ref[pl.ds(off, tk), :]
        v = v_ref[pl.ds(off, tk), :]
        for m in range(2):
            s = _dot_nt(qs[m], k)
            if bias is not None:
                s = s + bias
            m_prev = m_ref[m]
            m_new = jnp.maximum(m_prev, jnp.max(s, axis=-1, keepdims=True))
            alpha = jnp.exp2(m_prev - m_new)
            p = jnp.exp2(s - jnp.concatenate([m_new] * (tk // LANES), axis=1))
            psum = p[:, 0:LANES]
            for c in range(1, tk // LANES):
                psum = psum + p[:, c * LANES:(c + 1) * LANES]
            l_ref[m] = alpha * l_ref[m] + psum
            acc_ref[m] = alpha * acc_ref[m] + jnp.dot(p.astype(BF16), v, preferred_element_type=F32)
            m_ref[m] = m_new

    def body(j, carry):
        tile(pl.multiple_of(j * tk, tk), None)
        return carry

    lax.fori_loop(0, qi, body, 0)
    tile(pl.multiple_of(qi * tk, tk), bias_ref[...])

    lam = (jnp.exp(jnp.sum(lq1_ref[...] * lk1_ref[...], axis=-1, keepdims=True))
           - jnp.exp(jnp.sum(lq2_ref[...] * lk2_ref[...], axis=-1, keepdims=True)) + lambda_init)
    l0 = jnp.sum(l_ref[0], axis=-1, keepdims=True)
    l1 = jnp.sum(l_ref[1], axis=-1, keepdims=True)
    o = acc_ref[0] / l0 - lam * (acc_ref[1] / l1)
    ms = jnp.mean(o * o, axis=-1, keepdims=True)
    o_ref[...] = (o * lax.rsqrt(ms + LN_EPS) * g_ref[...] * (1.0 - lambda_init)).astype(BF16)


def _chunk_mask_bias(tq):
    r = np.arange(tq)[:, None] // CHUNK
    c = np.arange(tq)[None, :] // CHUNK
    return jnp.asarray(np.where(c <= r, 0.0, -np.inf), F32)


def _diff_attention(h, lq1, lk1, lq2, lk2, norm_g, lambda_init, batch, seq):
    t = h.shape[0]
    tq = min(512, seq)
    nq = seq // tq
    qcol = COL_Q * 4
    kcol = COL_K * 4
    vcol = COL_V * 4
    vec = lambda n: pl.BlockSpec((1, n), lambda b, hd, qi: (0, 0))
    return pl.pallas_call(
        functools.partial(_attn_kernel, lambda_init=lambda_init),
        out_shape=jax.ShapeDtypeStruct((t, GROUP_W), BF16),
        grid=(batch, DIFF_HEADS, nq),
        in_specs=[pl.BlockSpec((tq, LANES), lambda b, hd, qi: (b * nq + qi, qcol + hd)),
                  pl.BlockSpec((seq, LANES), lambda b, hd, qi: (b, kcol + hd)),
                  pl.BlockSpec((seq, LANES), lambda b, hd, qi: (b, vcol + hd)),
                  pl.BlockSpec((tq, tq), lambda b, hd, qi: (0, 0)),
                  vec(DIFF_QK_DIM), vec(DIFF_QK_DIM), vec(DIFF_QK_DIM), vec(DIFF_QK_DIM), vec(DIFF_V_DIM)],
        out_specs=pl.BlockSpec((tq, LANES), lambda b, hd, qi: (b * nq + qi, hd)),
        scratch_shapes=[pltpu.VMEM((2, tq, LANES), F32), pltpu.VMEM((2, tq, LANES), F32),
                        pltpu.VMEM((2, tq, DIFF_V_DIM), F32)],
        compiler_params=_cparams("arbitrary", "arbitrary", "arbitrary"),
        name="diff_attn",
    )(h, h, h, _chunk_mask_bias(tq), lq1.reshape(1, -1), lk1.reshape(1, -1), lq2.reshape(1, -1),
      lk2.reshape(1, -1), norm_g.reshape(1, -1))


CONF_HALO = 32
SC_HALO = 16
CONV_ROWS = 64


def _conv_kernel(u_ref, uh_ref, bg_ref, cg_ref, hh_ref, cgh_ref, hhh_ref,
                 dww_ref, dwb_ref, lng_ref, lnb_ref, pww_ref, pwb_ref, scw_ref,
                 yc_ref, yd_ref, hbuf_ref, pbuf_ref, cbuf_ref, sbuf_ref):
    i = pl.program_id(1)
    tm = u_ref.shape[0]

    def glu(u):
        u = u.astype(F32)
        return u[:, 0:GROUP_W] * _sigmoid(u[:, GROUP_W:2 * GROUP_W])

    first = (i == 0)
    hbuf_ref[0:CONF_HALO, :] = jnp.where(first, 0.0, glu(uh_ref[...]))
    hbuf_ref[CONF_HALO:CONF_HALO + tm, :] = glu(u_ref[...])
    pbuf_ref[0:SC_HALO, :] = jnp.where(first, 0.0, cgh_ref[...].astype(F32) * hhh_ref[...].astype(F32))
    pbuf_ref[SC_HALO:SC_HALO + tm, :] = cg_ref[...].astype(F32) * hh_ref[...].astype(F32)

    for b in range(1, SUBLANES):
        sbuf_ref[b - 1, 0:tm + CONF_HALO - SUBLANES, :] = hbuf_ref[pl.ds(b, tm + CONF_HALO - SUBLANES), :]
    for r0 in range(0, tm, CONV_ROWS):
        acc = jnp.zeros((CONV_ROWS, GROUP_W), F32) + dwb_ref[...]
        for k in range(CONF_KERNEL):
            off = CONF_HALO - (CONF_KERNEL - 1) + k
            phase, base = off % SUBLANES, r0 + off - off % SUBLANES
            src = hbuf_ref[pl.ds(base, CONV_ROWS), :] if phase == 0 else sbuf_ref[phase - 1, pl.ds(base, CONV_ROWS), :]
            acc = acc + dww_ref[k:k + 1, :] * src
        cbuf_ref[r0:r0 + CONV_ROWS, :] = acc
    hc = cbuf_ref[...]
    mu = jnp.mean(hc, axis=-1, keepdims=True)
    xc = hc - mu
    var = jnp.mean(xc * xc, axis=-1, keepdims=True)
    hn = _silu(xc * lax.rsqrt(var + LN_EPS) * lng_ref[...] + lnb_ref[...])
    yc = jnp.dot(hn.astype(BF16), pww_ref[...], preferred_element_type=F32) + pwb_ref[...]
    yc_ref[...] = yc.astype(BF16)

    sc = jnp.zeros((tm, GROUP_W), F32)
    for k in range(SC_KERNEL):
        sc = sc + scw_ref[k:k + 1, :] * pbuf_ref[pl.ds(SC_HALO - (SC_KERNEL - 1) + k, tm), :]
    yd_ref[...] = (bg_ref[...].astype(F32) * sc).astype(BF16)


def _conv_mixers(h, dw_w, dw_b, ln_g, ln_b, pw_w, pw_b, sc_w, batch, seq):
    t = h.shape[0]
    tm = min(512, seq)
    nt = seq // tm
    rb = lambda b, i: b * nt + i
    halo = lambda rows: (lambda b, i: jnp.maximum(rb(b, i) * (tm // rows) - 1, 0))
    hc, hs = halo(CONF_HALO), halo(SC_HALO)
    const = lambda shape: pl.BlockSpec(shape, lambda b, i: (0,) * len(shape))
    return pl.pallas_call(
        _conv_kernel,
        out_shape=(jax.ShapeDtypeStruct((t, GROUP_W), BF16), jax.ShapeDtypeStruct((t, GROUP_W), BF16)),
        grid=(batch, nt),
        in_specs=[pl.BlockSpec((tm, 2 * GROUP_W), lambda b, i: (rb(b, i), COL_CONF // 2)),
                  pl.BlockSpec((CONF_HALO, 2 * GROUP_W), lambda b, i: (hc(b, i), COL_CONF // 2)),
                  pl.BlockSpec((tm, GROUP_W), lambda b, i: (rb(b, i), COL_BG)),
                  pl.BlockSpec((tm, GROUP_W), lambda b, i: (rb(b, i), COL_CG)),
                  pl.BlockSpec((tm, GROUP_W), lambda b, i: (rb(b, i), COL_HH)),
                  pl.BlockSpec((SC_HALO, GROUP_W), lambda b, i: (hs(b, i), COL_CG)),
                  pl.BlockSpec((SC_HALO, GROUP_W), lambda b, i: (hs(b, i), COL_HH)),
                  const((CONF_KERNEL, GROUP_W)), const((1, GROUP_W)), const((1, GROUP_W)), const((1, GROUP_W)),
                  const((GROUP_W, GROUP_W)), const((1, GROUP_W)), const((SC_KERNEL, GROUP_W))],
        out_specs=(pl.BlockSpec((tm, GROUP_W), lambda b, i: (rb(b, i), 0)),
                   pl.BlockSpec((tm, GROUP_W), lambda b, i: (rb(b, i), 0))),
        scratch_shapes=[pltpu.VMEM((CONF_HALO + tm, GROUP_W), F32),
                        pltpu.VMEM((SC_HALO + tm, GROUP_W), F32),
                        pltpu.VMEM((tm, GROUP_W), F32),
                        pltpu.VMEM((SUBLANES - 1, CONF_HALO + tm, GROUP_W), F32)],
        compiler_params=_cparams("arbitrary", "arbitrary"),
        name="conv_mixers",
    )(h, h, h, h, h, h, h, dw_w, dw_b.reshape(1, -1), ln_g.reshape(1, -1), ln_b.reshape(1, -1),
      pw_w.astype(BF16), pw_b.reshape(1, -1), sc_w)


def _first_max4(vals):
    m1 = jnp.maximum(jnp.maximum(vals[0], vals[1]), jnp.maximum(vals[2], vals[3]))
    i1 = jnp.where(vals[0] == m1, 0, jnp.where(vals[1] == m1, 1, jnp.where(vals[2] == m1, 2, 3)))
    rest = [jnp.where(i1 == j, -jnp.inf, vals[j]) for j in range(4)]
    m2 = jnp.maximum(jnp.maximum(rest[0], rest[1]), jnp.maximum(rest[2], rest[3]))
    i2 = jnp.where(rest[0] == m2, 0, jnp.where(rest[1] == m2, 1, jnp.where(rest[2] == m2, 2, 3)))
    return m1, i1, m2, i2


def _outproj_kernel(ya_ref, yb_ref, yc_ref, yd_ref, x_ref, w_ref, g_ref, b_ref, x1_ref):
    mix = jnp.dot(ya_ref[...], w_ref[0:GROUP_W, :], preferred_element_type=F32)
    mix = mix + jnp.dot(yb_ref[...], w_ref[GROUP_W:2 * GROUP_W, :], preferred_element_type=F32)
    mix = mix + jnp.dot(yc_ref[...], w_ref[2 * GROUP_W:3 * GROUP_W, :], preferred_element_type=F32)
    mix = mix + jnp.dot(yd_ref[...], w_ref[3 * GROUP_W:4 * GROUP_W, :], preferred_element_type=F32)
    y = ALPHA * x_ref[...] + mix
    mu = jnp.mean(y, axis=-1, keepdims=True)
    yc = y - mu
    var = jnp.mean(yc * yc, axis=-1, keepdims=True)
    x1_ref[...] = yc * lax.rsqrt(var + LN_EPS) * g_ref[...] + b_ref[...]


def _router_kernel(x_ref, w2_ref, w1_ref, rb_ref, eid_ref, gate_ref):
    tm = x_ref.shape[0]
    x = x_ref[...]
    xh = x.astype(BF16)
    xm = (x - xh.astype(F32)).astype(BF16)
    r = (jnp.dot(xh, w2_ref[...], preferred_element_type=F32)
         + jnp.dot(xm, w1_ref[...], preferred_element_type=F32))
    r = r + pltpu.roll(r, LANES - N_EXPERTS, 1)
    logits = r.T[0:N_EXPERTS, :]
    aff = _sigmoid(logits)
    sel = aff + jnp.concatenate([rb_ref[...]] * (tm // LANES), axis=1)
    rows = [sel[e:e + 1, :] for e in range(N_EXPERTS)]
    arow = [aff[e:e + 1, :] for e in range(N_EXPERTS)]
    tops = [_first_max4(rows[4 * g:4 * g + 4]) for g in range(N_GROUPS)]
    score = [tp[0] + tp[2] for tp in tops]
    best = jnp.maximum(jnp.maximum(score[0], score[1]), jnp.maximum(score[2], score[3]))
    grp = jnp.where(score[0] == best, 0, jnp.where(score[1] == best, 1, jnp.where(score[2] == best, 2, 3)))
    pick = lambda idx: jnp.where(grp == 0, tops[0][idx], jnp.where(grp == 1, tops[1][idx],
                                 jnp.where(grp == 2, tops[2][idx], tops[3][idx])))
    e1 = grp * EXPERTS_PER_GROUP + pick(1)
    e2 = grp * EXPERTS_PER_GROUP + pick(3)
    a1 = jnp.zeros_like(best)
    a2 = jnp.zeros_like(best)
    for e in range(N_EXPERTS):
        a1 = jnp.where(e1 == e, arow[e], a1)
        a2 = jnp.where(e2 == e, arow[e], a2)
    den = a1 + a2
    zi = jnp.zeros((SUBLANES - 2, tm), I32)
    zf = jnp.zeros((SUBLANES - 2, tm), F32)
    eid_ref[...] = jnp.concatenate([e1.astype(I32), e2.astype(I32), zi], axis=0)
    gate_ref[...] = jnp.concatenate([a1 / den, a2 / den, zf], axis=0)


def _out_projection(ya, yb, yc, yd, x, w_out, ln_g, ln_b, layer):
    t, d = x.shape
    tm = min(512, t)
    act = lambda: pl.BlockSpec((tm, GROUP_W), lambda i: (i, 0))
    const = lambda shape: pl.BlockSpec(shape, lambda i: (0,) * len(shape))
    return pl.pallas_call(
        _outproj_kernel,
        out_shape=jax.ShapeDtypeStruct((t, d), F32),
        grid=(t // tm,),
        in_specs=[act(), act(), act(), act(),
                  pl.BlockSpec((tm, d), lambda i: (i, 0)),
                  pl.BlockSpec((None, d, d), lambda i: (layer, 0, 0)), const((1, d)), const((1, d))],
        out_specs=pl.BlockSpec((tm, d), lambda i: (i, 0)),
        compiler_params=_cparams("parallel"),
        name="out_proj",
    )(ya, yb, yc, yd, x, w_out, ln_g.reshape(1, d), ln_b.reshape(1, d))


def _router(x1, router_w, router_bias):
    t, d = x1.shape
    tm = min(512, t)
    w_hi = router_w.astype(BF16)
    w_mid = (router_w - w_hi.astype(F32)).astype(BF16)
    w2 = jnp.concatenate([w_hi, w_mid, jnp.zeros((d, LANES - 2 * N_EXPERTS), BF16)], axis=1)
    w1 = jnp.concatenate([w_hi, jnp.zeros((d, LANES - N_EXPERTS), BF16)], axis=1)
    router_b = jnp.broadcast_to(router_bias.reshape(N_EXPERTS, 1), (N_EXPERTS, LANES))
    const = lambda shape: pl.BlockSpec(shape, lambda i: (0,) * len(shape))
    return pl.pallas_call(
        _router_kernel,
        out_shape=(jax.ShapeDtypeStruct((SUBLANES, t), I32), jax.ShapeDtypeStruct((SUBLANES, t), F32)),
        grid=(t // tm,),
        in_specs=[pl.BlockSpec((tm, d), lambda i: (i, 0)),
                  const((d, LANES)), const((d, LANES)), const((N_EXPERTS, LANES))],
        out_specs=(pl.BlockSpec((SUBLANES, tm), lambda i: (0, i)),
                   pl.BlockSpec((SUBLANES, tm), lambda i: (0, i))),
        compiler_params=_cparams("parallel"),
        name="router",
    )(x1, w2, w1, router_b)


def _row_copy(src_hbm, tok_ref, dst_ref, sem, r, base=0):
    tok = tok_ref[0, 0, base + r]
    return pltpu.make_async_copy(src_hbm.at[pl.ds(tok, 1), :], dst_ref.at[pl.ds(r, 1), :], sem)


def _moe_kernel(blk_exp_ref, nused_ref, tok_ref, tokn_ref, x_hbm, wg_ref, wu_ref, wd_ref, y_ref,
                xbuf0_ref, xbuf1_ref, sem_ref):
    i = pl.program_id(0)
    nused = nused_ref[0]
    bufs = (xbuf0_ref, xbuf1_ref)

    def issue(tref, s):
        for r in range(MOE_BLOCK):
            _row_copy(x_hbm, tref, bufs[s], sem_ref.at[s], r).start(priority=r % 2)

    def wait(tref, s):
        for r in range(MOE_BLOCK):
            _row_copy(x_hbm, tref, bufs[s], sem_ref.at[s], r).wait()

    @pl.when(i == 0)
    def _():
        issue(tok_ref, 0)

    for s in range(2):
        @pl.when(jnp.logical_and(i < nused, i % 2 == s))
        def _():
            wait(tok_ref, s)
            issue(tokn_ref, 1 - s)
            x = bufs[s][...].astype(BF16)
            hg = jnp.dot(x, wg_ref[...], preferred_element_type=F32)
            hu = jnp.dot(x, wu_ref[...], preferred_element_type=F32)
            hid = (_silu(hg) * hu).astype(BF16)
            y = jnp.dot(hid, wd_ref[...], preferred_element_type=F32)
            for c in range(ROW_CHUNKS):
                y_ref[pl.ds(c, MOE_BLOCK, stride=ROW_CHUNKS), :] = y[:, c * LANES:(c + 1) * LANES]

        @pl.when(jnp.logical_and(i + 1 == nused, i % 2 == s))
        def _():
            wait(tokn_ref, 1 - s)

    @pl.when(i >= nused)
    def _():
        y_ref[...] = jnp.zeros(y_ref.shape, F32)


def _moe_experts(x1, slot_tok, blk_exp, nused, wg, wu, wd, layer):
    t, d = x1.shape
    n_blk = slot_tok.shape[0]
    tok_spec = lambda off: pl.BlockSpec((1, 1, MOE_BLOCK),
                                        lambda i, be, nu: (jnp.minimum(i + off, nu[0] - 1), 0, 0),
                                        memory_space=pltpu.SMEM)
    return pl.pallas_call(
        _moe_kernel,
        out_shape=jax.ShapeDtypeStruct((n_blk * MOE_BLOCK * ROW_CHUNKS, LANES), F32),
        grid_spec=pltpu.PrefetchScalarGridSpec(
            num_scalar_prefetch=2,
            grid=(n_blk,),
            in_specs=[tok_spec(0), tok_spec(1),
                      pl.BlockSpec(memory_space=pl.ANY),
                      pl.BlockSpec((None, None, d, D_EXPERT), lambda i, be, nu: (layer, be[i], 0, 0)),
                      pl.BlockSpec((None, None, d, D_EXPERT), lambda i, be, nu: (layer, be[i], 0, 0)),
                      pl.BlockSpec((None, None, D_EXPERT, d), lambda i, be, nu: (layer, be[i], 0, 0))],
            out_specs=pl.BlockSpec((MOE_BLOCK * ROW_CHUNKS, LANES), lambda i, be, nu: (i, 0)),
            scratch_shapes=[pltpu.VMEM((MOE_BLOCK, d), F32), pltpu.VMEM((MOE_BLOCK, d), F32),
                            pltpu.SemaphoreType.DMA((2,))]),
        compiler_params=_cparams("arbitrary"),
        name="moe_experts",
    )(blk_exp, nused, slot_tok, slot_tok, x1, wg, wu, wd)


COMBINE_ROWS = 256


def _combine_kernel(pos_ref, posn_ref, y_hbm, x_ref, gt_ref, g_ref, b_ref, o_ref, ob_ref, ybuf_ref, sem_ref):
    i = pl.program_id(0)
    n = pl.num_programs(0)
    slot = i % 2
    tm = x_ref.shape[0]

    def slot_copy(pref, s, k, r):
        pos = pref[0, 0, k * tm + r]
        return pltpu.make_async_copy(y_hbm.at[pl.ds(pl.multiple_of(pos * ROW_CHUNKS, ROW_CHUNKS), ROW_CHUNKS), :],
                                     ybuf_ref.at[s, k, pl.ds(pl.multiple_of(r * ROW_CHUNKS, ROW_CHUNKS), ROW_CHUNKS), :],
                                     sem_ref.at[s])

    def issue(pref, s):
        def body(r8, carry):
            for u in range(SUBLANES):
                for k in range(TOP_K):
                    slot_copy(pref, s, k, r8 * SUBLANES + u).start(priority=k)
            return carry
        lax.fori_loop(0, tm // SUBLANES, body, 0)

    def wait(pref, s):
        def body(r8, carry):
            for u in range(SUBLANES):
                for k in range(TOP_K):
                    slot_copy(pref, s, k, r8 * SUBLANES + u).wait()
            return carry
        lax.fori_loop(0, tm // SUBLANES, body, 0)

    @pl.when(i == 0)
    def _():
        issue(pos_ref, 0)

    @pl.when(i + 1 < n)
    def _():
        issue(posn_ref, 1 - slot)

    wait(pos_ref, slot)
    gt = gt_ref[...]

    def rows(k):
        return jnp.concatenate([ybuf_ref[slot, k, pl.ds(c, tm, stride=ROW_CHUNKS), :] for c in range(ROW_CHUNKS)],
                               axis=1)

    moe = rows(0) * gt[:, 0:1] + rows(1) * gt[:, 1:2]
    y = ALPHA * x_ref[...] + moe
    mu = jnp.mean(y, axis=-1, keepdims=True)
    yc = y - mu
    var = jnp.mean(yc * yc, axis=-1, keepdims=True)
    out = yc * lax.rsqrt(var + LN_EPS) * g_ref[...] + b_ref[...]
    o_ref[...] = out
    ob_ref[...] = out.astype(BF16)


def _moe_combine(y_pad, pos_tiles, x1, gate_tok, ln_g, ln_b):
    t, d = x1.shape
    tm = min(COMBINE_ROWS, t)
    n = t // tm
    pos_spec = lambda off: pl.BlockSpec((1, 1, TOP_K * tm), lambda i: (jnp.minimum(i + off, n - 1), 0, 0),
                                        memory_space=pltpu.SMEM)
    return pl.pallas_call(
        _combine_kernel,
        out_shape=(jax.ShapeDtypeStruct((t, d), F32), jax.ShapeDtypeStruct((t, d), BF16)),
        grid=(n,),
        in_specs=[pos_spec(0), pos_spec(1),
                  pl.BlockSpec(memory_space=pl.ANY),
                  pl.BlockSpec((tm, d), lambda i: (i, 0)),
                  pl.BlockSpec((tm, TOP_K), lambda i: (i, 0)),
                  pl.BlockSpec((1, d), lambda i: (0, 0)),
                  pl.BlockSpec((1, d), lambda i: (0, 0))],
        out_specs=(pl.BlockSpec((tm, d), lambda i: (i, 0)), pl.BlockSpec((tm, d), lambda i: (i, 0))),
        scratch_shapes=[pltpu.VMEM((2, TOP_K, tm * ROW_CHUNKS, LANES), F32), pltpu.SemaphoreType.DMA((2,))],
        compiler_params=_cparams("arbitrary"),
        name="moe_combine",
    )(pos_tiles, pos_tiles, y_pad, x1, gate_tok, ln_g.reshape(1, d), ln_b.reshape(1, d))


def _routing_tables(eid, t):
    n_slots = t * TOP_K
    e_flat = eid.T.reshape(-1)
    onehot = (e_flat[:, None] == jnp.arange(N_EXPERTS, dtype=I32)[None, :]).astype(I32)
    csum = jnp.cumsum(onehot, axis=0)
    counts = csum[-1]
    rank = jnp.take_along_axis(csum, e_flat[:, None], axis=1)[:, 0] - 1
    padded = ((counts + MOE_BLOCK - 1) // MOE_BLOCK) * MOE_BLOCK
    pends = jnp.cumsum(padded)
    pstarts = pends - padded
    pos = (pstarts[e_flat] + rank).astype(I32)
    n_blk = -(-n_slots // MOE_BLOCK) + N_EXPERTS
    slot_tok = jnp.zeros((n_blk * MOE_BLOCK,), I32).at[pos].set(jnp.arange(n_slots, dtype=I32) // TOP_K)
    blk_start = jnp.arange(n_blk, dtype=I32) * MOE_BLOCK
    blk_exp = jnp.minimum(jnp.sum((pends[None, :] <= blk_start[:, None]).astype(I32), axis=1),
                          N_EXPERTS - 1).astype(I32)
    nused = (pends[-1] // MOE_BLOCK).astype(I32).reshape(1)
    return slot_tok.reshape(n_blk, 1, MOE_BLOCK), blk_exp, nused, pos


def _rope_tables(seq):
    half = DIFF_QK_DIM // 2
    inv = 1.0 / (ROPE_THETA ** (jnp.arange(0, DIFF_QK_DIM, 2, dtype=F32) / DIFF_QK_DIM))
    ang = jnp.arange(seq, dtype=F32)[:, None] * inv[None, :]
    cos, sin = jnp.cos(ang), jnp.sin(ang)
    cos_t = jnp.concatenate([cos, cos, cos, cos], axis=1)
    sin_t = jnp.concatenate([-sin, sin, -sin, sin], axis=1)
    return cos_t, sin_t


W_IN_SPLITS = (0, 512, 1536, 1544, 2056, 2568, 3080, 4104, 5640)


def _wprep_kernel(w_ref, o_ref):
    w = w_ref[...]
    z, xbc, dt, q, k, v, conf, sc = [w[:, a:b] for a, b in zip(W_IN_SPLITS[:-1], W_IN_SPLITS[1:])]
    rows = w.shape[0]
    o_ref[0] = jnp.concatenate([xbc, conf], axis=1).astype(BF16)
    o_ref[1] = jnp.concatenate([sc, z], axis=1).astype(BF16)
    o_ref[2] = jnp.concatenate([q, k, v, dt, jnp.zeros((rows, TN - SSD_HEADS), F32)], axis=1).astype(BF16)


def _prep_w_in(w_all, layer):
    _, d, n = w_all.shape
    rows = 256
    return pl.pallas_call(
        _wprep_kernel,
        out_shape=jax.ShapeDtypeStruct((INPROJ_TILES, d, INPROJ_TN), BF16),
        grid=(d // rows,),
        in_specs=[pl.BlockSpec((None, rows, n), lambda i: (layer, i, 0))],
        out_specs=pl.BlockSpec((INPROJ_TILES, rows, INPROJ_TN), lambda i: (0, i, 0)),
        compiler_params=_cparams("parallel"),
        name="w_in_prep",
    )(w_all)


def kernel(x, ln_in_g, ln_in_b, w_in, ssd_conv_w, ssd_conv_b, ssd_dt_bias, ssd_a_log, ssd_d, ssd_norm_g, diff_lq1, diff_lk1, diff_lq2, diff_lk2, diff_norm_g, conf_dw_w, conf_dw_b, conf_ln_g, conf_ln_b, conf_pw_w, conf_pw_b, sc_conv_w, w_out, ln1_g, ln1_b, router_w, router_bias, moe_w_gate, moe_w_up, moe_w_down, ln2_g, ln2_b):
    b, s, d = x.shape
    t = b * s
    cos_t, sin_t = _rope_tables(s)
    xf, xb = _layer_norm(x.reshape(t, d), ln_in_g, ln_in_b)
    w_out_b = w_out.astype(BF16)
    wg_b, wu_b, wd_b = moe_w_gate.astype(BF16), moe_w_up.astype(BF16), moe_w_down.astype(BF16)
    for l in range(DEPTH):
        lambda_init = 0.8 - 0.6 * math.exp(-0.3 * l)
        h, dt_raw = _in_projection(xb, _prep_w_in(w_in, l), cos_t, sin_t, s)
        ya = _ssd_mixer(h, dt_raw, ssd_conv_w[l], ssd_conv_b[l], ssd_dt_bias[l], ssd_a_log[l],
                        ssd_d[l], ssd_norm_g[l], b, s)
        yb = _diff_attention(h, diff_lq1[l], diff_lk1[l], diff_lq2[l], diff_lk2[l], diff_norm_g[l],
                             lambda_init, b, s)
        yc, yd = _conv_mixers(h, conf_dw_w[l], conf_dw_b[l], conf_ln_g[l], conf_ln_b[l], conf_pw_w[l],
                              conf_pw_b[l], sc_conv_w[l], b, s)
        x1 = _out_projection(ya, yb, yc, yd, xf, w_out_b, ln1_g[l], ln1_b[l], l)
        eid, gate = _router(x1, router_w, router_bias)
        slot_tok, blk_exp, nused, pos = _routing_tables(eid[0:TOP_K], t)
        y_pad = _moe_experts(x1, slot_tok, blk_exp, nused, wg_b, wu_b, wd_b, l)
        tmc = min(COMBINE_ROWS, t)
        pos_tiles = pos.reshape(t // tmc, tmc, TOP_K).transpose(0, 2, 1).reshape(t // tmc, 1, TOP_K * tmc)
        xf, xb = _moe_combine(y_pad, pos_tiles, x1, gate[0:TOP_K].T, ln2_g[l], ln2_b[l])
    return xf.reshape(b, s, d)
```

```python
import functools
import math

import numpy as np
import jax
import jax.numpy as jnp
from jax import lax
from jax.experimental import pallas as pl
from jax.experimental.pallas import tpu as pltpu

F32 = jnp.float32
BF16 = jnp.bfloat16
I32 = jnp.int32

D_MODEL = 2048
DEPTH = 2
CHUNK = 64
GROUP_W = 512
SSD_HEADS = 8
SSD_STATE = 128
SSD_CONV = 4
SSD_XBC = 1024
DIFF_HEADS = 4
DIFF_QK_DIM = 64
DIFF_V_DIM = 128
ROPE_THETA = 10000.0
CONF_KERNEL = 31
SC_KERNEL = 3
N_EXPERTS = 16
N_GROUPS = 4
EXPERTS_PER_GROUP = 4
TOP_K = 2
D_EXPERT = 1024
MOE_BLOCK = 512
ALPHA = (2 * DEPTH) ** 0.25
LN_EPS = 1e-5
LOG2E = 1.4426950408889634

LANES = 128
SUBLANES = 8
VMEM_LIMIT_BYTES = 56 * 1024 * 1024

COL_XBC, COL_CONF, COL_BG, COL_CG, COL_HH, COL_Z, COL_Q, COL_K, COL_V, COL_DT = 0, 2, 4, 5, 6, 7, 8, 9, 10, 11
H_COLS = 12 * 512
TN = 512
INPROJ_TN = 2048
INPROJ_TILES = H_COLS // INPROJ_TN
SSD_L = 128
ATTN_HEADS = 2


def _cparams(*sem):
    return pltpu.CompilerParams(dimension_semantics=tuple(sem), vmem_limit_bytes=VMEM_LIMIT_BYTES)


def _sigmoid(x):
    return 1.0 / (1.0 + jnp.exp(-x))


def _silu(x):
    return x * _sigmoid(x)


def _softplus(x):
    return jnp.maximum(x, 0.0) + jnp.log(1.0 + jnp.exp(-jnp.abs(x)))


def _split3(v):
    hi = v.astype(BF16)
    r = v - hi.astype(F32)
    mid = r.astype(BF16)
    lo = (r - mid.astype(F32)).astype(BF16)
    return hi, mid, lo


def _dot_exact_rhs(v, m):
    return sum(jnp.dot(p, m, preferred_element_type=F32) for p in _split3(v))


def _dot_exact_lhs(m, v):
    return sum(jnp.dot(m, p, preferred_element_type=F32) for p in _split3(v))


def _dot_nt(a, b):
    return lax.dot_general(a, b, (((1,), (1,)), ((), ())), preferred_element_type=F32)


def _ln_kernel(x_ref, g_ref, b_ref, o_ref, ob_ref):
    x = x_ref[...]
    mu = jnp.mean(x, axis=-1, keepdims=True)
    xc = x - mu
    var = jnp.mean(xc * xc, axis=-1, keepdims=True)
    y = xc * lax.rsqrt(var + LN_EPS) * g_ref[...] + b_ref[...]
    o_ref[...] = y
    ob_ref[...] = y.astype(BF16)


def _layer_norm(x, g, b):
    t, d = x.shape
    tm = min(512, t)
    return pl.pallas_call(
        _ln_kernel,
        out_shape=(jax.ShapeDtypeStruct((t, d), F32), jax.ShapeDtypeStruct((t, d), BF16)),
        grid=(t // tm,),
        in_specs=[pl.BlockSpec((tm, d), lambda i: (i, 0)),
                  pl.BlockSpec((1, d), lambda i: (0, 0)),
                  pl.BlockSpec((1, d), lambda i: (0, 0))],
        out_specs=(pl.BlockSpec((tm, d), lambda i: (i, 0)), pl.BlockSpec((tm, d), lambda i: (i, 0))),
        compiler_params=_cparams("parallel"),
        name="entry_ln",
    )(x, g.reshape(1, d), b.reshape(1, d))


def _inproj_kernel(x_ref, w_ref, cos_ref, sin_ref, h_ref, dt_ref):
    j = pl.program_id(1)
    tm = x_ref.shape[0]
    x = x_ref[...]
    last = INPROJ_TILES - 1

    def rope_store(c, acc, scale):
        cs = cos_ref[...] * scale
        sn = sin_ref[...] * scale
        lane = lax.broadcasted_iota(I32, (tm, LANES), 1)
        first_half = (lane % DIFF_QK_DIM) < (DIFF_QK_DIM // 2)
        for hh in range(TN // LANES):
            a = acc[:, hh * LANES:(hh + 1) * LANES]
            rot = jnp.where(first_half, pltpu.roll(a, LANES - 32, 1), pltpu.roll(a, 32, 1))
            h_ref[:, c * TN + hh * LANES:c * TN + (hh + 1) * LANES] = (a * cs + rot * sn).astype(BF16)

    for c in range(INPROJ_TN // TN):
        acc = jnp.dot(x, w_ref[:, c * TN:(c + 1) * TN], preferred_element_type=F32)
        if c * TN == (COL_Q * TN) % INPROJ_TN or c * TN == (COL_K * TN) % INPROJ_TN:
            scale = LOG2E * DIFF_QK_DIM ** -0.5 if c * TN == (COL_Q * TN) % INPROJ_TN else 1.0

            @pl.when(j == last)
            def _():
                rope_store(c, acc, scale)

            @pl.when(j != last)
            def _():
                h_ref[:, c * TN:(c + 1) * TN] = acc.astype(BF16)
        else:
            h_ref[:, c * TN:(c + 1) * TN] = acc.astype(BF16)
        if c * TN == (COL_DT * TN) % INPROJ_TN:
            @pl.when(j == last)
            def _():
                dt_ref[...] = acc[:, 0:LANES]


def _in_projection(xb, w_tiles, cos_t, sin_t, seq):
    t, d = xb.shape
    tm = min(1024, seq)
    nseq = seq // tm
    return pl.pallas_call(
        _inproj_kernel,
        out_shape=(jax.ShapeDtypeStruct((t, H_COLS), BF16),
                   jax.ShapeDtypeStruct((t, LANES), F32)),
        grid=(t // tm, INPROJ_TILES),
        in_specs=[pl.BlockSpec((tm, d), lambda i, j: (i, 0)),
                  pl.BlockSpec((None, d, INPROJ_TN), lambda i, j: (j, 0, 0)),
                  pl.BlockSpec((tm, LANES), lambda i, j: (i % nseq, 0)),
                  pl.BlockSpec((tm, LANES), lambda i, j: (i % nseq, 0))],
        out_specs=(pl.BlockSpec((tm, INPROJ_TN), lambda i, j: (i, j)),
                   pl.BlockSpec((tm, LANES), lambda i, j: (i, 0))),
        compiler_params=_cparams("arbitrary", "arbitrary"),
        name="in_proj",
    )(xb, w_tiles, cos_t, sin_t)


def _ssd_kernel(xbc_ref, z_ref, dt_ref, cw_ref, cb_ref, dtb_ref, alog_ref, dtbT_ref, alogT_ref,
                dsk_ref, ng_ref, e64_ref, e128_ref, tri_ref, triT_ref, y_ref, xpad_ref, hst_ref):
    c = pl.program_id(1)
    L = SSD_L

    @pl.when(c == 0)
    def _():
        xpad_ref[0:SUBLANES, :] = jnp.zeros((SUBLANES, SSD_XBC), F32)
        hst_ref[...] = jnp.zeros(hst_ref.shape, F32)

    cur = xbc_ref[...].astype(F32)
    xpad_ref[SUBLANES:SUBLANES + L, :] = cur
    conv = cb_ref[...]
    for k in range(SSD_CONV):
        conv = conv + cw_ref[k:k + 1, :] * xpad_ref[pl.ds(SUBLANES - (SSD_CONV - 1) + k, L), :]
    xpad_ref[0:SUBLANES, :] = cur[L - SUBLANES:L, :]
    xbc = _silu(conv)
    X = xbc[:, 0:GROUP_W]
    Bm = xbc[:, GROUP_W:GROUP_W + 2 * SSD_STATE]
    Cm = xbc[:, GROUP_W + 2 * SSD_STATE:SSD_XBC]

    dt_raw = dt_ref[...]
    dt = _softplus(dt_raw + dtb_ref[...])
    acs = _dot_exact_lhs(tri_ref[...], dt * (-jnp.exp(alog_ref[...])))
    dtT = _softplus(dt_raw.T[0:16, :] + dtbT_ref[...])
    acsT = _dot_exact_rhs(dtT * (-jnp.exp(alogT_ref[...])), triT_ref[...])
    dt64 = _dot_exact_rhs(dt, e64_ref[...])
    acs64 = _dot_exact_rhs(acs, e64_ref[...])
    acs128 = _dot_exact_rhs(acs, e128_ref[...])

    last = acs64[L - 1:L, :]
    in_decay = jnp.exp(acs64)
    decay = jnp.exp(last - acs64)
    chunk_decay = jnp.exp(last)
    Xdt = X * dt64
    Xd = (Xdt * decay).astype(BF16)
    Xdt_b = Xdt.astype(BF16)

    row = lax.broadcasted_iota(I32, (L, L), 0)
    col = lax.broadcasted_iota(I32, (L, L), 1)
    causal = row >= col
    lane = lax.broadcasted_iota(I32, (L, LANES), 1)
    low_half = lane < 64

    y_parts = []
    for g in range(2):
        Bg = Bm[:, g * SSD_STATE:(g + 1) * SSD_STATE]
        Cg = Cm[:, g * SSD_STATE:(g + 1) * SSD_STATE].astype(BF16)
        cb = _dot_nt(Cg, Bg.astype(BF16))
        hprev = hst_ref[g]
        y_off = jnp.dot(Cg, hprev.astype(BF16), preferred_element_type=F32) * in_decay[:, g * 256:(g + 1) * 256]
        for pr in range(2):
            xp = Xdt_b[:, (2 * g + pr) * LANES:(2 * g + pr + 1) * LANES]
            ys = []
            for hh in range(2):
                h = 4 * g + 2 * pr + hh
                seg = acs128[:, h * LANES:(h + 1) * LANES] - acsT[h:h + 1, :]
                lmat = jnp.exp(jnp.where(causal, seg, -jnp.inf))
                w = (cb * lmat).astype(BF16)
                ys.append(jnp.dot(w, xp, preferred_element_type=F32))
            y_parts.append(jnp.where(low_half, ys[0], ys[1]) + y_off[:, pr * LANES:(pr + 1) * LANES])
        states = jnp.dot(Bg.T.astype(BF16), Xd[:, g * 256:(g + 1) * 256], preferred_element_type=F32)
        hst_ref[g] = hprev * chunk_decay[:, g * 256:(g + 1) * 256] + states

    z = z_ref[...].astype(F32)
    gate = _silu(z)
    for g in range(2):
        yg = jnp.concatenate(y_parts[2 * g:2 * g + 2], axis=1) + dsk_ref[:, g * 256:(g + 1) * 256] * X[:, g * 256:(g + 1) * 256]
        yg = yg * gate[:, g * 256:(g + 1) * 256]
        ms = jnp.mean(yg * yg, axis=-1, keepdims=True)
        y_ref[:, g * 256:(g + 1) * 256] = (yg * lax.rsqrt(ms + LN_EPS) * ng_ref[:, g * 256:(g + 1) * 256]).astype(BF16)


def _ssd_constants():
    e64 = np.zeros((LANES, 512), np.float32)
    e128 = np.zeros((LANES, 1024), np.float32)
    for h in range(SSD_HEADS):
        e64[h, 64 * h:64 * (h + 1)] = 1.0
        e128[h, 128 * h:128 * (h + 1)] = 1.0
    tri = np.tril(np.ones((SSD_L, SSD_L), np.float32))
    return (jnp.asarray(e64, BF16), jnp.asarray(e128, BF16), jnp.asarray(tri, BF16), jnp.asarray(tri.T, BF16))


def _ssd_mixer(h, dt_raw, conv_w, conv_b, dt_bias, a_log, d_skip, norm_g, batch, seq):
    t = h.shape[0]
    L = SSD_L
    nc = seq // L
    e64, e128, tri, triT = _ssd_constants()
    pad_l = lambda v: jnp.pad(v.reshape(1, SSD_HEADS), ((0, 0), (0, LANES - SSD_HEADS)))
    pad_t = lambda v: jnp.pad(jnp.broadcast_to(v.reshape(SSD_HEADS, 1), (SSD_HEADS, L)), ((0, 16 - SSD_HEADS), (0, 0)))
    rb = lambda b, c: b * nc + c
    const = lambda shape: pl.BlockSpec(shape, lambda b, c: (0,) * len(shape))
    return pl.pallas_call(
        _ssd_kernel,
        out_shape=jax.ShapeDtypeStruct((t, GROUP_W), BF16),
        grid=(batch, nc),
        in_specs=[pl.BlockSpec((L, SSD_XBC), lambda b, c: (rb(b, c), COL_XBC // 2)),
                  pl.BlockSpec((L, GROUP_W), lambda b, c: (rb(b, c), COL_Z)),
                  pl.BlockSpec((L, LANES), lambda b, c: (rb(b, c), 0)),
                  const((SSD_CONV, SSD_XBC)), const((1, SSD_XBC)),
                  const((1, LANES)), const((1, LANES)), const((16, L)), const((16, L)),
                  const((1, GROUP_W)), const((1, GROUP_W)),
                  const((LANES, 512)), const((LANES, 1024)), const((L, L)), const((L, L))],
        out_specs=pl.BlockSpec((L, GROUP_W), lambda b, c: (rb(b, c), 0)),
        scratch_shapes=[pltpu.VMEM((SUBLANES + L, SSD_XBC), F32),
                        pltpu.VMEM((2, SSD_STATE, 256), F32)],
        compiler_params=_cparams("arbitrary", "arbitrary"),
        name="ssd_mixer",
    )(h, h, dt_raw, conv_w, conv_b.reshape(1, SSD_XBC),
      pad_l(dt_bias), pad_l(a_log), pad_t(dt_bias), pad_t(a_log),
      jnp.repeat(d_skip, 64).reshape(1, GROUP_W), norm_g.reshape(1, GROUP_W), e64, e128, tri, triT)


def _attn_kernel(q_ref, k_ref, v_ref, bias_ref, lq1_ref, lk1_ref, lq2_ref, lk2_ref, g_ref, o_ref,
                 m_ref, l_ref, acc_ref, *, lambda_init):
    qi = pl.program_id(2)
    tq = q_ref.shape[0]
    tk = tq
    lane = lax.broadcasted_iota(I32, (tq, LANES), 1)
    qs = []
    for hh in range(ATTN_HEADS):
        q = q_ref[:, hh * LANES:(hh + 1) * LANES]
        zero = jnp.zeros_like(q)
        qs += [jnp.where(lane < DIFF_QK_DIM, q, zero), jnp.where(lane >= DIFF_QK_DIM, q, zero)]
    m_ref[...] = jnp.full(m_ref.shape, -jnp.inf, F32)
    l_ref[...] = jnp.zeros(l_ref.shape, F32)
    acc_ref[...] = jnp.zeros(acc_ref.shape, F32)

    def tile(off, bias):
        for st in range(2 * ATTN_HEADS):
            hh = st // 2
            k = k_ref[pl.ds(off, tk), hh * LANES:(hh + 1) * LANES]
            v = v_ref[pl.ds(off, tk), hh * LANES:(hh + 1) * LANES]
            s = _dot_nt(qs[st], k)
            if bias is not None:
                s = s + bias
            m_prev = m_ref[st]
            m_new = jnp.maximum(m_prev, jnp.max(s, axis=-1, keepdims=True))
            alpha = jnp.exp2(m_prev - m_new)
            p = jnp.exp2(s - jnp.concatenate([m_new] * (tk // LANES), axis=1))
            psum = p[:, 0:LANES]
            for c in range(1, tk // LANES):
                psum = psum + p[:, c * LANES:(c + 1) * LANES]
            l_ref[st] = alpha * l_ref[st] + psum
            acc_ref[st] = alpha * acc_ref[st] + jnp.dot(p.astype(BF16), v, preferred_element_type=F32)
            m_ref[st] = m_new

    def body(j, carry):
        tile(pl.multiple_of(j * tk, tk), None)
        return carry

    lax.fori_loop(0, qi, body, 0)
    tile(pl.multiple_of(qi * tk, tk), bias_ref[...])

    lam = (jnp.exp(jnp.sum(lq1_ref[...] * lk1_ref[...], axis=-1, keepdims=True))
           - jnp.exp(jnp.sum(lq2_ref[...] * lk2_ref[...], axis=-1, keepdims=True)) + lambda_init)
    for hh in range(ATTN_HEADS):
        l0 = jnp.sum(l_ref[2 * hh], axis=-1, keepdims=True)
        l1 = jnp.sum(l_ref[2 * hh + 1], axis=-1, keepdims=True)
        o = acc_ref[2 * hh] / l0 - lam * (acc_ref[2 * hh + 1] / l1)
        ms = jnp.mean(o * o, axis=-1, keepdims=True)
        o_ref[:, hh * LANES:(hh + 1) * LANES] = (o * lax.rsqrt(ms + LN_EPS) * g_ref[...]
                                                 * (1.0 - lambda_init)).astype(BF16)


def _chunk_mask_bias(tq):
    r = np.arange(tq)[:, None] // CHUNK
    c = np.arange(tq)[None, :] // CHUNK
    return jnp.asarray(np.where(c <= r, 0.0, -np.inf), F32)


def _diff_attention(h, lq1, lk1, lq2, lk2, norm_g, lambda_init, batch, seq):
    t = h.shape[0]
    tq = min(512, seq)
    nq = seq // tq
    w = ATTN_HEADS * LANES
    qcol = COL_Q * TN // w
    kcol = COL_K * TN // w
    vcol = COL_V * TN // w
    vec = lambda n: pl.BlockSpec((1, n), lambda b, hd, qi: (0, 0))
    return pl.pallas_call(
        functools.partial(_attn_kernel, lambda_init=lambda_init),
        out_shape=jax.ShapeDtypeStruct((t, GROUP_W), BF16),
        grid=(batch, DIFF_HEADS // ATTN_HEADS, nq),
        in_specs=[pl.BlockSpec((tq, w), lambda b, hd, qi: (b * nq + qi, qcol + hd)),
                  pl.BlockSpec((seq, w), lambda b, hd, qi: (b, kcol + hd)),
                  pl.BlockSpec((seq, w), lambda b, hd, qi: (b, vcol + hd)),
                  pl.BlockSpec((tq, tq), lambda b, hd, qi: (0, 0)),
                  vec(DIFF_QK_DIM), vec(DIFF_QK_DIM), vec(DIFF_QK_DIM), vec(DIFF_QK_DIM), vec(DIFF_V_DIM)],
        out_specs=pl.BlockSpec((tq, w), lambda b, hd, qi: (b * nq + qi, hd)),
        scratch_shapes=[pltpu.VMEM((2 * ATTN_HEADS, tq, LANES), F32), pltpu.VMEM((2 * ATTN_HEADS, tq, LANES), F32),
                        pltpu.VMEM((2 * ATTN_HEADS, tq, DIFF_V_DIM), F32)],
        compiler_params=_cparams("arbitrary", "arbitrary", "arbitrary"),
        name="diff_attn",
    )(h, h, h, _chunk_mask_bias(tq), lq1.reshape(1, -1), lk1.reshape(1, -1), lq2.reshape(1, -1),
      lk2.reshape(1, -1), norm_g.reshape(1, -1))


CONF_HALO = 32
SC_HALO = 16
CONV_ROWS = 64


def _conv_kernel(u_ref, uh_ref, bg_ref, cg_ref, hh_ref, cgh_ref, hhh_ref,
                 dww_ref, dwb_ref, lng_ref, lnb_ref, pww_ref, pwb_ref, scw_ref,
                 yc_ref, yd_ref, hbuf_ref, pbuf_ref, cbuf_ref, sbuf_ref):
    i = pl.program_id(1)
    tm = u_ref.shape[0]

    def glu(u):
        u = u.astype(F32)
        return u[:, 0:GROUP_W] * _sigmoid(u[:, GROUP_W:2 * GROUP_W])

    first = (i == 0)
    hbuf_ref[0:CONF_HALO, :] = jnp.where(first, 0.0, glu(uh_ref[...]))
    hbuf_ref[CONF_HALO:CONF_HALO + tm, :] = glu(u_ref[...])
    pbuf_ref[0:SC_HALO, :] = jnp.where(first, 0.0, cgh_ref[...].astype(F32) * hhh_ref[...].astype(F32))
    pbuf_ref[SC_HALO:SC_HALO + tm, :] = cg_ref[...].astype(F32) * hh_ref[...].astype(F32)

    for b in range(1, SUBLANES):
        sbuf_ref[b - 1, 0:tm + CONF_HALO - SUBLANES, :] = hbuf_ref[pl.ds(b, tm + CONF_HALO - SUBLANES), :]
    for r0 in range(0, tm, CONV_ROWS):
        acc = jnp.zeros((CONV_ROWS, GROUP_W), F32) + dwb_ref[...]
        for k in range(CONF_KERNEL):
            off = CONF_HALO - (CONF_KERNEL - 1) + k
            phase, base = off % SUBLANES, r0 + off - off % SUBLANES
            src = hbuf_ref[pl.ds(base, CONV_ROWS), :] if phase == 0 else sbuf_ref[phase - 1, pl.ds(base, CONV_ROWS), :]
            acc = acc + dww_ref[k:k + 1, :] * src
        cbuf_ref[r0:r0 + CONV_ROWS, :] = acc
    hc = cbuf_ref[...]
    mu = jnp.mean(hc, axis=-1, keepdims=True)
    xc = hc - mu
    var = jnp.mean(xc * xc, axis=-1, keepdims=True)
    hn = _silu(xc * lax.rsqrt(var + LN_EPS) * lng_ref[...] + lnb_ref[...])
    yc = jnp.dot(hn.astype(BF16), pww_ref[...], preferred_element_type=F32) + pwb_ref[...]
    yc_ref[...] = yc.astype(BF16)

    sc = jnp.zeros((tm, GROUP_W), F32)
    for k in range(SC_KERNEL):
        sc = sc + scw_ref[k:k + 1, :] * pbuf_ref[pl.ds(SC_HALO - (SC_KERNEL - 1) + k, tm), :]
    yd_ref[...] = (bg_ref[...].astype(F32) * sc).astype(BF16)


def _conv_mixers(h, dw_w, dw_b, ln_g, ln_b, pw_w, pw_b, sc_w, batch, seq):
    t = h.shape[0]
    tm = min(512, seq)
    nt = seq // tm
    rb = lambda b, i: b * nt + i
    halo = lambda rows: (lambda b, i: jnp.maximum(rb(b, i) * (tm // rows) - 1, 0))
    hc, hs = halo(CONF_HALO), halo(SC_HALO)
    const = lambda shape: pl.BlockSpec(shape, lambda b, i: (0,) * len(shape))
    return pl.pallas_call(
        _conv_kernel,
        out_shape=(jax.ShapeDtypeStruct((t, GROUP_W), BF16), jax.ShapeDtypeStruct((t, GROUP_W), BF16)),
        grid=(batch, nt),
        in_specs=[pl.BlockSpec((tm, 2 * GROUP_W), lambda b, i: (rb(b, i), COL_CONF // 2)),
                  pl.BlockSpec((CONF_HALO, 2 * GROUP_W), lambda b, i: (hc(b, i), COL_CONF // 2)),
                  pl.BlockSpec((tm, GROUP_W), lambda b, i: (rb(b, i), COL_BG)),
                  pl.BlockSpec((tm, GROUP_W), lambda b, i: (rb(b, i), COL_CG)),
                  pl.BlockSpec((tm, GROUP_W), lambda b, i: (rb(b, i), COL_HH)),
                  pl.BlockSpec((SC_HALO, GROUP_W), lambda b, i: (hs(b, i), COL_CG)),
                  pl.BlockSpec((SC_HALO, GROUP_W), lambda b, i: (hs(b, i), COL_HH)),
                  const((CONF_KERNEL, GROUP_W)), const((1, GROUP_W)), const((1, GROUP_W)), const((1, GROUP_W)),
                  const((GROUP_W, GROUP_W)), const((1, GROUP_W)), const((SC_KERNEL, GROUP_W))],
        out_specs=(pl.BlockSpec((tm, GROUP_W), lambda b, i: (rb(b, i), 0)),
                   pl.BlockSpec((tm, GROUP_W), lambda b, i: (rb(b, i), 0))),
        scratch_shapes=[pltpu.VMEM((CONF_HALO + tm, GROUP_W), F32),
                        pltpu.VMEM((SC_HALO + tm, GROUP_W), F32),
                        pltpu.VMEM((tm, GROUP_W), F32),
                        pltpu.VMEM((SUBLANES - 1, CONF_HALO + tm, GROUP_W), F32)],
        compiler_params=_cparams("arbitrary", "arbitrary"),
        name="conv_mixers",
    )(h, h, h, h, h, h, h, dw_w, dw_b.reshape(1, -1), ln_g.reshape(1, -1), ln_b.reshape(1, -1),
      pw_w.astype(BF16), pw_b.reshape(1, -1), sc_w)


def _first_max4(vals):
    m1 = jnp.maximum(jnp.maximum(vals[0], vals[1]), jnp.maximum(vals[2], vals[3]))
    i1 = jnp.where(vals[0] == m1, 0, jnp.where(vals[1] == m1, 1, jnp.where(vals[2] == m1, 2, 3)))
    rest = [jnp.where(i1 == j, -jnp.inf, vals[j]) for j in range(4)]
    m2 = jnp.maximum(jnp.maximum(rest[0], rest[1]), jnp.maximum(rest[2], rest[3]))
    i2 = jnp.where(rest[0] == m2, 0, jnp.where(rest[1] == m2, 1, jnp.where(rest[2] == m2, 2, 3)))
    return m1, i1, m2, i2


def _outproj_kernel(ya_ref, yb_ref, yc_ref, yd_ref, x_ref, w_ref, g_ref, b_ref, x1_ref):
    mix = jnp.dot(ya_ref[...], w_ref[0:GROUP_W, :], preferred_element_type=F32)
    mix = mix + jnp.dot(yb_ref[...], w_ref[GROUP_W:2 * GROUP_W, :], preferred_element_type=F32)
    mix = mix + jnp.dot(yc_ref[...], w_ref[2 * GROUP_W:3 * GROUP_W, :], preferred_element_type=F32)
    mix = mix + jnp.dot(yd_ref[...], w_ref[3 * GROUP_W:4 * GROUP_W, :], preferred_element_type=F32)
    y = ALPHA * x_ref[...] + mix
    mu = jnp.mean(y, axis=-1, keepdims=True)
    yc = y - mu
    var = jnp.mean(yc * yc, axis=-1, keepdims=True)
    x1_ref[...] = yc * lax.rsqrt(var + LN_EPS) * g_ref[...] + b_ref[...]


def _router_kernel(x_ref, w2_ref, w1_ref, rb_ref, eid_ref, gate_ref):
    tm = x_ref.shape[0]
    x = x_ref[...]
    xh = x.astype(BF16)
    xm = (x - xh.astype(F32)).astype(BF16)
    r = (jnp.dot(xh, w2_ref[...], preferred_element_type=F32)
         + jnp.dot(xm, w1_ref[...], preferred_element_type=F32))
    r = r + pltpu.roll(r, LANES - N_EXPERTS, 1)
    logits = r.T[0:N_EXPERTS, :]
    aff = _sigmoid(logits)
    sel = aff + jnp.concatenate([rb_ref[...]] * (tm // LANES), axis=1)
    rows = [sel[e:e + 1, :] for e in range(N_EXPERTS)]
    arow = [aff[e:e + 1, :] for e in range(N_EXPERTS)]
    tops = [_first_max4(rows[4 * g:4 * g + 4]) for g in range(N_GROUPS)]
    score = [tp[0] + tp[2] for tp in tops]
    best = jnp.maximum(jnp.maximum(score[0], score[1]), jnp.maximum(score[2], score[3]))
    grp = jnp.where(score[0] == best, 0, jnp.where(score[1] == best, 1, jnp.where(score[2] == best, 2, 3)))
    pick = lambda idx: jnp.where(grp == 0, tops[0][idx], jnp.where(grp == 1, tops[1][idx],
                                 jnp.where(grp == 2, tops[2][idx], tops[3][idx])))
    e1 = grp * EXPERTS_PER_GROUP + pick(1)
    e2 = grp * EXPERTS_PER_GROUP + pick(3)
    a1 = jnp.zeros_like(best)
    a2 = jnp.zeros_like(best)
    for e in range(N_EXPERTS):
        a1 = jnp.where(e1 == e, arow[e], a1)
        a2 = jnp.where(e2 == e, arow[e], a2)
    den = a1 + a2
    zi = jnp.zeros((SUBLANES - 2, tm), I32)
    zf = jnp.zeros((SUBLANES - 2, tm), F32)
    eid_ref[...] = jnp.concatenate([e1.astype(I32), e2.astype(I32), zi], axis=0)
    gate_ref[...] = jnp.concatenate([a1 / den, a2 / den, zf], axis=0)


def _out_projection(ya, yb, yc, yd, x, w_out, ln_g, ln_b, layer):
    t, d = x.shape
    tm = min(512, t)
    act = lambda: pl.BlockSpec((tm, GROUP_W), lambda i: (i, 0))
    const = lambda shape: pl.BlockSpec(shape, lambda i: (0,) * len(shape))
    return pl.pallas_call(
        _outproj_kernel,
        out_shape=jax.ShapeDtypeStruct((t, d), F32),
        grid=(t // tm,),
        in_specs=[act(), act(), act(), act(),
                  pl.BlockSpec((tm, d), lambda i: (i, 0)),
                  pl.BlockSpec((None, d, d), lambda i: (layer, 0, 0)), const((1, d)), const((1, d))],
        out_specs=pl.BlockSpec((tm, d), lambda i: (i, 0)),
        compiler_params=_cparams("parallel"),
        name="out_proj",
    )(ya, yb, yc, yd, x, w_out, ln_g.reshape(1, d), ln_b.reshape(1, d))


def _router(x1, router_w, router_bias):
    t, d = x1.shape
    tm = min(512, t)
    w_hi = router_w.astype(BF16)
    w_mid = (router_w - w_hi.astype(F32)).astype(BF16)
    w2 = jnp.concatenate([w_hi, w_mid, jnp.zeros((d, LANES - 2 * N_EXPERTS), BF16)], axis=1)
    w1 = jnp.concatenate([w_hi, jnp.zeros((d, LANES - N_EXPERTS), BF16)], axis=1)
    router_b = jnp.broadcast_to(router_bias.reshape(N_EXPERTS, 1), (N_EXPERTS, LANES))
    const = lambda shape: pl.BlockSpec(shape, lambda i: (0,) * len(shape))
    return pl.pallas_call(
        _router_kernel,
        out_shape=(jax.ShapeDtypeStruct((SUBLANES, t), I32), jax.ShapeDtypeStruct((SUBLANES, t), F32)),
        grid=(t // tm,),
        in_specs=[pl.BlockSpec((tm, d), lambda i: (i, 0)),
                  const((d, LANES)), const((d, LANES)), const((N_EXPERTS, LANES))],
        out_specs=(pl.BlockSpec((SUBLANES, tm), lambda i: (0, i)),
                   pl.BlockSpec((SUBLANES, tm), lambda i: (0, i))),
        compiler_params=_cparams("parallel"),
        name="router",
    )(x1, w2, w1, router_b)


def _row_copy(src_hbm, tok_ref, dst_ref, sem, r, base=0):
    tok = tok_ref[0, 0, base + r]
    return pltpu.make_async_copy(src_hbm.at[pl.ds(tok, 1), :], dst_ref.at[pl.ds(r, 1), :], sem)


def _moe_kernel(blk_exp_ref, nused_ref, tok_ref, tokn_ref, x_hbm, wg_ref, wu_ref, wd_ref, y_ref,
                xbuf0_ref, xbuf1_ref, sem_ref):
    i = pl.program_id(0)
    nused = nused_ref[0]
    bufs = (xbuf0_ref, xbuf1_ref)

    def issue(tref, s):
        for r in range(MOE_BLOCK):
            _row_copy(x_hbm, tref, bufs[s], sem_ref.at[s], r).start(priority=r % 2)

    def wait(tref, s):
        for r in range(MOE_BLOCK):
            _row_copy(x_hbm, tref, bufs[s], sem_ref.at[s], r).wait()

    @pl.when(i == 0)
    def _():
        issue(tok_ref, 0)

    for s in range(2):
        @pl.when(jnp.logical_and(i < nused, i % 2 == s))
        def _():
            wait(tok_ref, s)
            issue(tokn_ref, 1 - s)
            x = bufs[s][...].astype(BF16)
            hg = jnp.dot(x, wg_ref[...], preferred_element_type=F32)
            hu = jnp.dot(x, wu_ref[...], preferred_element_type=F32)
            hid = (_silu(hg) * hu).astype(BF16)
            y_ref[...] = jnp.dot(hid, wd_ref[...], preferred_element_type=F32)

        @pl.when(jnp.logical_and(i + 1 == nused, i % 2 == s))
        def _():
            wait(tokn_ref, 1 - s)

    @pl.when(i >= nused)
    def _():
        y_ref[...] = jnp.zeros(y_ref.shape, F32)


def _moe_experts(x1, slot_tok, blk_exp, nused, wg, wu, wd, layer):
    t, d = x1.shape
    n_blk = slot_tok.shape[0]
    tok_spec = lambda off: pl.BlockSpec((1, 1, MOE_BLOCK),
                                        lambda i, be, nu: (jnp.minimum(i + off, nu[0] - 1), 0, 0),
                                        memory_space=pltpu.SMEM)
    return pl.pallas_call(
        _moe_kernel,
        out_shape=jax.ShapeDtypeStruct((n_blk * MOE_BLOCK, d), F32),
        grid_spec=pltpu.PrefetchScalarGridSpec(
            num_scalar_prefetch=2,
            grid=(n_blk,),
            in_specs=[tok_spec(0), tok_spec(1),
                      pl.BlockSpec(memory_space=pl.ANY),
                      pl.BlockSpec((None, None, d, D_EXPERT), lambda i, be, nu: (layer, be[i], 0, 0)),
                      pl.BlockSpec((None, None, d, D_EXPERT), lambda i, be, nu: (layer, be[i], 0, 0)),
                      pl.BlockSpec((None, None, D_EXPERT, d), lambda i, be, nu: (layer, be[i], 0, 0))],
            out_specs=pl.BlockSpec((MOE_BLOCK, d), lambda i, be, nu: (i, 0)),
            scratch_shapes=[pltpu.VMEM((MOE_BLOCK, d), F32), pltpu.VMEM((MOE_BLOCK, d), F32),
                            pltpu.SemaphoreType.DMA((2,))]),
        compiler_params=_cparams("arbitrary"),
        name="moe_experts",
    )(blk_exp, nused, slot_tok, slot_tok, x1, wg, wu, wd)


COMBINE_ROWS = 256


def _combine_kernel(pos_ref, posn_ref, y_hbm, x_ref, gt_ref, g_ref, b_ref, o_ref, ob_ref, ybuf_ref, sem_ref):
    i = pl.program_id(0)
    n = pl.num_programs(0)
    slot = i % 2
    tm = x_ref.shape[0]

    def issue(pref, s):
        def body(r8, carry):
            for u in range(SUBLANES):
                for k in range(TOP_K):
                    _row_copy(y_hbm, pref, ybuf_ref.at[s, k], sem_ref.at[s], r8 * SUBLANES + u,
                              base=k * tm).start(priority=k)
            return carry
        lax.fori_loop(0, tm // SUBLANES, body, 0)

    def wait(pref, s):
        def body(r8, carry):
            for u in range(SUBLANES):
                for k in range(TOP_K):
                    _row_copy(y_hbm, pref, ybuf_ref.at[s, k], sem_ref.at[s], r8 * SUBLANES + u, base=k * tm).wait()
            return carry
        lax.fori_loop(0, tm // SUBLANES, body, 0)

    @pl.when(i == 0)
    def _():
        issue(pos_ref, 0)

    @pl.when(i + 1 < n)
    def _():
        issue(posn_ref, 1 - slot)

    wait(pos_ref, slot)
    gt = gt_ref[...]

    moe = ybuf_ref[slot, 0] * gt[:, 0:1] + ybuf_ref[slot, 1] * gt[:, 1:2]
    y = ALPHA * x_ref[...] + moe
    mu = jnp.mean(y, axis=-1, keepdims=True)
    yc = y - mu
    var = jnp.mean(yc * yc, axis=-1, keepdims=True)
    out = yc * lax.rsqrt(var + LN_EPS) * g_ref[...] + b_ref[...]
    o_ref[...] = out
    ob_ref[...] = out.astype(BF16)


def _moe_combine(y_pad, pos_tiles, x1, gate_tok, ln_g, ln_b):
    t, d = x1.shape
    tm = min(COMBINE_ROWS, t)
    n = t // tm
    pos_spec = lambda off: pl.BlockSpec((1, 1, TOP_K * tm), lambda i: (jnp.minimum(i + off, n - 1), 0, 0),
                                        memory_space=pltpu.SMEM)
    return pl.pallas_call(
        _combine_kernel,
        out_shape=(jax.ShapeDtypeStruct((t, d), F32), jax.ShapeDtypeStruct((t, d), BF16)),
        grid=(n,),
        in_specs=[pos_spec(0), pos_spec(1),
                  pl.BlockSpec(memory_space=pl.ANY),
                  pl.BlockSpec((tm, d), lambda i: (i, 0)),
                  pl.BlockSpec((tm, TOP_K), lambda i: (i, 0)),
                  pl.BlockSpec((1, d), lambda i: (0, 0)),
                  pl.BlockSpec((1, d), lambda i: (0, 0))],
        out_specs=(pl.BlockSpec((tm, d), lambda i: (i, 0)), pl.BlockSpec((tm, d), lambda i: (i, 0))),
        scratch_shapes=[pltpu.VMEM((2, TOP_K, tm, d), F32), pltpu.SemaphoreType.DMA((2,))],
        compiler_params=_cparams("arbitrary"),
        name="moe_combine",
    )(pos_tiles, pos_tiles, y_pad, x1, gate_tok, ln_g.reshape(1, d), ln_b.reshape(1, d))


def _routing_tables(eid, t):
    n_slots = t * TOP_K
    e_flat = eid.T.reshape(-1)
    onehot = (e_flat[:, None] == jnp.arange(N_EXPERTS, dtype=I32)[None, :]).astype(I32)
    csum = jnp.cumsum(onehot, axis=0)
    counts = csum[-1]
    rank = jnp.take_along_axis(csum, e_flat[:, None], axis=1)[:, 0] - 1
    padded = ((counts + MOE_BLOCK - 1) // MOE_BLOCK) * MOE_BLOCK
    pends = jnp.cumsum(padded)
    pstarts = pends - padded
    pos = (pstarts[e_flat] + rank).astype(I32)
    n_blk = -(-n_slots // MOE_BLOCK) + N_EXPERTS
    slot_tok = jnp.zeros((n_blk * MOE_BLOCK,), I32).at[pos].set(jnp.arange(n_slots, dtype=I32) // TOP_K)
    blk_start = jnp.arange(n_blk, dtype=I32) * MOE_BLOCK
    blk_exp = jnp.minimum(jnp.sum((pends[None, :] <= blk_start[:, None]).astype(I32), axis=1),
                          N_EXPERTS - 1).astype(I32)
    nused = (pends[-1] // MOE_BLOCK).astype(I32).reshape(1)
    return slot_tok.reshape(n_blk, 1, MOE_BLOCK), blk_exp, nused, pos


def _rope_tables(seq):
    half = DIFF_QK_DIM // 2
    inv = 1.0 / (ROPE_THETA ** (jnp.arange(0, DIFF_QK_DIM, 2, dtype=F32) / DIFF_QK_DIM))
    ang = jnp.arange(seq, dtype=F32)[:, None] * inv[None, :]
    cos, sin = jnp.cos(ang), jnp.sin(ang)
    cos_t = jnp.concatenate([cos, cos, cos, cos], axis=1)
    sin_t = jnp.concatenate([-sin, sin, -sin, sin], axis=1)
    return cos_t, sin_t


W_IN_SPLITS = (0, 512, 1536, 1544, 2056, 2568, 3080, 4104, 5640)


def _wprep_kernel(w_ref, o_ref):
    w = w_ref[...]
    z, xbc, dt, q, k, v, conf, sc = [w[:, a:b] for a, b in zip(W_IN_SPLITS[:-1], W_IN_SPLITS[1:])]
    rows = w.shape[0]
    o_ref[0] = jnp.concatenate([xbc, conf], axis=1).astype(BF16)
    o_ref[1] = jnp.concatenate([sc, z], axis=1).astype(BF16)
    o_ref[2] = jnp.concatenate([q, k, v, dt, jnp.zeros((rows, TN - SSD_HEADS), F32)], axis=1).astype(BF16)


def _prep_w_in(w_all, layer):
    _, d, n = w_all.shape
    rows = 256
    return pl.pallas_call(
        _wprep_kernel,
        out_shape=jax.ShapeDtypeStruct((INPROJ_TILES, d, INPROJ_TN), BF16),
        grid=(d // rows,),
        in_specs=[pl.BlockSpec((None, rows, n), lambda i: (layer, i, 0))],
        out_specs=pl.BlockSpec((INPROJ_TILES, rows, INPROJ_TN), lambda i: (0, i, 0)),
        compiler_params=_cparams("parallel"),
        name="w_in_prep",
    )(w_all)


def kernel(x, ln_in_g, ln_in_b, w_in, ssd_conv_w, ssd_conv_b, ssd_dt_bias, ssd_a_log, ssd_d, ssd_norm_g, diff_lq1, diff_lk1, diff_lq2, diff_lk2, diff_norm_g, conf_dw_w, conf_dw_b, conf_ln_g, conf_ln_b, conf_pw_w, conf_pw_b, sc_conv_w, w_out, ln1_g, ln1_b, router_w, router_bias, moe_w_gate, moe_w_up, moe_w_down, ln2_g, ln2_b):
    b, s, d = x.shape
    t = b * s
    cos_t, sin_t = _rope_tables(s)
    xf, xb = _layer_norm(x.reshape(t, d), ln_in_g, ln_in_b)
    w_out_b = w_out.astype(BF16)
    wg_b, wu_b, wd_b = moe_w_gate.astype(BF16), moe_w_up.astype(BF16), moe_w_down.astype(BF16)
    for l in range(DEPTH):
        lambda_init = 0.8 - 0.6 * math.exp(-0.3 * l)
        h, dt_raw = _in_projection(xb, _prep_w_in(w_in, l), cos_t, sin_t, s)
        ya = _ssd_mixer(h, dt_raw, ssd_conv_w[l], ssd_conv_b[l], ssd_dt_bias[l], ssd_a_log[l],
                        ssd_d[l], ssd_norm_g[l], b, s)
        yb = _diff_attention(h, diff_lq1[l], diff_lk1[l], diff_lq2[l], diff_lk2[l], diff_norm_g[l],
                             lambda_init, b, s)
        yc, yd = _conv_mixers(h, conf_dw_w[l], conf_dw_b[l], conf_ln_g[l], conf_ln_b[l], conf_pw_w[l],
                              conf_pw_b[l], sc_conv_w[l], b, s)
        x1 = _out_projection(ya, yb, yc, yd, xf, w_out_b, ln1_g[l], ln1_b[l], l)
        eid, gate = _router(x1, router_w, router_bias)
        slot_tok, blk_exp, nused, pos = _routing_tables(eid[0:TOP_K], t)
        y_pad = _moe_experts(x1, slot_tok, blk_exp, nused, wg_b, wu_b, wd_b, l)
        tmc = min(COMBINE_ROWS, t)
        pos_tiles = pos.reshape(t // tmc, tmc, TOP_K).transpose(0, 2, 1).reshape(t // tmc, 1, TOP_K * tmc)
        xf, xb = _moe_combine(y_pad, pos_tiles, x1, gate[0:TOP_K].T, ln2_g[l], ln2_b[l])
    return xf.reshape(b, s, d)
```

```python
import functools
import math

import numpy as np
import jax
import jax.numpy as jnp
from jax import lax
from jax.experimental import pallas as pl
from jax.experimental.pallas import tpu as pltpu

F32 = jnp.float32
BF16 = jnp.bfloat16
I32 = jnp.int32

D_MODEL = 2048
DEPTH = 2
CHUNK = 64
GROUP_W = 512
SSD_HEADS = 8
SSD_STATE = 128
SSD_CONV = 4
SSD_XBC = 1024
DIFF_HEADS = 4
DIFF_QK_DIM = 64
DIFF_V_DIM = 128
ROPE_THETA = 10000.0
CONF_KERNEL = 31
SC_KERNEL = 3
N_EXPERTS = 16
N_GROUPS = 4
EXPERTS_PER_GROUP = 4
TOP_K = 2
D_EXPERT = 1024
MOE_BLOCK = 256
ALPHA = (2 * DEPTH) ** 0.25
LN_EPS = 1e-5
LOG2E = 1.4426950408889634

LANES = 128
SUBLANES = 8
VMEM_LIMIT_BYTES = 56 * 1024 * 1024

COL_XBC, COL_CONF, COL_BG, COL_CG, COL_HH, COL_Z, COL_Q, COL_K, COL_V, COL_DT = 0, 2, 4, 5, 6, 7, 8, 9, 10, 11
H_COLS = 12 * 512
TN = 512
INPROJ_TN = 2048
INPROJ_TILES = H_COLS // INPROJ_TN
SSD_L = 128
ATTN_HEADS = 2


def _cparams(*sem):
    return pltpu.CompilerParams(dimension_semantics=tuple(sem), vmem_limit_bytes=VMEM_LIMIT_BYTES)


def _sigmoid(x):
    return 1.0 / (1.0 + jnp.exp(-x))


def _silu(x):
    return x * _sigmoid(x)


def _softplus(x):
    return jnp.maximum(x, 0.0) + jnp.log(1.0 + jnp.exp(-jnp.abs(x)))


def _split3(v):
    hi = v.astype(BF16)
    r = v - hi.astype(F32)
    mid = r.astype(BF16)
    lo = (r - mid.astype(F32)).astype(BF16)
    return hi, mid, lo


def _dot_exact_rhs(v, m):
    return sum(jnp.dot(p, m, preferred_element_type=F32) for p in _split3(v))


def _dot_exact_lhs(m, v):
    return sum(jnp.dot(m, p, preferred_element_type=F32) for p in _split3(v))


def _dot_nt(a, b):
    return lax.dot_general(a, b, (((1,), (1,)), ((), ())), preferred_element_type=F32)


def _ln_kernel(x_ref, g_ref, b_ref, o_ref):
    x = x_ref[...]
    mu = jnp.mean(x, axis=-1, keepdims=True)
    xc = x - mu
    var = jnp.mean(xc * xc, axis=-1, keepdims=True)
    o_ref[...] = xc * lax.rsqrt(var + LN_EPS) * g_ref[...] + b_ref[...]


def _layer_norm(x, g, b):
    t, d = x.shape
    tm = min(512, t)
    return pl.pallas_call(
        _ln_kernel,
        out_shape=jax.ShapeDtypeStruct((t, d), F32),
        grid=(t // tm,),
        in_specs=[pl.BlockSpec((tm, d), lambda i: (i, 0)),
                  pl.BlockSpec((1, d), lambda i: (0, 0)),
                  pl.BlockSpec((1, d), lambda i: (0, 0))],
        out_specs=pl.BlockSpec((tm, d), lambda i: (i, 0)),
        compiler_params=_cparams("parallel"),
        name="entry_ln",
    )(x, g.reshape(1, d), b.reshape(1, d))


def _inproj_kernel(x_ref, w_ref, cos_ref, sin_ref, h_ref, dt_ref, xb_ref):
    j = pl.program_id(1)
    tm = x_ref.shape[0]

    @pl.when(j == 0)
    def _():
        xb_ref[...] = x_ref[...].astype(BF16)

    x = xb_ref[...]
    last = INPROJ_TILES - 1

    def rope_store(c, acc, scale):
        cs = cos_ref[...] * scale
        sn = sin_ref[...] * scale
        lane = lax.broadcasted_iota(I32, (tm, LANES), 1)
        first_half = (lane % DIFF_QK_DIM) < (DIFF_QK_DIM // 2)
        for hh in range(TN // LANES):
            a = acc[:, hh * LANES:(hh + 1) * LANES]
            rot = jnp.where(first_half, pltpu.roll(a, LANES - 32, 1), pltpu.roll(a, 32, 1))
            h_ref[:, c * TN + hh * LANES:c * TN + (hh + 1) * LANES] = (a * cs + rot * sn).astype(BF16)

    for c in range(INPROJ_TN // TN):
        acc = jnp.dot(x, w_ref[:, c * TN:(c + 1) * TN], preferred_element_type=F32)
        if c * TN == (COL_Q * TN) % INPROJ_TN or c * TN == (COL_K * TN) % INPROJ_TN:
            scale = LOG2E * DIFF_QK_DIM ** -0.5 if c * TN == (COL_Q * TN) % INPROJ_TN else 1.0

            @pl.when(j == last)
            def _():
                rope_store(c, acc, scale)

            @pl.when(j != last)
            def _():
                h_ref[:, c * TN:(c + 1) * TN] = acc.astype(BF16)
        else:
            h_ref[:, c * TN:(c + 1) * TN] = acc.astype(BF16)
        if c * TN == (COL_DT * TN) % INPROJ_TN:
            @pl.when(j == last)
            def _():
                dt_ref[...] = acc[:, 0:LANES]


def _in_projection(x, w_tiles, cos_t, sin_t, seq, t):
    d = x.shape[1]
    tm = min(1024, seq)
    nseq = seq // tm
    return pl.pallas_call(
        _inproj_kernel,
        out_shape=(jax.ShapeDtypeStruct((t, H_COLS), BF16),
                   jax.ShapeDtypeStruct((t, LANES), F32)),
        grid=(t // tm, INPROJ_TILES),
        in_specs=[pl.BlockSpec((tm, d), lambda i, j: (i, 0)),
                  pl.BlockSpec((None, d, INPROJ_TN), lambda i, j: (j, 0, 0)),
                  pl.BlockSpec((tm, LANES), lambda i, j: (i % nseq, 0)),
                  pl.BlockSpec((tm, LANES), lambda i, j: (i % nseq, 0))],
        out_specs=(pl.BlockSpec((tm, INPROJ_TN), lambda i, j: (i, j)),
                   pl.BlockSpec((tm, LANES), lambda i, j: (i, 0))),
        scratch_shapes=[pltpu.VMEM((tm, d), BF16)],
        compiler_params=_cparams("arbitrary", "arbitrary"),
        name="in_proj",
    )(x, w_tiles, cos_t, sin_t)


def _ssd_kernel(xbc_ref, z_ref, dt_ref, cw_ref, cb_ref, dtb_ref, alog_ref, dtbT_ref, alogT_ref,
                dsk_ref, ng_ref, e64_ref, e128_ref, tri_ref, triT_ref, y_ref, xpad_ref, hst_ref):
    c = pl.program_id(1)
    L = SSD_L

    @pl.when(c == 0)
    def _():
        xpad_ref[0:SUBLANES, :] = jnp.zeros((SUBLANES, SSD_XBC), F32)
        hst_ref[...] = jnp.zeros(hst_ref.shape, F32)

    cur = xbc_ref[...].astype(F32)
    xpad_ref[SUBLANES:SUBLANES + L, :] = cur
    conv = cb_ref[...]
    for k in range(SSD_CONV):
        conv = conv + cw_ref[k:k + 1, :] * xpad_ref[pl.ds(SUBLANES - (SSD_CONV - 1) + k, L), :]
    xpad_ref[0:SUBLANES, :] = cur[L - SUBLANES:L, :]
    xbc = _silu(conv)
    X = xbc[:, 0:GROUP_W]
    Bm = xbc[:, GROUP_W:GROUP_W + 2 * SSD_STATE]
    Cm = xbc[:, GROUP_W + 2 * SSD_STATE:SSD_XBC]

    dt_raw = dt_ref[...]
    dt = _softplus(dt_raw + dtb_ref[...])
    acs = _dot_exact_lhs(tri_ref[...], dt * (-jnp.exp(alog_ref[...])))
    dtT = _softplus(dt_raw.T[0:16, :] + dtbT_ref[...])
    acsT = _dot_exact_rhs(dtT * (-jnp.exp(alogT_ref[...])), triT_ref[...])
    dt64 = _dot_exact_rhs(dt, e64_ref[...])
    acs64 = _dot_exact_rhs(acs, e64_ref[...])
    acs128 = _dot_exact_rhs(acs, e128_ref[...])

    last = acs64[L - 1:L, :]
    in_decay = jnp.exp(acs64)
    decay = jnp.exp(last - acs64)
    chunk_decay = jnp.exp(last)
    Xdt = X * dt64
    Xd = (Xdt * decay).astype(BF16)
    Xdt_b = Xdt.astype(BF16)

    row = lax.broadcasted_iota(I32, (L, L), 0)
    col = lax.broadcasted_iota(I32, (L, L), 1)
    causal = row >= col
    lane = lax.broadcasted_iota(I32, (L, LANES), 1)
    low_half = lane < 64

    y_parts = []
    for g in range(2):
        Bg = Bm[:, g * SSD_STATE:(g + 1) * SSD_STATE]
        Cg = Cm[:, g * SSD_STATE:(g + 1) * SSD_STATE].astype(BF16)
        cb = _dot_nt(Cg, Bg.astype(BF16))
        hprev = hst_ref[g]
        y_off = jnp.dot(Cg, hprev.astype(BF16), preferred_element_type=F32) * in_decay[:, g * 256:(g + 1) * 256]
        for pr in range(2):
            xp = Xdt_b[:, (2 * g + pr) * LANES:(2 * g + pr + 1) * LANES]
            ys = []
            for hh in range(2):
                h = 4 * g + 2 * pr + hh
                seg = acs128[:, h * LANES:(h + 1) * LANES] - acsT[h:h + 1, :]
                lmat = jnp.exp(jnp.where(causal, seg, -jnp.inf))
                w = (cb * lmat).astype(BF16)
                ys.append(jnp.dot(w, xp, preferred_element_type=F32))
            y_parts.append(jnp.where(low_half, ys[0], ys[1]) + y_off[:, pr * LANES:(pr + 1) * LANES])
        states = jnp.dot(Bg.T.astype(BF16), Xd[:, g * 256:(g + 1) * 256], preferred_element_type=F32)
        hst_ref[g] = hprev * chunk_decay[:, g * 256:(g + 1) * 256] + states

    z = z_ref[...].astype(F32)
    gate = _silu(z)
    for g in range(2):
        yg = jnp.concatenate(y_parts[2 * g:2 * g + 2], axis=1) + dsk_ref[:, g * 256:(g + 1) * 256] * X[:, g * 256:(g + 1) * 256]
        yg = yg * gate[:, g * 256:(g + 1) * 256]
        ms = jnp.mean(yg * yg, axis=-1, keepdims=True)
        y_ref[:, g * 256:(g + 1) * 256] = (yg * lax.rsqrt(ms + LN_EPS) * ng_ref[:, g * 256:(g + 1) * 256]).astype(BF16)


def _ssd_constants():
    e64 = np.zeros((LANES, 512), np.float32)
    e128 = np.zeros((LANES, 1024), np.float32)
    for h in range(SSD_HEADS):
        e64[h, 64 * h:64 * (h + 1)] = 1.0
        e128[h, 128 * h:128 * (h + 1)] = 1.0
    tri = np.tril(np.ones((SSD_L, SSD_L), np.float32))
    return (jnp.asarray(e64, BF16), jnp.asarray(e128, BF16), jnp.asarray(tri, BF16), jnp.asarray(tri.T, BF16))


def _ssd_mixer(h, dt_raw, conv_w, conv_b, dt_bias, a_log, d_skip, norm_g, batch, seq):
    t = h.shape[0]
    L = SSD_L
    nc = seq // L
    e64, e128, tri, triT = _ssd_constants()
    pad_l = lambda v: jnp.pad(v.reshape(1, SSD_HEADS), ((0, 0), (0, LANES - SSD_HEADS)))
    pad_t = lambda v: jnp.pad(jnp.broadcast_to(v.reshape(SSD_HEADS, 1), (SSD_HEADS, L)), ((0, 16 - SSD_HEADS), (0, 0)))
    rb = lambda b, c: b * nc + c
    const = lambda shape: pl.BlockSpec(shape, lambda b, c: (0,) * len(shape))
    return pl.pallas_call(
        _ssd_kernel,
        out_shape=jax.ShapeDtypeStruct((t, GROUP_W), BF16),
        grid=(batch, nc),
        in_specs=[pl.BlockSpec((L, SSD_XBC), lambda b, c: (rb(b, c), COL_XBC // 2)),
                  pl.BlockSpec((L, GROUP_W), lambda b, c: (rb(b, c), COL_Z)),
                  pl.BlockSpec((L, LANES), lambda b, c: (rb(b, c), 0)),
                  const((SSD_CONV, SSD_XBC)), const((1, SSD_XBC)),
                  const((1, LANES)), const((1, LANES)), const((16, L)), const((16, L)),
                  const((1, GROUP_W)), const((1, GROUP_W)),
                  const((LANES, 512)), const((LANES, 1024)), const((L, L)), const((L, L))],
        out_specs=pl.BlockSpec((L, GROUP_W), lambda b, c: (rb(b, c), 0)),
        scratch_shapes=[pltpu.VMEM((SUBLANES + L, SSD_XBC), F32),
                        pltpu.VMEM((2, SSD_STATE, 256), F32)],
        compiler_params=_cparams("arbitrary", "arbitrary"),
        name="ssd_mixer",
    )(h, h, dt_raw, conv_w, conv_b.reshape(1, SSD_XBC),
      pad_l(dt_bias), pad_l(a_log), pad_t(dt_bias), pad_t(a_log),
      jnp.repeat(d_skip, 64).reshape(1, GROUP_W), norm_g.reshape(1, GROUP_W), e64, e128, tri, triT)


def _attn_kernel(q_ref, k_ref, v_ref, bias_ref, lq1_ref, lk1_ref, lq2_ref, lk2_ref, g_ref, o_ref,
                 m_ref, l_ref, acc_ref, *, lambda_init):
    qi = pl.program_id(2)
    tq = q_ref.shape[0]
    tk = tq
    lane = lax.broadcasted_iota(I32, (tq, LANES), 1)
    qs = []
    for hh in range(ATTN_HEADS):
        q = q_ref[:, hh * LANES:(hh + 1) * LANES]
        zero = jnp.zeros_like(q)
        qs += [jnp.where(lane < DIFF_QK_DIM, q, zero), jnp.where(lane >= DIFF_QK_DIM, q, zero)]
    m_ref[...] = jnp.full(m_ref.shape, -jnp.inf, F32)
    l_ref[...] = jnp.zeros(l_ref.shape, F32)
    acc_ref[...] = jnp.zeros(acc_ref.shape, F32)

    def tile(off, bias):
        for st in range(2 * ATTN_HEADS):
            hh = st // 2
            k = k_ref[pl.ds(off, tk), hh * LANES:(hh + 1) * LANES]
            v = v_ref[pl.ds(off, tk), hh * LANES:(hh + 1) * LANES]
            s = _dot_nt(qs[st], k)
            if bias is not None:
                s = s + bias
            m_prev = m_ref[st]
            m_new = jnp.maximum(m_prev, jnp.max(s, axis=-1, keepdims=True))
            alpha = jnp.exp2(m_prev - m_new)
            p = jnp.exp2(s - jnp.concatenate([m_new] * (tk // LANES), axis=1))
            psum = p[:, 0:LANES]
            for c in range(1, tk // LANES):
                psum = psum + p[:, c * LANES:(c + 1) * LANES]
            l_ref[st] = alpha * l_ref[st] + psum
            acc_ref[st] = alpha * acc_ref[st] + jnp.dot(p.astype(BF16), v, preferred_element_type=F32)
            m_ref[st] = m_new

    def body(j, carry):
        tile(pl.multiple_of(j * tk, tk), None)
        return carry

    lax.fori_loop(0, qi, body, 0)
    tile(pl.multiple_of(qi * tk, tk), bias_ref[...])

    lam = (jnp.exp(jnp.sum(lq1_ref[...] * lk1_ref[...], axis=-1, keepdims=True))
           - jnp.exp(jnp.sum(lq2_ref[...] * lk2_ref[...], axis=-1, keepdims=True)) + lambda_init)
    for hh in range(ATTN_HEADS):
        l0 = jnp.sum(l_ref[2 * hh], axis=-1, keepdims=True)
        l1 = jnp.sum(l_ref[2 * hh + 1], axis=-1, keepdims=True)
        o = acc_ref[2 * hh] / l0 - lam * (acc_ref[2 * hh + 1] / l1)
        ms = jnp.mean(o * o, axis=-1, keepdims=True)
        o_ref[:, hh * LANES:(hh + 1) * LANES] = (o * lax.rsqrt(ms + LN_EPS) * g_ref[...]
                                                 * (1.0 - lambda_init)).astype(BF16)


def _chunk_mask_bias(tq):
    r = np.arange(tq)[:, None] // CHUNK
    c = np.arange(tq)[None, :] // CHUNK
    return jnp.asarray(np.where(c <= r, 0.0, -np.inf), F32)


def _diff_attention(h, lq1, lk1, lq2, lk2, norm_g, lambda_init, batch, seq):
    t = h.shape[0]
    tq = min(512, seq)
    nq = seq // tq
    w = ATTN_HEADS * LANES
    qcol = COL_Q * TN // w
    kcol = COL_K * TN // w
    vcol = COL_V * TN // w
    vec = lambda n: pl.BlockSpec((1, n), lambda b, hd, qi: (0, 0))
    return pl.pallas_call(
        functools.partial(_attn_kernel, lambda_init=lambda_init),
        out_shape=jax.ShapeDtypeStruct((t, GROUP_W), BF16),
        grid=(batch, DIFF_HEADS // ATTN_HEADS, nq),
        in_specs=[pl.BlockSpec((tq, w), lambda b, hd, qi: (b * nq + qi, qcol + hd)),
                  pl.BlockSpec((seq, w), lambda b, hd, qi: (b, kcol + hd)),
                  pl.BlockSpec((seq, w), lambda b, hd, qi: (b, vcol + hd)),
                  pl.BlockSpec((tq, tq), lambda b, hd, qi: (0, 0)),
                  vec(DIFF_QK_DIM), vec(DIFF_QK_DIM), vec(DIFF_QK_DIM), vec(DIFF_QK_DIM), vec(DIFF_V_DIM)],
        out_specs=pl.BlockSpec((tq, w), lambda b, hd, qi: (b * nq + qi, hd)),
        scratch_shapes=[pltpu.VMEM((2 * ATTN_HEADS, tq, LANES), F32), pltpu.VMEM((2 * ATTN_HEADS, tq, LANES), F32),
                        pltpu.VMEM((2 * ATTN_HEADS, tq, DIFF_V_DIM), F32)],
        compiler_params=_cparams("arbitrary", "arbitrary", "arbitrary"),
        name="diff_attn",
    )(h, h, h, _chunk_mask_bias(tq), lq1.reshape(1, -1), lk1.reshape(1, -1), lq2.reshape(1, -1),
      lk2.reshape(1, -1), norm_g.reshape(1, -1))


CONF_HALO = 32
SC_HALO = 16
CONV_ROWS = 64


def _conv_kernel(u_ref, uh_ref, bg_ref, cg_ref, hh_ref, cgh_ref, hhh_ref,
                 dww_ref, dwb_ref, lng_ref, lnb_ref, pww_ref, pwb_ref, scw_ref,
                 yc_ref, yd_ref, hbuf_ref, pbuf_ref, cbuf_ref, sbuf_ref):
    i = pl.program_id(1)
    tm = u_ref.shape[0]

    def glu(u):
        u = u.astype(F32)
        return u[:, 0:GROUP_W] * _sigmoid(u[:, GROUP_W:2 * GROUP_W])

    first = (i == 0)
    hbuf_ref[0:CONF_HALO, :] = jnp.where(first, 0.0, glu(uh_ref[...]))
    hbuf_ref[CONF_HALO:CONF_HALO + tm, :] = glu(u_ref[...])
    pbuf_ref[0:SC_HALO, :] = jnp.where(first, 0.0, cgh_ref[...].astype(F32) * hhh_ref[...].astype(F32))
    pbuf_ref[SC_HALO:SC_HALO + tm, :] = cg_ref[...].astype(F32) * hh_ref[...].astype(F32)

    for b in range(1, SUBLANES):
        sbuf_ref[b - 1, 0:tm + CONF_HALO - SUBLANES, :] = hbuf_ref[pl.ds(b, tm + CONF_HALO - SUBLANES), :]
    for r0 in range(0, tm, CONV_ROWS):
        acc = jnp.zeros((CONV_ROWS, GROUP_W), F32) + dwb_ref[...]
        for k in range(CONF_KERNEL):
            off = CONF_HALO - (CONF_KERNEL - 1) + k
            phase, base = off % SUBLANES, r0 + off - off % SUBLANES
            src = hbuf_ref[pl.ds(base, CONV_ROWS), :] if phase == 0 else sbuf_ref[phase - 1, pl.ds(base, CONV_ROWS), :]
            acc = acc + dww_ref[k:k + 1, :] * src
        cbuf_ref[r0:r0 + CONV_ROWS, :] = acc
    hc = cbuf_ref[...]
    mu = jnp.mean(hc, axis=-1, keepdims=True)
    xc = hc - mu
    var = jnp.mean(xc * xc, axis=-1, keepdims=True)
    hn = _silu(xc * lax.rsqrt(var + LN_EPS) * lng_ref[...] + lnb_ref[...])
    yc = jnp.dot(hn.astype(BF16), pww_ref[...], preferred_element_type=F32) + pwb_ref[...]
    yc_ref[...] = yc.astype(BF16)

    sc = jnp.zeros((tm, GROUP_W), F32)
    for k in range(SC_KERNEL):
        sc = sc + scw_ref[k:k + 1, :] * pbuf_ref[pl.ds(SC_HALO - (SC_KERNEL - 1) + k, tm), :]
    yd_ref[...] = (bg_ref[...].astype(F32) * sc).astype(BF16)


def _conv_mixers(h, dw_w, dw_b, ln_g, ln_b, pw_w, pw_b, sc_w, batch, seq):
    t = h.shape[0]
    tm = min(512, seq)
    nt = seq // tm
    rb = lambda b, i: b * nt + i
    halo = lambda rows: (lambda b, i: jnp.maximum(rb(b, i) * (tm // rows) - 1, 0))
    hc, hs = halo(CONF_HALO), halo(SC_HALO)
    const = lambda shape: pl.BlockSpec(shape, lambda b, i: (0,) * len(shape))
    return pl.pallas_call(
        _conv_kernel,
        out_shape=(jax.ShapeDtypeStruct((t, GROUP_W), BF16), jax.ShapeDtypeStruct((t, GROUP_W), BF16)),
        grid=(batch, nt),
        in_specs=[pl.BlockSpec((tm, 2 * GROUP_W), lambda b, i: (rb(b, i), COL_CONF // 2)),
                  pl.BlockSpec((CONF_HALO, 2 * GROUP_W), lambda b, i: (hc(b, i), COL_CONF // 2)),
                  pl.BlockSpec((tm, GROUP_W), lambda b, i: (rb(b, i), COL_BG)),
                  pl.BlockSpec((tm, GROUP_W), lambda b, i: (rb(b, i), COL_CG)),
                  pl.BlockSpec((tm, GROUP_W), lambda b, i: (rb(b, i), COL_HH)),
                  pl.BlockSpec((SC_HALO, GROUP_W), lambda b, i: (hs(b, i), COL_CG)),
                  pl.BlockSpec((SC_HALO, GROUP_W), lambda b, i: (hs(b, i), COL_HH)),
                  const((CONF_KERNEL, GROUP_W)), const((1, GROUP_W)), const((1, GROUP_W)), const((1, GROUP_W)),
                  const((GROUP_W, GROUP_W)), const((1, GROUP_W)), const((SC_KERNEL, GROUP_W))],
        out_specs=(pl.BlockSpec((tm, GROUP_W), lambda b, i: (rb(b, i), 0)),
                   pl.BlockSpec((tm, GROUP_W), lambda b, i: (rb(b, i), 0))),
        scratch_shapes=[pltpu.VMEM((CONF_HALO + tm, GROUP_W), F32),
                        pltpu.VMEM((SC_HALO + tm, GROUP_W), F32),
                        pltpu.VMEM((tm, GROUP_W), F32),
                        pltpu.VMEM((SUBLANES - 1, CONF_HALO + tm, GROUP_W), F32)],
        compiler_params=_cparams("arbitrary", "arbitrary"),
        name="conv_mixers",
    )(h, h, h, h, h, h, h, dw_w, dw_b.reshape(1, -1), ln_g.reshape(1, -1), ln_b.reshape(1, -1),
      pw_w.astype(BF16), pw_b.reshape(1, -1), sc_w)


def _first_max4(vals):
    m1 = jnp.maximum(jnp.maximum(vals[0], vals[1]), jnp.maximum(vals[2], vals[3]))
    i1 = jnp.where(vals[0] == m1, 0, jnp.where(vals[1] == m1, 1, jnp.where(vals[2] == m1, 2, 3)))
    rest = [jnp.where(i1 == j, -jnp.inf, vals[j]) for j in range(4)]
    m2 = jnp.maximum(jnp.maximum(rest[0], rest[1]), jnp.maximum(rest[2], rest[3]))
    i2 = jnp.where(rest[0] == m2, 0, jnp.where(rest[1] == m2, 1, jnp.where(rest[2] == m2, 2, 3)))
    return m1, i1, m2, i2


def _outproj_kernel(ya_ref, yb_ref, yc_ref, yd_ref, x_ref, w_ref, g_ref, b_ref, x1_ref):
    mix = jnp.dot(ya_ref[...], w_ref[0:GROUP_W, :], preferred_element_type=F32)
    mix = mix + jnp.dot(yb_ref[...], w_ref[GROUP_W:2 * GROUP_W, :], preferred_element_type=F32)
    mix = mix + jnp.dot(yc_ref[...], w_ref[2 * GROUP_W:3 * GROUP_W, :], preferred_element_type=F32)
    mix = mix + jnp.dot(yd_ref[...], w_ref[3 * GROUP_W:4 * GROUP_W, :], preferred_element_type=F32)
    y = ALPHA * x_ref[...] + mix
    mu = jnp.mean(y, axis=-1, keepdims=True)
    yc = y - mu
    var = jnp.mean(yc * yc, axis=-1, keepdims=True)
    x1_ref[...] = yc * lax.rsqrt(var + LN_EPS) * g_ref[...] + b_ref[...]


def _router_kernel(x_ref, w2_ref, w1_ref, rb_ref, eid_ref, gate_ref):
    tm = x_ref.shape[0]
    x = x_ref[...]
    xh = x.astype(BF16)
    xm = (x - xh.astype(F32)).astype(BF16)
    r = (jnp.dot(xh, w2_ref[...], preferred_element_type=F32)
         + jnp.dot(xm, w1_ref[...], preferred_element_type=F32))
    r = r + pltpu.roll(r, LANES - N_EXPERTS, 1)
    logits = r.T[0:N_EXPERTS, :]
    aff = _sigmoid(logits)
    sel = aff + jnp.concatenate([rb_ref[...]] * (tm // LANES), axis=1)
    rows = [sel[e:e + 1, :] for e in range(N_EXPERTS)]
    arow = [aff[e:e + 1, :] for e in range(N_EXPERTS)]
    tops = [_first_max4(rows[4 * g:4 * g + 4]) for g in range(N_GROUPS)]
    score = [tp[0] + tp[2] for tp in tops]
    best = jnp.maximum(jnp.maximum(score[0], score[1]), jnp.maximum(score[2], score[3]))
    grp = jnp.where(score[0] == best, 0, jnp.where(score[1] == best, 1, jnp.where(score[2] == best, 2, 3)))
    pick = lambda idx: jnp.where(grp == 0, tops[0][idx], jnp.where(grp == 1, tops[1][idx],
                                 jnp.where(grp == 2, tops[2][idx], tops[3][idx])))
    e1 = grp * EXPERTS_PER_GROUP + pick(1)
    e2 = grp * EXPERTS_PER_GROUP + pick(3)
    a1 = jnp.zeros_like(best)
    a2 = jnp.zeros_like(best)
    for e in range(N_EXPERTS):
        a1 = jnp.where(e1 == e, arow[e], a1)
        a2 = jnp.where(e2 == e, arow[e], a2)
    den = a1 + a2
    zi = jnp.zeros((SUBLANES - 2, tm), I32)
    zf = jnp.zeros((SUBLANES - 2, tm), F32)
    eid_ref[...] = jnp.concatenate([e1.astype(I32), e2.astype(I32), zi], axis=0)
    gate_ref[...] = jnp.concatenate([a1 / den, a2 / den, zf], axis=0)


def _out_projection(ya, yb, yc, yd, x, w_out, ln_g, ln_b, layer):
    t, d = ya.shape[0], x.shape[1]
    tm = min(512, t)
    act = lambda: pl.BlockSpec((tm, GROUP_W), lambda i: (i, 0))
    const = lambda shape: pl.BlockSpec(shape, lambda i: (0,) * len(shape))
    return pl.pallas_call(
        _outproj_kernel,
        out_shape=jax.ShapeDtypeStruct((t, d), F32),
        grid=(t // tm,),
        in_specs=[act(), act(), act(), act(),
                  pl.BlockSpec((tm, d), lambda i: (i, 0)),
                  pl.BlockSpec((None, d, d), lambda i: (layer, 0, 0)), const((1, d)), const((1, d))],
        out_specs=pl.BlockSpec((tm, d), lambda i: (i, 0)),
        compiler_params=_cparams("parallel"),
        name="out_proj",
    )(ya, yb, yc, yd, x, w_out, ln_g.reshape(1, d), ln_b.reshape(1, d))


def _router(x1, router_w, router_bias):
    t, d = x1.shape
    tm = min(512, t)
    w_hi = router_w.astype(BF16)
    w_mid = (router_w - w_hi.astype(F32)).astype(BF16)
    w2 = jnp.concatenate([w_hi, w_mid, jnp.zeros((d, LANES - 2 * N_EXPERTS), BF16)], axis=1)
    w1 = jnp.concatenate([w_hi, jnp.zeros((d, LANES - N_EXPERTS), BF16)], axis=1)
    router_b = jnp.broadcast_to(router_bias.reshape(N_EXPERTS, 1), (N_EXPERTS, LANES))
    const = lambda shape: pl.BlockSpec(shape, lambda i: (0,) * len(shape))
    return pl.pallas_call(
        _router_kernel,
        out_shape=(jax.ShapeDtypeStruct((SUBLANES, t), I32), jax.ShapeDtypeStruct((SUBLANES, t), F32)),
        grid=(t // tm,),
        in_specs=[pl.BlockSpec((tm, d), lambda i: (i, 0)),
                  const((d, LANES)), const((d, LANES)), const((N_EXPERTS, LANES))],
        out_specs=(pl.BlockSpec((SUBLANES, tm), lambda i: (0, i)),
                   pl.BlockSpec((SUBLANES, tm), lambda i: (0, i))),
        compiler_params=_cparams("parallel"),
        name="router",
    )(x1, w2, w1, router_b)


PAIR_A = (0, 0, 0, 1, 1, 3)
PAIR_B = (1, 2, 3, 3, 2, 2)
N_PAIRS = len(PAIR_A)
N_BUCKETS = N_GROUPS * N_PAIRS


def _moe_pair_kernel(ea_ref, eb_ref, nused_ref, nv_ref, src_ref, srcn_ref, dst_ref, dstp_ref, dstpp_ref, gt_ref, x_hbm,
                     wga_ref, wua_ref, wda_ref, wgb_ref, wub_ref, wdb_ref, g_ref, b_ref, o_hbm,
                     xbuf0_ref, xbuf1_ref, obuf0_ref, obuf1_ref, gsem_ref, ssem_ref):
    i = pl.program_id(0)
    nused = nused_ref[0]
    xbufs = (xbuf0_ref, xbuf1_ref)
    obufs = (obuf0_ref, obuf1_ref)

    def gather(rows_ref, s):
        return [pltpu.make_async_copy(x_hbm.at[pl.ds(rows_ref[0, 0, r], 1), :], xbufs[s].at[pl.ds(r, 1), :],
                                      gsem_ref.at[s]) for r in range(MOE_BLOCK)]

    def scatter(rows_ref, s):
        return [pltpu.make_async_copy(obufs[s].at[pl.ds(r, 1), :], o_hbm.at[pl.ds(rows_ref[0, 0, r], 1), :],
                                      ssem_ref.at[s]) for r in range(MOE_BLOCK)]

    def start(copies):
        for r, c in enumerate(copies):
            c.start(priority=r % 2)

    def wait(copies):
        for c in copies:
            c.wait()

    def scatter_rows(rows_ref, s, n, go):
        def one(r):
            return pltpu.make_async_copy(obufs[s].at[pl.ds(r, 1), :], o_hbm.at[pl.ds(rows_ref[0, 0, r], 1), :],
                                         ssem_ref.at[s])

        @pl.when(n == MOE_BLOCK)
        def _():
            (start if go else wait)(scatter(rows_ref, s))

        @pl.when(n < MOE_BLOCK)
        def _():
            def body(r, carry):
                if go:
                    one(r).start()
                else:
                    one(r).wait()
                return carry
            lax.fori_loop(0, n, body, 0)

    def ffn(xh, wg_ref, wu_ref, wd_ref):
        hg = jnp.dot(xh, wg_ref[...], preferred_element_type=F32)
        hu = jnp.dot(xh, wu_ref[...], preferred_element_type=F32)
        return jnp.dot((_silu(hg) * hu).astype(BF16), wd_ref[...], preferred_element_type=F32)

    @pl.when(i == 0)
    def _():
        start(gather(src_ref, 0))

    for s in range(2):
        mine = jnp.logical_and(i < nused, i % 2 == s)

        @pl.when(mine)
        def _():
            wait(gather(src_ref, s))

        @pl.when(jnp.logical_and(mine, i >= 2))
        def _():
            scatter_rows(dstpp_ref, s, nv_ref[jnp.maximum(i - 2, 0)], False)

        @pl.when(jnp.logical_and(mine, i >= 1))
        def _():
            scatter_rows(dstp_ref, 1 - s, nv_ref[jnp.maximum(i - 1, 0)], True)

        @pl.when(mine)
        def _():
            start(gather(srcn_ref, 1 - s))
            x = xbufs[s][...]
            xh = x.astype(BF16)
            gt = gt_ref[...]
            moe = (ffn(xh, wga_ref, wua_ref, wda_ref) * gt[:, 0:1]
                   + ffn(xh, wgb_ref, wub_ref, wdb_ref) * gt[:, 1:2])
            y = ALPHA * x + moe
            mu = jnp.mean(y, axis=-1, keepdims=True)
            yc = y - mu
            var = jnp.mean(yc * yc, axis=-1, keepdims=True)
            obufs[s][...] = yc * lax.rsqrt(var + LN_EPS) * g_ref[...] + b_ref[...]

        last = jnp.logical_and(mine, i + 1 == nused)

        @pl.when(last)
        def _():
            scatter_rows(dst_ref, s, nv_ref[i], True)
            scatter_rows(dst_ref, s, nv_ref[i], False)
            wait(gather(srcn_ref, 1 - s))

        @pl.when(jnp.logical_and(last, i >= 1))
        def _():
            scatter_rows(dstp_ref, 1 - s, nv_ref[jnp.maximum(i - 1, 0)], False)


def _moe_pairs(x1, tables, gates_sorted, wg, wu, wd, ln_g, ln_b, layer):
    rows, ea, eb, nused, nvalid = tables
    t, d = x1.shape
    n_blk = rows.shape[0]
    rows_spec = lambda off, clip_hi: pl.BlockSpec(
        (1, 1, MOE_BLOCK),
        lambda i, a, b, nu, nv: (jnp.clip(i + off, 0, nu[0] - 1) if clip_hi else jnp.maximum(i + off, 0), 0, 0),
        memory_space=pltpu.SMEM)
    wspec = lambda shape, sel: pl.BlockSpec(
        (None, None) + shape, lambda i, a, b, nu, nv: (layer, (a if sel == 0 else b)[i], 0, 0),
        pipeline_mode=pl.Buffered(1))
    vec = pl.BlockSpec((1, d), lambda i, a, b, nu, nv: (0, 0))
    return pl.pallas_call(
        _moe_pair_kernel,
        out_shape=jax.ShapeDtypeStruct((t, d), F32),
        grid_spec=pltpu.PrefetchScalarGridSpec(
            num_scalar_prefetch=4,
            grid=(n_blk,),
            in_specs=[rows_spec(0, True), rows_spec(1, True), rows_spec(0, True), rows_spec(-1, False),
                      rows_spec(-2, False),
                      pl.BlockSpec((MOE_BLOCK, TOP_K), lambda i, a, b, nu, nv: (i, 0)),
                      pl.BlockSpec(memory_space=pl.ANY),
                      wspec((d, D_EXPERT), 0), wspec((d, D_EXPERT), 0), wspec((D_EXPERT, d), 0),
                      wspec((d, D_EXPERT), 1), wspec((d, D_EXPERT), 1), wspec((D_EXPERT, d), 1),
                      vec, vec],
            out_specs=pl.BlockSpec(memory_space=pl.ANY),
            scratch_shapes=[pltpu.VMEM((MOE_BLOCK, d), F32), pltpu.VMEM((MOE_BLOCK, d), F32),
                            pltpu.VMEM((MOE_BLOCK, d), F32), pltpu.VMEM((MOE_BLOCK, d), F32),
                            pltpu.SemaphoreType.DMA((2,)), pltpu.SemaphoreType.DMA((2,))]),
        compiler_params=_cparams("arbitrary"),
        name="moe_pairs",
    )(ea, eb, nused, nvalid, rows, rows, rows, rows, rows, gates_sorted, x1, wg, wu, wd, wg, wu, wd,
      ln_g.reshape(1, d), ln_b.reshape(1, d))


def _pair_tables(eid, gate, t):
    e1, e2 = eid[0], eid[1]
    grp = e1 // EXPERTS_PER_GROUP
    a, b = e1 % EXPERTS_PER_GROUP, e2 % EXPERTS_PER_GROUP
    lo, hi = jnp.minimum(a, b), jnp.maximum(a, b)
    pair_of = np.zeros((EXPERTS_PER_GROUP * EXPERTS_PER_GROUP,), np.int32)
    for p in range(N_PAIRS):
        pa, pb = min(PAIR_A[p], PAIR_B[p]), max(PAIR_A[p], PAIR_B[p])
        pair_of[pa * EXPERTS_PER_GROUP + pb] = p
    pidx = jnp.asarray(pair_of)[lo * EXPERTS_PER_GROUP + hi]
    bucket = grp * N_PAIRS + pidx
    in_a = a == jnp.asarray(PAIR_A, dtype=I32)[pidx]
    gates = jnp.stack([jnp.where(in_a, gate[0], gate[1]), jnp.where(in_a, gate[1], gate[0])], axis=1)

    onehot = (bucket[:, None] == jnp.arange(N_BUCKETS, dtype=I32)[None, :]).astype(I32)
    csum = jnp.cumsum(onehot, axis=0)
    counts = csum[-1]
    rank = jnp.take_along_axis(csum, bucket[:, None], axis=1)[:, 0] - 1
    padded = ((counts + MOE_BLOCK - 1) // MOE_BLOCK) * MOE_BLOCK
    pends = jnp.cumsum(padded)
    pos = (pends - padded)[bucket] + rank
    n_blk = -(-t // MOE_BLOCK) + N_BUCKETS
    n_pad = n_blk * MOE_BLOCK
    rows = jnp.zeros((n_pad,), I32).at[pos].set(jnp.arange(t, dtype=I32))
    nused = (pends[-1] // MOE_BLOCK).astype(I32)
    blk = jnp.minimum(jnp.arange(n_blk, dtype=I32), nused - 1)
    blk_bucket = jnp.sum((pends[None, :] <= (blk * MOE_BLOCK)[:, None]).astype(I32), axis=1)
    base = (blk_bucket // N_PAIRS) * EXPERTS_PER_GROUP
    ea = (base + jnp.asarray(PAIR_A, dtype=I32)[blk_bucket % N_PAIRS]).astype(I32)
    eb = (base + jnp.asarray(PAIR_B, dtype=I32)[blk_bucket % N_PAIRS]).astype(I32)
    valid_end = pends - padded + counts
    nvalid = jnp.clip(valid_end[blk_bucket] - blk * MOE_BLOCK, 0, MOE_BLOCK).astype(I32)
    tables = (rows.reshape(n_blk, 1, MOE_BLOCK), ea, eb, nused.reshape(1), nvalid)
    return tables, gates[rows]


def _rope_tables(seq):
    half = DIFF_QK_DIM // 2
    inv = 1.0 / (ROPE_THETA ** (jnp.arange(0, DIFF_QK_DIM, 2, dtype=F32) / DIFF_QK_DIM))
    ang = jnp.arange(seq, dtype=F32)[:, None] * inv[None, :]
    cos, sin = jnp.cos(ang), jnp.sin(ang)
    cos_t = jnp.concatenate([cos, cos, cos, cos], axis=1)
    sin_t = jnp.concatenate([-sin, sin, -sin, sin], axis=1)
    return cos_t, sin_t


W_IN_SPLITS = (0, 512, 1536, 1544, 2056, 2568, 3080, 4104, 5640)


def _wprep_kernel(w_ref, o_ref):
    w = w_ref[...]
    z, xbc, dt, q, k, v, conf, sc = [w[:, a:b] for a, b in zip(W_IN_SPLITS[:-1], W_IN_SPLITS[1:])]
    rows = w.shape[0]
    o_ref[0] = jnp.concatenate([xbc, conf], axis=1).astype(BF16)
    o_ref[1] = jnp.concatenate([sc, z], axis=1).astype(BF16)
    o_ref[2] = jnp.concatenate([q, k, v, dt, jnp.zeros((rows, TN - SSD_HEADS), F32)], axis=1).astype(BF16)


def _prep_w_in(w_all, layer):
    _, d, n = w_all.shape
    rows = 256
    return pl.pallas_call(
        _wprep_kernel,
        out_shape=jax.ShapeDtypeStruct((INPROJ_TILES, d, INPROJ_TN), BF16),
        grid=(d // rows,),
        in_specs=[pl.BlockSpec((None, rows, n), lambda i: (layer, i, 0))],
        out_specs=pl.BlockSpec((INPROJ_TILES, rows, INPROJ_TN), lambda i: (0, i, 0)),
        compiler_params=_cparams("parallel"),
        name="w_in_prep",
    )(w_all)


def kernel(x, ln_in_g, ln_in_b, w_in, ssd_conv_w, ssd_conv_b, ssd_dt_bias, ssd_a_log, ssd_d, ssd_norm_g, diff_lq1, diff_lk1, diff_lq2, diff_lk2, diff_norm_g, conf_dw_w, conf_dw_b, conf_ln_g, conf_ln_b, conf_pw_w, conf_pw_b, sc_conv_w, w_out, ln1_g, ln1_b, router_w, router_bias, moe_w_gate, moe_w_up, moe_w_down, ln2_g, ln2_b):
    b, s, d = x.shape
    t = b * s
    cos_t, sin_t = _rope_tables(s)
    xf = _layer_norm(x.reshape(t, d), ln_in_g, ln_in_b)
    w_out_b = w_out.astype(BF16)
    wg_b, wu_b, wd_b = moe_w_gate.astype(BF16), moe_w_up.astype(BF16), moe_w_down.astype(BF16)
    for l in range(DEPTH):
        lambda_init = 0.8 - 0.6 * math.exp(-0.3 * l)
        h, dt_raw = _in_projection(xf, _prep_w_in(w_in, l), cos_t, sin_t, s, t)
        ya = _ssd_mixer(h, dt_raw, ssd_conv_w[l], ssd_conv_b[l], ssd_dt_bias[l], ssd_a_log[l],
                        ssd_d[l], ssd_norm_g[l], b, s)
        yb = _diff_attention(h, diff_lq1[l], diff_lk1[l], diff_lq2[l], diff_lk2[l], diff_norm_g[l],
                             lambda_init, b, s)
        yc, yd = _conv_mixers(h, conf_dw_w[l], conf_dw_b[l], conf_ln_g[l], conf_ln_b[l], conf_pw_w[l],
                              conf_pw_b[l], sc_conv_w[l], b, s)
        x1 = _out_projection(ya, yb, yc, yd, xf, w_out_b, ln1_g[l], ln1_b[l], l)
        eid, gate = _router(x1, router_w, router_bias)
        tables, gates_sorted = _pair_tables(eid, gate, t)
        xf = _moe_pairs(x1, tables, gates_sorted, wg_b, wu_b, wd_b, ln2_g[l], ln2_b[l], l)
    return xf.reshape(b, s, d)
```

```python
import functools
import math

import numpy as np
import jax
import jax.numpy as jnp
from jax import lax
from jax.experimental import pallas as pl
from jax.experimental.pallas import tpu as pltpu

F32 = jnp.float32
BF16 = jnp.bfloat16
I32 = jnp.int32

D_MODEL = 2048
DEPTH = 2
CHUNK = 64
GROUP_W = 512
SSD_HEADS = 8
SSD_STATE = 128
SSD_CONV = 4
SSD_XBC = 1024
DIFF_HEADS = 4
DIFF_QK_DIM = 64
DIFF_V_DIM = 128
ROPE_THETA = 10000.0
CONF_KERNEL = 31
SC_KERNEL = 3
N_EXPERTS = 16
N_GROUPS = 4
EXPERTS_PER_GROUP = 4
TOP_K = 2
D_EXPERT = 1024
MOE_BLOCK = 256
ALPHA = (2 * DEPTH) ** 0.25
LN_EPS = 1e-5
LOG2E = 1.4426950408889634

LANES = 128
SUBLANES = 8
VMEM_LIMIT_BYTES = 56 * 1024 * 1024

COL_XBC, COL_CONF, COL_BG, COL_CG, COL_HH, COL_Z, COL_Q, COL_K, COL_V, COL_DT = 0, 2, 4, 5, 6, 7, 8, 9, 10, 11
H_COLS = 12 * 512
TN = 512
INPROJ_TN = 2048
INPROJ_TILES = H_COLS // INPROJ_TN
SSD_L = 128
ATTN_HEADS = 2


def _cparams(*sem):
    return pltpu.CompilerParams(dimension_semantics=tuple(sem), vmem_limit_bytes=VMEM_LIMIT_BYTES)


def _sigmoid(x):
    return 1.0 / (1.0 + jnp.exp(-x))


def _silu(x):
    return x * _sigmoid(x)


def _softplus(x):
    return jnp.maximum(x, 0.0) + jnp.log(1.0 + jnp.exp(-jnp.abs(x)))


def _split3(v):
    hi = v.astype(BF16)
    r = v - hi.astype(F32)
    mid = r.astype(BF16)
    lo = (r - mid.astype(F32)).astype(BF16)
    return hi, mid, lo


def _dot_exact_rhs(v, m):
    return sum(jnp.dot(p, m, preferred_element_type=F32) for p in _split3(v))


def _dot_exact_lhs(m, v):
    return sum(jnp.dot(m, p, preferred_element_type=F32) for p in _split3(v))


def _dot_nt(a, b):
    return lax.dot_general(a, b, (((1,), (1,)), ((), ())), preferred_element_type=F32)


def _ln_kernel(x_ref, g_ref, b_ref, o_ref):
    x = x_ref[...]
    mu = jnp.mean(x, axis=-1, keepdims=True)
    xc = x - mu
    var = jnp.mean(xc * xc, axis=-1, keepdims=True)
    o_ref[...] = xc * lax.rsqrt(var + LN_EPS) * g_ref[...] + b_ref[...]


def _layer_norm(x, g, b):
    t, d = x.shape
    tm = min(512, t)
    return pl.pallas_call(
        _ln_kernel,
        out_shape=jax.ShapeDtypeStruct((t, d), F32),
        grid=(t // tm,),
        in_specs=[pl.BlockSpec((tm, d), lambda i: (i, 0)),
                  pl.BlockSpec((1, d), lambda i: (0, 0)),
                  pl.BlockSpec((1, d), lambda i: (0, 0))],
        out_specs=pl.BlockSpec((tm, d), lambda i: (i, 0)),
        compiler_params=_cparams("parallel"),
        name="entry_ln",
    )(x, g.reshape(1, d), b.reshape(1, d))


def _inproj_kernel(x_ref, w_ref, cos_ref, sin_ref, h_ref, dt_ref, xb_ref):
    j = pl.program_id(1)
    tm = x_ref.shape[0]

    @pl.when(j == 0)
    def _():
        xb_ref[...] = x_ref[...].astype(BF16)

    x = xb_ref[...]
    last = INPROJ_TILES - 1

    def rope_store(c, acc, scale):
        cs = cos_ref[...] * scale
        sn = sin_ref[...] * scale
        lane = lax.broadcasted_iota(I32, (tm, LANES), 1)
        first_half = (lane % DIFF_QK_DIM) < (DIFF_QK_DIM // 2)
        for hh in range(TN // LANES):
            a = acc[:, hh * LANES:(hh + 1) * LANES]
            rot = jnp.where(first_half, pltpu.roll(a, LANES - 32, 1), pltpu.roll(a, 32, 1))
            h_ref[:, c * TN + hh * LANES:c * TN + (hh + 1) * LANES] = (a * cs + rot * sn).astype(BF16)

    for c in range(INPROJ_TN // TN):
        acc = jnp.dot(x, w_ref[:, c * TN:(c + 1) * TN], preferred_element_type=F32)
        if c * TN == (COL_Q * TN) % INPROJ_TN or c * TN == (COL_K * TN) % INPROJ_TN:
            scale = LOG2E * DIFF_QK_DIM ** -0.5 if c * TN == (COL_Q * TN) % INPROJ_TN else 1.0

            @pl.when(j == last)
            def _():
                rope_store(c, acc, scale)

            @pl.when(j != last)
            def _():
                h_ref[:, c * TN:(c + 1) * TN] = acc.astype(BF16)
        else:
            h_ref[:, c * TN:(c + 1) * TN] = acc.astype(BF16)
        if c * TN == (COL_DT * TN) % INPROJ_TN:
            @pl.when(j == last)
            def _():
                dt_ref[...] = acc[:, 0:LANES]


def _in_projection(x, w_tiles, cos_t, sin_t, seq, t):
    d = x.shape[1]
    tm = min(1024, seq)
    nseq = seq // tm
    return pl.pallas_call(
        _inproj_kernel,
        out_shape=(jax.ShapeDtypeStruct((t, H_COLS), BF16),
                   jax.ShapeDtypeStruct((t, LANES), F32)),
        grid=(t // tm, INPROJ_TILES),
        in_specs=[pl.BlockSpec((tm, d), lambda i, j: (i, 0)),
                  pl.BlockSpec((None, d, INPROJ_TN), lambda i, j: (j, 0, 0)),
                  pl.BlockSpec((tm, LANES), lambda i, j: (i % nseq, 0)),
                  pl.BlockSpec((tm, LANES), lambda i, j: (i % nseq, 0))],
        out_specs=(pl.BlockSpec((tm, INPROJ_TN), lambda i, j: (i, j)),
                   pl.BlockSpec((tm, LANES), lambda i, j: (i, 0))),
        scratch_shapes=[pltpu.VMEM((tm, d), BF16)],
        compiler_params=_cparams("arbitrary", "arbitrary"),
        name="in_proj",
    )(x, w_tiles, cos_t, sin_t)


def _ssd_kernel(xbc_ref, z_ref, dt_ref, cw_ref, cb_ref, dtb_ref, alog_ref, dtbT_ref, alogT_ref,
                dsk_ref, ng_ref, e64_ref, e128_ref, tri_ref, triT_ref, y_ref, xpad_ref, hst_ref):
    c = pl.program_id(1)
    L = SSD_L

    @pl.when(c == 0)
    def _():
        xpad_ref[0:SUBLANES, :] = jnp.zeros((SUBLANES, SSD_XBC), F32)
        hst_ref[...] = jnp.zeros(hst_ref.shape, F32)

    cur = xbc_ref[...].astype(F32)
    xpad_ref[SUBLANES:SUBLANES + L, :] = cur
    conv = cb_ref[...]
    for k in range(SSD_CONV):
        conv = conv + cw_ref[k:k + 1, :] * xpad_ref[pl.ds(SUBLANES - (SSD_CONV - 1) + k, L), :]
    xpad_ref[0:SUBLANES, :] = cur[L - SUBLANES:L, :]
    xbc = _silu(conv)
    X = xbc[:, 0:GROUP_W]
    Bm = xbc[:, GROUP_W:GROUP_W + 2 * SSD_STATE]
    Cm = xbc[:, GROUP_W + 2 * SSD_STATE:SSD_XBC]

    dt_raw = dt_ref[...]
    dt = _softplus(dt_raw + dtb_ref[...])
    acs = _dot_exact_lhs(tri_ref[...], dt * (-jnp.exp(alog_ref[...])))
    dtT = _softplus(dt_raw.T[0:16, :] + dtbT_ref[...])
    acsT = _dot_exact_rhs(dtT * (-jnp.exp(alogT_ref[...])), triT_ref[...])
    dt64 = _dot_exact_rhs(dt, e64_ref[...])
    acs64 = _dot_exact_rhs(acs, e64_ref[...])
    acs128 = _dot_exact_rhs(acs, e128_ref[...])

    last = acs64[L - 1:L, :]
    in_decay = jnp.exp(acs64)
    decay = jnp.exp(last - acs64)
    chunk_decay = jnp.exp(last)
    Xdt = X * dt64
    Xd = (Xdt * decay).astype(BF16)
    Xdt_b = Xdt.astype(BF16)

    row = lax.broadcasted_iota(I32, (L, L), 0)
    col = lax.broadcasted_iota(I32, (L, L), 1)
    causal = row >= col
    lane = lax.broadcasted_iota(I32, (L, LANES), 1)
    low_half = lane < 64

    y_parts = []
    for g in range(2):
        Bg = Bm[:, g * SSD_STATE:(g + 1) * SSD_STATE]
        Cg = Cm[:, g * SSD_STATE:(g + 1) * SSD_STATE].astype(BF16)
        cb = _dot_nt(Cg, Bg.astype(BF16))
        hprev = hst_ref[g]
        y_off = jnp.dot(Cg, hprev.astype(BF16), preferred_element_type=F32) * in_decay[:, g * 256:(g + 1) * 256]
        for pr in range(2):
            xp = Xdt_b[:, (2 * g + pr) * LANES:(2 * g + pr + 1) * LANES]
            ys = []
            for hh in range(2):
                h = 4 * g + 2 * pr + hh
                seg = acs128[:, h * LANES:(h + 1) * LANES] - acsT[h:h + 1, :]
                lmat = jnp.exp(jnp.where(causal, seg, -jnp.inf))
                w = (cb * lmat).astype(BF16)
                ys.append(jnp.dot(w, xp, preferred_element_type=F32))
            y_parts.append(jnp.where(low_half, ys[0], ys[1]) + y_off[:, pr * LANES:(pr + 1) * LANES])
        states = jnp.dot(Bg.T.astype(BF16), Xd[:, g * 256:(g + 1) * 256], preferred_element_type=F32)
        hst_ref[g] = hprev * chunk_decay[:, g * 256:(g + 1) * 256] + states

    z = z_ref[...].astype(F32)
    gate = _silu(z)
    for g in range(2):
        yg = jnp.concatenate(y_parts[2 * g:2 * g + 2], axis=1) + dsk_ref[:, g * 256:(g + 1) * 256] * X[:, g * 256:(g + 1) * 256]
        yg = yg * gate[:, g * 256:(g + 1) * 256]
        ms = jnp.mean(yg * yg, axis=-1, keepdims=True)
        y_ref[:, g * 256:(g + 1) * 256] = (yg * lax.rsqrt(ms + LN_EPS) * ng_ref[:, g * 256:(g + 1) * 256]).astype(BF16)


def _ssd_constants():
    e64 = np.zeros((LANES, 512), np.float32)
    e128 = np.zeros((LANES, 1024), np.float32)
    for h in range(SSD_HEADS):
        e64[h, 64 * h:64 * (h + 1)] = 1.0
        e128[h, 128 * h:128 * (h + 1)] = 1.0
    tri = np.tril(np.ones((SSD_L, SSD_L), np.float32))
    return (jnp.asarray(e64, BF16), jnp.asarray(e128, BF16), jnp.asarray(tri, BF16), jnp.asarray(tri.T, BF16))


def _ssd_mixer(h, dt_raw, conv_w, conv_b, dt_bias, a_log, d_skip, norm_g, batch, seq):
    t = h.shape[0]
    L = SSD_L
    nc = seq // L
    e64, e128, tri, triT = _ssd_constants()
    pad_l = lambda v: jnp.pad(v.reshape(1, SSD_HEADS), ((0, 0), (0, LANES - SSD_HEADS)))
    pad_t = lambda v: jnp.pad(jnp.broadcast_to(v.reshape(SSD_HEADS, 1), (SSD_HEADS, L)), ((0, 16 - SSD_HEADS), (0, 0)))
    rb = lambda b, c: b * nc + c
    const = lambda shape: pl.BlockSpec(shape, lambda b, c: (0,) * len(shape))
    return pl.pallas_call(
        _ssd_kernel,
        out_shape=jax.ShapeDtypeStruct((t, GROUP_W), BF16),
        grid=(batch, nc),
        in_specs=[pl.BlockSpec((L, SSD_XBC), lambda b, c: (rb(b, c), COL_XBC // 2)),
                  pl.BlockSpec((L, GROUP_W), lambda b, c: (rb(b, c), COL_Z)),
                  pl.BlockSpec((L, LANES), lambda b, c: (rb(b, c), 0)),
                  const((SSD_CONV, SSD_XBC)), const((1, SSD_XBC)),
                  const((1, LANES)), const((1, LANES)), const((16, L)), const((16, L)),
                  const((1, GROUP_W)), const((1, GROUP_W)),
                  const((LANES, 512)), const((LANES, 1024)), const((L, L)), const((L, L))],
        out_specs=pl.BlockSpec((L, GROUP_W), lambda b, c: (rb(b, c), 0)),
        scratch_shapes=[pltpu.VMEM((SUBLANES + L, SSD_XBC), F32),
                        pltpu.VMEM((2, SSD_STATE, 256), F32)],
        compiler_params=_cparams("arbitrary", "arbitrary"),
        name="ssd_mixer",
    )(h, h, dt_raw, conv_w, conv_b.reshape(1, SSD_XBC),
      pad_l(dt_bias), pad_l(a_log), pad_t(dt_bias), pad_t(a_log),
      jnp.repeat(d_skip, 64).reshape(1, GROUP_W), norm_g.reshape(1, GROUP_W), e64, e128, tri, triT)


def _attn_kernel(q_ref, k_ref, v_ref, bias_ref, lq1_ref, lk1_ref, lq2_ref, lk2_ref, g_ref, o_ref,
                 m_ref, l_ref, acc_ref, *, lambda_init):
    qi = pl.program_id(2)
    tq = q_ref.shape[0]
    tk = tq
    lane = lax.broadcasted_iota(I32, (tq, LANES), 1)
    qs = []
    for hh in range(ATTN_HEADS):
        q = q_ref[:, hh * LANES:(hh + 1) * LANES]
        zero = jnp.zeros_like(q)
        qs += [jnp.where(lane < DIFF_QK_DIM, q, zero), jnp.where(lane >= DIFF_QK_DIM, q, zero)]
    m_ref[...] = jnp.full(m_ref.shape, -jnp.inf, F32)
    l_ref[...] = jnp.zeros(l_ref.shape, F32)
    acc_ref[...] = jnp.zeros(acc_ref.shape, F32)

    def tile(off, bias):
        for st in range(2 * ATTN_HEADS):
            hh = st // 2
            k = k_ref[pl.ds(off, tk), hh * LANES:(hh + 1) * LANES]
            v = v_ref[pl.ds(off, tk), hh * LANES:(hh + 1) * LANES]
            s = _dot_nt(qs[st], k)
            if bias is not None:
                s = s + bias
            m_prev = m_ref[st]
            m_new = jnp.maximum(m_prev, jnp.max(s, axis=-1, keepdims=True))
            alpha = jnp.exp2(m_prev - m_new)
            p = jnp.exp2(s - jnp.concatenate([m_new] * (tk // LANES), axis=1))
            psum = p[:, 0:LANES]
            for c in range(1, tk // LANES):
                psum = psum + p[:, c * LANES:(c + 1) * LANES]
            l_ref[st] = alpha * l_ref[st] + psum
            acc_ref[st] = alpha * acc_ref[st] + jnp.dot(p.astype(BF16), v, preferred_element_type=F32)
            m_ref[st] = m_new

    def body(j, carry):
        tile(pl.multiple_of(j * tk, tk), None)
        return carry

    lax.fori_loop(0, qi, body, 0)
    tile(pl.multiple_of(qi * tk, tk), bias_ref[...])

    lam = (jnp.exp(jnp.sum(lq1_ref[...] * lk1_ref[...], axis=-1, keepdims=True))
           - jnp.exp(jnp.sum(lq2_ref[...] * lk2_ref[...], axis=-1, keepdims=True)) + lambda_init)
    for hh in range(ATTN_HEADS):
        l0 = jnp.sum(l_ref[2 * hh], axis=-1, keepdims=True)
        l1 = jnp.sum(l_ref[2 * hh + 1], axis=-1, keepdims=True)
        o = acc_ref[2 * hh] / l0 - lam * (acc_ref[2 * hh + 1] / l1)
        ms = jnp.mean(o * o, axis=-1, keepdims=True)
        o_ref[:, hh * LANES:(hh + 1) * LANES] = (o * lax.rsqrt(ms + LN_EPS) * g_ref[...]
                                                 * (1.0 - lambda_init)).astype(BF16)


def _chunk_mask_bias(tq):
    r = np.arange(tq)[:, None] // CHUNK
    c = np.arange(tq)[None, :] // CHUNK
    return jnp.asarray(np.where(c <= r, 0.0, -np.inf), F32)


def _diff_attention(h, lq1, lk1, lq2, lk2, norm_g, lambda_init, batch, seq):
    t = h.shape[0]
    tq = min(512, seq)
    nq = seq // tq
    w = ATTN_HEADS * LANES
    qcol = COL_Q * TN // w
    kcol = COL_K * TN // w
    vcol = COL_V * TN // w
    vec = lambda n: pl.BlockSpec((1, n), lambda b, hd, qi: (0, 0))
    return pl.pallas_call(
        functools.partial(_attn_kernel, lambda_init=lambda_init),
        out_shape=jax.ShapeDtypeStruct((t, GROUP_W), BF16),
        grid=(batch, DIFF_HEADS // ATTN_HEADS, nq),
        in_specs=[pl.BlockSpec((tq, w), lambda b, hd, qi: (b * nq + qi, qcol + hd)),
                  pl.BlockSpec((seq, w), lambda b, hd, qi: (b, kcol + hd)),
                  pl.BlockSpec((seq, w), lambda b, hd, qi: (b, vcol + hd)),
                  pl.BlockSpec((tq, tq), lambda b, hd, qi: (0, 0)),
                  vec(DIFF_QK_DIM), vec(DIFF_QK_DIM), vec(DIFF_QK_DIM), vec(DIFF_QK_DIM), vec(DIFF_V_DIM)],
        out_specs=pl.BlockSpec((tq, w), lambda b, hd, qi: (b * nq + qi, hd)),
        scratch_shapes=[pltpu.VMEM((2 * ATTN_HEADS, tq, LANES), F32), pltpu.VMEM((2 * ATTN_HEADS, tq, LANES), F32),
                        pltpu.VMEM((2 * ATTN_HEADS, tq, DIFF_V_DIM), F32)],
        compiler_params=_cparams("arbitrary", "arbitrary", "arbitrary"),
        name="diff_attn",
    )(h, h, h, _chunk_mask_bias(tq), lq1.reshape(1, -1), lk1.reshape(1, -1), lq2.reshape(1, -1),
      lk2.reshape(1, -1), norm_g.reshape(1, -1))


CONF_HALO = 32
SC_HALO = 16
CONV_ROWS = 64


def _conv_kernel(u_ref, uh_ref, bg_ref, cg_ref, hh_ref, cgh_ref, hhh_ref,
                 dww_ref, dwb_ref, lng_ref, lnb_ref, pww_ref, pwb_ref, scw_ref,
                 yc_ref, yd_ref, hbuf_ref, pbuf_ref, cbuf_ref, sbuf_ref):
    i = pl.program_id(1)
    tm = u_ref.shape[0]

    def glu(u):
        u = u.astype(F32)
        return u[:, 0:GROUP_W] * _sigmoid(u[:, GROUP_W:2 * GROUP_W])

    first = (i == 0)
    hbuf_ref[0:CONF_HALO, :] = jnp.where(first, 0.0, glu(uh_ref[...]))
    hbuf_ref[CONF_HALO:CONF_HALO + tm, :] = glu(u_ref[...])
    pbuf_ref[0:SC_HALO, :] = jnp.where(first, 0.0, cgh_ref[...].astype(F32) * hhh_ref[...].astype(F32))
    pbuf_ref[SC_HALO:SC_HALO + tm, :] = cg_ref[...].astype(F32) * hh_ref[...].astype(F32)

    for b in range(1, SUBLANES):
        sbuf_ref[b - 1, 0:tm + CONF_HALO - SUBLANES, :] = hbuf_ref[pl.ds(b, tm + CONF_HALO - SUBLANES), :]
    for r0 in range(0, tm, CONV_ROWS):
        acc = jnp.zeros((CONV_ROWS, GROUP_W), F32) + dwb_ref[...]
        for k in range(CONF_KERNEL):
            off = CONF_HALO - (CONF_KERNEL - 1) + k
            phase, base = off % SUBLANES, r0 + off - off % SUBLANES
            src = hbuf_ref[pl.ds(base, CONV_ROWS), :] if phase == 0 else sbuf_ref[phase - 1, pl.ds(base, CONV_ROWS), :]
            acc = acc + dww_ref[k:k + 1, :] * src
        cbuf_ref[r0:r0 + CONV_ROWS, :] = acc
    hc = cbuf_ref[...]
    mu = jnp.mean(hc, axis=-1, keepdims=True)
    xc = hc - mu
    var = jnp.mean(xc * xc, axis=-1, keepdims=True)
    hn = _silu(xc * lax.rsqrt(var + LN_EPS) * lng_ref[...] + lnb_ref[...])
    yc = jnp.dot(hn.astype(BF16), pww_ref[...], preferred_element_type=F32) + pwb_ref[...]
    yc_ref[...] = yc.astype(BF16)

    sc = jnp.zeros((tm, GROUP_W), F32)
    for k in range(SC_KERNEL):
        sc = sc + scw_ref[k:k + 1, :] * pbuf_ref[pl.ds(SC_HALO - (SC_KERNEL - 1) + k, tm), :]
    yd_ref[...] = (bg_ref[...].astype(F32) * sc).astype(BF16)


def _conv_mixers(h, dw_w, dw_b, ln_g, ln_b, pw_w, pw_b, sc_w, batch, seq):
    t = h.shape[0]
    tm = min(512, seq)
    nt = seq // tm
    rb = lambda b, i: b * nt + i
    halo = lambda rows: (lambda b, i: jnp.maximum(rb(b, i) * (tm // rows) - 1, 0))
    hc, hs = halo(CONF_HALO), halo(SC_HALO)
    const = lambda shape: pl.BlockSpec(shape, lambda b, i: (0,) * len(shape))
    return pl.pallas_call(
        _conv_kernel,
        out_shape=(jax.ShapeDtypeStruct((t, GROUP_W), BF16), jax.ShapeDtypeStruct((t, GROUP_W), BF16)),
        grid=(batch, nt),
        in_specs=[pl.BlockSpec((tm, 2 * GROUP_W), lambda b, i: (rb(b, i), COL_CONF // 2)),
                  pl.BlockSpec((CONF_HALO, 2 * GROUP_W), lambda b, i: (hc(b, i), COL_CONF // 2)),
                  pl.BlockSpec((tm, GROUP_W), lambda b, i: (rb(b, i), COL_BG)),
                  pl.BlockSpec((tm, GROUP_W), lambda b, i: (rb(b, i), COL_CG)),
                  pl.BlockSpec((tm, GROUP_W), lambda b, i: (rb(b, i), COL_HH)),
                  pl.BlockSpec((SC_HALO, GROUP_W), lambda b, i: (hs(b, i), COL_CG)),
                  pl.BlockSpec((SC_HALO, GROUP_W), lambda b, i: (hs(b, i), COL_HH)),
                  const((CONF_KERNEL, GROUP_W)), const((1, GROUP_W)), const((1, GROUP_W)), const((1, GROUP_W)),
                  const((GROUP_W, GROUP_W)), const((1, GROUP_W)), const((SC_KERNEL, GROUP_W))],
        out_specs=(pl.BlockSpec((tm, GROUP_W), lambda b, i: (rb(b, i), 0)),
                   pl.BlockSpec((tm, GROUP_W), lambda b, i: (rb(b, i), 0))),
        scratch_shapes=[pltpu.VMEM((CONF_HALO + tm, GROUP_W), F32),
                        pltpu.VMEM((SC_HALO + tm, GROUP_W), F32),
                        pltpu.VMEM((tm, GROUP_W), F32),
                        pltpu.VMEM((SUBLANES - 1, CONF_HALO + tm, GROUP_W), F32)],
        compiler_params=_cparams("arbitrary", "arbitrary"),
        name="conv_mixers",
    )(h, h, h, h, h, h, h, dw_w, dw_b.reshape(1, -1), ln_g.reshape(1, -1), ln_b.reshape(1, -1),
      pw_w.astype(BF16), pw_b.reshape(1, -1), sc_w)


def _first_max4(vals):
    m1 = jnp.maximum(jnp.maximum(vals[0], vals[1]), jnp.maximum(vals[2], vals[3]))
    i1 = jnp.where(vals[0] == m1, 0, jnp.where(vals[1] == m1, 1, jnp.where(vals[2] == m1, 2, 3)))
    rest = [jnp.where(i1 == j, -jnp.inf, vals[j]) for j in range(4)]
    m2 = jnp.maximum(jnp.maximum(rest[0], rest[1]), jnp.maximum(rest[2], rest[3]))
    i2 = jnp.where(rest[0] == m2, 0, jnp.where(rest[1] == m2, 1, jnp.where(rest[2] == m2, 2, 3)))
    return m1, i1, m2, i2


def _outproj_kernel(ya_ref, yb_ref, yc_ref, yd_ref, x_ref, w_ref, g_ref, b_ref, x1_ref):
    mix = jnp.dot(ya_ref[...], w_ref[0:GROUP_W, :], preferred_element_type=F32)
    mix = mix + jnp.dot(yb_ref[...], w_ref[GROUP_W:2 * GROUP_W, :], preferred_element_type=F32)
    mix = mix + jnp.dot(yc_ref[...], w_ref[2 * GROUP_W:3 * GROUP_W, :], preferred_element_type=F32)
    mix = mix + jnp.dot(yd_ref[...], w_ref[3 * GROUP_W:4 * GROUP_W, :], preferred_element_type=F32)
    y = ALPHA * x_ref[...] + mix
    mu = jnp.mean(y, axis=-1, keepdims=True)
    yc = y - mu
    var = jnp.mean(yc * yc, axis=-1, keepdims=True)
    x1_ref[...] = yc * lax.rsqrt(var + LN_EPS) * g_ref[...] + b_ref[...]


def _router_kernel(x_ref, w2_ref, w1_ref, rb_ref, eid_ref, gate_ref):
    tm = x_ref.shape[0]
    x = x_ref[...]
    xh = x.astype(BF16)
    xm = (x - xh.astype(F32)).astype(BF16)
    r = (jnp.dot(xh, w2_ref[...], preferred_element_type=F32)
         + jnp.dot(xm, w1_ref[...], preferred_element_type=F32))
    r = r + pltpu.roll(r, LANES - N_EXPERTS, 1)
    logits = r.T[0:N_EXPERTS, :]
    aff = _sigmoid(logits)
    sel = aff + jnp.concatenate([rb_ref[...]] * (tm // LANES), axis=1)
    rows = [sel[e:e + 1, :] for e in range(N_EXPERTS)]
    arow = [aff[e:e + 1, :] for e in range(N_EXPERTS)]
    tops = [_first_max4(rows[4 * g:4 * g + 4]) for g in range(N_GROUPS)]
    score = [tp[0] + tp[2] for tp in tops]
    best = jnp.maximum(jnp.maximum(score[0], score[1]), jnp.maximum(score[2], score[3]))
    grp = jnp.where(score[0] == best, 0, jnp.where(score[1] == best, 1, jnp.where(score[2] == best, 2, 3)))
    pick = lambda idx: jnp.where(grp == 0, tops[0][idx], jnp.where(grp == 1, tops[1][idx],
                                 jnp.where(grp == 2, tops[2][idx], tops[3][idx])))
    e1 = grp * EXPERTS_PER_GROUP + pick(1)
    e2 = grp * EXPERTS_PER_GROUP + pick(3)
    a1 = jnp.zeros_like(best)
    a2 = jnp.zeros_like(best)
    for e in range(N_EXPERTS):
        a1 = jnp.where(e1 == e, arow[e], a1)
        a2 = jnp.where(e2 == e, arow[e], a2)
    den = a1 + a2
    zi = jnp.zeros((SUBLANES - 2, tm), I32)
    zf = jnp.zeros((SUBLANES - 2, tm), F32)
    eid_ref[...] = jnp.concatenate([e1.astype(I32), e2.astype(I32), zi], axis=0)
    gate_ref[...] = jnp.concatenate([a1 / den, a2 / den, zf], axis=0)


def _out_projection(ya, yb, yc, yd, x, w_out, ln_g, ln_b, layer):
    t, d = ya.shape[0], x.shape[1]
    tm = min(512, t)
    act = lambda: pl.BlockSpec((tm, GROUP_W), lambda i: (i, 0))
    const = lambda shape: pl.BlockSpec(shape, lambda i: (0,) * len(shape))
    return pl.pallas_call(
        _outproj_kernel,
        out_shape=jax.ShapeDtypeStruct((t, d), F32),
        grid=(t // tm,),
        in_specs=[act(), act(), act(), act(),
                  pl.BlockSpec((tm, d), lambda i: (i, 0)),
                  pl.BlockSpec((None, d, d), lambda i: (layer, 0, 0)), const((1, d)), const((1, d))],
        out_specs=pl.BlockSpec((tm, d), lambda i: (i, 0)),
        compiler_params=_cparams("parallel"),
        name="out_proj",
    )(ya, yb, yc, yd, x, w_out, ln_g.reshape(1, d), ln_b.reshape(1, d))


def _router(x1, router_w, router_bias):
    t, d = x1.shape
    tm = min(512, t)
    w_hi = router_w.astype(BF16)
    w_mid = (router_w - w_hi.astype(F32)).astype(BF16)
    w2 = jnp.concatenate([w_hi, w_mid, jnp.zeros((d, LANES - 2 * N_EXPERTS), BF16)], axis=1)
    w1 = jnp.concatenate([w_hi, jnp.zeros((d, LANES - N_EXPERTS), BF16)], axis=1)
    router_b = jnp.broadcast_to(router_bias.reshape(N_EXPERTS, 1), (N_EXPERTS, LANES))
    const = lambda shape: pl.BlockSpec(shape, lambda i: (0,) * len(shape))
    return pl.pallas_call(
        _router_kernel,
        out_shape=(jax.ShapeDtypeStruct((SUBLANES, t), I32), jax.ShapeDtypeStruct((SUBLANES, t), F32)),
        grid=(t // tm,),
        in_specs=[pl.BlockSpec((tm, d), lambda i: (i, 0)),
                  const((d, LANES)), const((d, LANES)), const((N_EXPERTS, LANES))],
        out_specs=(pl.BlockSpec((SUBLANES, tm), lambda i: (0, i)),
                   pl.BlockSpec((SUBLANES, tm), lambda i: (0, i))),
        compiler_params=_cparams("parallel"),
        name="router",
    )(x1, w2, w1, router_b)


PAIR_A = (0, 0, 0, 1, 1, 3)
PAIR_B = (1, 2, 3, 3, 2, 2)
N_PAIRS = len(PAIR_A)
N_BUCKETS = N_GROUPS * N_PAIRS


def _moe_pair_kernel(ea_ref, eb_ref, nused_ref, nv_ref, src_ref, srcn_ref, dst_ref, dstp_ref, dstpp_ref, gt_ref, x_hbm,
                     wga_ref, wua_ref, wda_ref, wgb_ref, wub_ref, wdb_ref, g_ref, b_ref, o_hbm,
                     xbuf0_ref, xbuf1_ref, obuf0_ref, obuf1_ref, gsem_ref, ssem_ref):
    i = pl.program_id(0)
    nused = nused_ref[0]
    xbufs = (xbuf0_ref, xbuf1_ref)
    obufs = (obuf0_ref, obuf1_ref)

    def gather(rows_ref, s):
        return [pltpu.make_async_copy(x_hbm.at[pl.ds(rows_ref[0, 0, r], 1), :], xbufs[s].at[pl.ds(r, 1), :],
                                      gsem_ref.at[s]) for r in range(MOE_BLOCK)]

    def scatter(rows_ref, s):
        return [pltpu.make_async_copy(obufs[s].at[pl.ds(r, 1), :], o_hbm.at[pl.ds(rows_ref[0, 0, r], 1), :],
                                      ssem_ref.at[s]) for r in range(MOE_BLOCK)]

    def start(copies):
        for r, c in enumerate(copies):
            c.start(priority=r % 2)

    def wait(copies):
        for c in copies:
            c.wait()

    def scatter_rows(rows_ref, s, n, go):
        def one(r):
            return pltpu.make_async_copy(obufs[s].at[pl.ds(r, 1), :], o_hbm.at[pl.ds(rows_ref[0, 0, r], 1), :],
                                         ssem_ref.at[s])

        @pl.when(n == MOE_BLOCK)
        def _():
            (start if go else wait)(scatter(rows_ref, s))

        @pl.when(n < MOE_BLOCK)
        def _():
            def body(r, carry):
                if go:
                    one(r).start()
                else:
                    one(r).wait()
                return carry
            lax.fori_loop(0, n, body, 0)

    def ffn(xh, wg_ref, wu_ref, wd_ref):
        hg = jnp.dot(xh, wg_ref[...], preferred_element_type=F32)
        hu = jnp.dot(xh, wu_ref[...], preferred_element_type=F32)
        return jnp.dot((_silu(hg) * hu).astype(BF16), wd_ref[...], preferred_element_type=F32)

    @pl.when(i == 0)
    def _():
        start(gather(src_ref, 0))

    for s in range(2):
        mine = jnp.logical_and(i < nused, i % 2 == s)

        @pl.when(mine)
        def _():
            wait(gather(src_ref, s))

        @pl.when(jnp.logical_and(mine, i >= 2))
        def _():
            scatter_rows(dstpp_ref, s, nv_ref[jnp.maximum(i - 2, 0)], False)

        @pl.when(jnp.logical_and(mine, i >= 1))
        def _():
            scatter_rows(dstp_ref, 1 - s, nv_ref[jnp.maximum(i - 1, 0)], True)

        @pl.when(mine)
        def _():
            start(gather(srcn_ref, 1 - s))
            x = xbufs[s][...]
            xh = x.astype(BF16)
            gt = gt_ref[...]
            moe = (ffn(xh, wga_ref, wua_ref, wda_ref) * gt[:, 0:1]
                   + ffn(xh, wgb_ref, wub_ref, wdb_ref) * gt[:, 1:2])
            y = ALPHA * x + moe
            mu = jnp.mean(y, axis=-1, keepdims=True)
            yc = y - mu
            var = jnp.mean(yc * yc, axis=-1, keepdims=True)
            obufs[s][...] = yc * lax.rsqrt(var + LN_EPS) * g_ref[...] + b_ref[...]

        last = jnp.logical_and(mine, i + 1 == nused)

        @pl.when(last)
        def _():
            scatter_rows(dst_ref, s, nv_ref[i], True)
            scatter_rows(dst_ref, s, nv_ref[i], False)
            wait(gather(srcn_ref, 1 - s))

        @pl.when(jnp.logical_and(last, i >= 1))
        def _():
            scatter_rows(dstp_ref, 1 - s, nv_ref[jnp.maximum(i - 1, 0)], False)


def _moe_pairs(x1, tables, gates_sorted, wg, wu, wd, ln_g, ln_b, layer):
    rows, ea, eb, nused, nvalid = tables
    t, d = x1.shape
    n_blk = rows.shape[0]
    rows_spec = lambda off, clip_hi: pl.BlockSpec(
        (1, 1, MOE_BLOCK),
        lambda i, a, b, nu, nv: (jnp.clip(i + off, 0, nu[0] - 1) if clip_hi else jnp.maximum(i + off, 0), 0, 0),
        memory_space=pltpu.SMEM)
    wspec = lambda shape, sel: pl.BlockSpec(
        (None, None) + shape, lambda i, a, b, nu, nv: (layer, (a if sel == 0 else b)[i], 0, 0),
        pipeline_mode=pl.Buffered(1 if sel == 0 else 2))
    vec = pl.BlockSpec((1, d), lambda i, a, b, nu, nv: (0, 0))
    return pl.pallas_call(
        _moe_pair_kernel,
        out_shape=jax.ShapeDtypeStruct((t, d), F32),
        grid_spec=pltpu.PrefetchScalarGridSpec(
            num_scalar_prefetch=4,
            grid=(n_blk,),
            in_specs=[rows_spec(0, True), rows_spec(1, True), rows_spec(0, True), rows_spec(-1, False),
                      rows_spec(-2, False),
                      pl.BlockSpec((MOE_BLOCK, TOP_K), lambda i, a, b, nu, nv: (i, 0)),
                      pl.BlockSpec(memory_space=pl.ANY),
                      wspec((d, D_EXPERT), 0), wspec((d, D_EXPERT), 0), wspec((D_EXPERT, d), 0),
                      wspec((d, D_EXPERT), 1), wspec((d, D_EXPERT), 1), wspec((D_EXPERT, d), 1),
                      vec, vec],
            out_specs=pl.BlockSpec(memory_space=pl.ANY),
            scratch_shapes=[pltpu.VMEM((MOE_BLOCK, d), F32), pltpu.VMEM((MOE_BLOCK, d), F32),
                            pltpu.VMEM((MOE_BLOCK, d), F32), pltpu.VMEM((MOE_BLOCK, d), F32),
                            pltpu.SemaphoreType.DMA((2,)), pltpu.SemaphoreType.DMA((2,))]),
        compiler_params=_cparams("arbitrary"),
        name="moe_pairs",
    )(ea, eb, nused, nvalid, rows, rows, rows, rows, rows, gates_sorted, x1, wg, wu, wd, wg, wu, wd,
      ln_g.reshape(1, d), ln_b.reshape(1, d))


def _pair_tables(eid, gate, t):
    e1, e2 = eid[0], eid[1]
    grp = e1 // EXPERTS_PER_GROUP
    a, b = e1 % EXPERTS_PER_GROUP, e2 % EXPERTS_PER_GROUP
    pidx = jnp.zeros_like(a)
    a_of_pair = jnp.zeros_like(a)
    for p in range(N_PAIRS):
        hit = jnp.logical_or(jnp.logical_and(a == PAIR_A[p], b == PAIR_B[p]),
                             jnp.logical_and(a == PAIR_B[p], b == PAIR_A[p]))
        pidx = jnp.where(hit, p, pidx)
        a_of_pair = jnp.where(hit, PAIR_A[p], a_of_pair)
    bucket = grp * N_PAIRS + pidx
    in_a = a == a_of_pair
    gate_a = jnp.where(in_a, gate[0], gate[1])
    gate_b = jnp.where(in_a, gate[1], gate[0])

    onehot = (bucket[:, None] == jnp.arange(N_BUCKETS, dtype=I32)[None, :]).astype(I32)
    csum = jnp.cumsum(onehot, axis=0)
    counts = csum[-1]
    padded = ((counts + MOE_BLOCK - 1) // MOE_BLOCK) * MOE_BLOCK
    pends = jnp.cumsum(padded)
    pos = jnp.sum(onehot * (csum - 1 + (pends - padded)[None, :]), axis=1)
    n_blk = -(-t // MOE_BLOCK) + N_BUCKETS
    n_pad = n_blk * MOE_BLOCK
    rows = jnp.zeros((n_pad,), I32).at[pos].set(jnp.arange(t, dtype=I32))
    nused = (pends[-1] // MOE_BLOCK).astype(I32)
    blk = jnp.minimum(jnp.arange(n_blk, dtype=I32), nused - 1)
    blk_bucket = jnp.sum((pends[None, :] <= (blk * MOE_BLOCK)[:, None]).astype(I32), axis=1)
    base = (blk_bucket // N_PAIRS) * EXPERTS_PER_GROUP
    pair = blk_bucket % N_PAIRS
    ea, eb = base, base
    for p in range(N_PAIRS):
        ea = ea + jnp.where(pair == p, PAIR_A[p], 0)
        eb = eb + jnp.where(pair == p, PAIR_B[p], 0)
    ea, eb = ea.astype(I32), eb.astype(I32)
    valid_end = pends - padded + counts
    blk_onehot = (blk_bucket[:, None] == jnp.arange(N_BUCKETS, dtype=I32)[None, :]).astype(I32)
    nvalid = jnp.clip(jnp.sum(blk_onehot * valid_end[None, :], axis=1) - blk * MOE_BLOCK, 0, MOE_BLOCK).astype(I32)
    tables = (rows.reshape(n_blk, 1, MOE_BLOCK), ea, eb, nused.reshape(1), nvalid)
    return tables, jnp.stack([gate_a[rows], gate_b[rows]], axis=1)


def _rope_tables(seq):
    half = DIFF_QK_DIM // 2
    inv = 1.0 / (ROPE_THETA ** (jnp.arange(0, DIFF_QK_DIM, 2, dtype=F32) / DIFF_QK_DIM))
    ang = jnp.arange(seq, dtype=F32)[:, None] * inv[None, :]
    cos, sin = jnp.cos(ang), jnp.sin(ang)
    cos_t = jnp.concatenate([cos, cos, cos, cos], axis=1)
    sin_t = jnp.concatenate([-sin, sin, -sin, sin], axis=1)
    return cos_t, sin_t


W_IN_SPLITS = (0, 512, 1536, 1544, 2056, 2568, 3080, 4104, 5640)


def _wprep_kernel(w_ref, o_ref):
    w = w_ref[...]
    z, xbc, dt, q, k, v, conf, sc = [w[:, a:b] for a, b in zip(W_IN_SPLITS[:-1], W_IN_SPLITS[1:])]
    rows = w.shape[0]
    o_ref[0] = jnp.concatenate([xbc, conf], axis=1).astype(BF16)
    o_ref[1] = jnp.concatenate([sc, z], axis=1).astype(BF16)
    o_ref[2] = jnp.concatenate([q, k, v, dt, jnp.zeros((rows, TN - SSD_HEADS), F32)], axis=1).astype(BF16)


def _prep_w_in(w_all, layer):
    _, d, n = w_all.shape
    rows = 256
    return pl.pallas_call(
        _wprep_kernel,
        out_shape=jax.ShapeDtypeStruct((INPROJ_TILES, d, INPROJ_TN), BF16),
        grid=(d // rows,),
        in_specs=[pl.BlockSpec((None, rows, n), lambda i: (layer, i, 0))],
        out_specs=pl.BlockSpec((INPROJ_TILES, rows, INPROJ_TN), lambda i: (0, i, 0)),
        compiler_params=_cparams("parallel"),
        name="w_in_prep",
    )(w_all)


def kernel(x, ln_in_g, ln_in_b, w_in, ssd_conv_w, ssd_conv_b, ssd_dt_bias, ssd_a_log, ssd_d, ssd_norm_g, diff_lq1, diff_lk1, diff_lq2, diff_lk2, diff_norm_g, conf_dw_w, conf_dw_b, conf_ln_g, conf_ln_b, conf_pw_w, conf_pw_b, sc_conv_w, w_out, ln1_g, ln1_b, router_w, router_bias, moe_w_gate, moe_w_up, moe_w_down, ln2_g, ln2_b):
    b, s, d = x.shape
    t = b * s
    cos_t, sin_t = _rope_tables(s)
    xf = _layer_norm(x.reshape(t, d), ln_in_g, ln_in_b)
    w_out_b = w_out.astype(BF16)
    wg_b, wu_b, wd_b = moe_w_gate.astype(BF16), moe_w_up.astype(BF16), moe_w_down.astype(BF16)
    for l in range(DEPTH):
        lambda_init = 0.8 - 0.6 * math.exp(-0.3 * l)
        h, dt_raw = _in_projection(xf, _prep_w_in(w_in, l), cos_t, sin_t, s, t)
        ya = _ssd_mixer(h, dt_raw, ssd_conv_w[l], ssd_conv_b[l], ssd_dt_bias[l], ssd_a_log[l],
                        ssd_d[l], ssd_norm_g[l], b, s)
        yb = _diff_attention(h, diff_lq1[l], diff_lk1[l], diff_lq2[l], diff_lk2[l], diff_norm_g[l],
                             lambda_init, b, s)
        yc, yd = _conv_mixers(h, conf_dw_w[l], conf_dw_b[l], conf_ln_g[l], conf_ln_b[l], conf_pw_w[l],
                              conf_pw_b[l], sc_conv_w[l], b, s)
        x1 = _out_projection(ya, yb, yc, yd, xf, w_out_b, ln1_g[l], ln1_b[l], l)
        eid, gate = _router(x1, router_w, router_bias)
        tables, gates_sorted = _pair_tables(eid, gate, t)
        xf = _moe_pairs(x1, tables, gates_sorted, wg_b, wu_b, wd_b, ln2_g[l], ln2_b[l], l)
    return xf.reshape(b, s, d)
```

```python
import functools
import math

import numpy as np
import jax
import jax.numpy as jnp
from jax import lax
from jax.experimental import pallas as pl
from jax.experimental.pallas import tpu as pltpu

F32 = jnp.float32
BF16 = jnp.bfloat16
I32 = jnp.int32

D_MODEL = 2048
DEPTH = 2
CHUNK = 64
GROUP_W = 512
SSD_HEADS = 8
SSD_STATE = 128
SSD_CONV = 4
SSD_XBC = 1024
DIFF_HEADS = 4
DIFF_QK_DIM = 64
DIFF_V_DIM = 128
ROPE_THETA = 10000.0
CONF_KERNEL = 31
SC_KERNEL = 3
N_EXPERTS = 16
N_GROUPS = 4
EXPERTS_PER_GROUP = 4
TOP_K = 2
D_EXPERT = 1024
MOE_BLOCK = 256
ALPHA = (2 * DEPTH) ** 0.25
LN_EPS = 1e-5
LOG2E = 1.4426950408889634

LANES = 128
SUBLANES = 8
VMEM_LIMIT_BYTES = 56 * 1024 * 1024

COL_XBC, COL_CONF, COL_BG, COL_CG, COL_HH, COL_Z, COL_Q, COL_K, COL_V, COL_DT = 0, 2, 4, 5, 6, 7, 8, 9, 10, 11
H_COLS = 12 * 512
TN = 512
INPROJ_TN = 2048
INPROJ_TILES = H_COLS // INPROJ_TN
SSD_L = 128
ATTN_HEADS = 2


def _cparams(*sem):
    return pltpu.CompilerParams(dimension_semantics=tuple(sem), vmem_limit_bytes=VMEM_LIMIT_BYTES)


def _sigmoid(x):
    return 1.0 / (1.0 + jnp.exp(-x))


def _silu(x):
    return x * _sigmoid(x)


def _softplus(x):
    return jnp.maximum(x, 0.0) + jnp.log(1.0 + jnp.exp(-jnp.abs(x)))


def _split3(v):
    hi = v.astype(BF16)
    r = v - hi.astype(F32)
    mid = r.astype(BF16)
    lo = (r - mid.astype(F32)).astype(BF16)
    return hi, mid, lo


def _dot_exact_rhs(v, m):
    return sum(jnp.dot(p, m, preferred_element_type=F32) for p in _split3(v))


def _dot_exact_lhs(m, v):
    return sum(jnp.dot(m, p, preferred_element_type=F32) for p in _split3(v))


def _dot_nt(a, b):
    return lax.dot_general(a, b, (((1,), (1,)), ((), ())), preferred_element_type=F32)


def _ln_kernel(x_ref, g_ref, b_ref, o_ref):
    x = x_ref[...]
    mu = jnp.mean(x, axis=-1, keepdims=True)
    xc = x - mu
    var = jnp.mean(xc * xc, axis=-1, keepdims=True)
    o_ref[...] = xc * lax.rsqrt(var + LN_EPS) * g_ref[...] + b_ref[...]


def _layer_norm(x, g, b):
    t, d = x.shape
    tm = min(512, t)
    return pl.pallas_call(
        _ln_kernel,
        out_shape=jax.ShapeDtypeStruct((t, d), F32),
        grid=(t // tm,),
        in_specs=[pl.BlockSpec((tm, d), lambda i: (i, 0)),
                  pl.BlockSpec((1, d), lambda i: (0, 0)),
                  pl.BlockSpec((1, d), lambda i: (0, 0))],
        out_specs=pl.BlockSpec((tm, d), lambda i: (i, 0)),
        compiler_params=_cparams("parallel"),
        name="entry_ln",
    )(x, g.reshape(1, d), b.reshape(1, d))


def _inproj_kernel(x_ref, w_ref, cos_ref, sin_ref, h_ref, dt_ref, xb_ref):
    j = pl.program_id(1)
    tm = x_ref.shape[0]

    @pl.when(j == 0)
    def _():
        xb_ref[...] = x_ref[...].astype(BF16)

    x = xb_ref[...]
    last = INPROJ_TILES - 1

    def rope_store(c, acc, scale):
        cs = cos_ref[...] * scale
        sn = sin_ref[...] * scale
        lane = lax.broadcasted_iota(I32, (tm, LANES), 1)
        first_half = (lane % DIFF_QK_DIM) < (DIFF_QK_DIM // 2)
        for hh in range(TN // LANES):
            a = acc[:, hh * LANES:(hh + 1) * LANES]
            rot = jnp.where(first_half, pltpu.roll(a, LANES - 32, 1), pltpu.roll(a, 32, 1))
            h_ref[:, c * TN + hh * LANES:c * TN + (hh + 1) * LANES] = (a * cs + rot * sn).astype(BF16)

    for c in range(INPROJ_TN // TN):
        acc = jnp.dot(x, w_ref[:, c * TN:(c + 1) * TN], preferred_element_type=F32)
        if c * TN == (COL_Q * TN) % INPROJ_TN or c * TN == (COL_K * TN) % INPROJ_TN:
            scale = LOG2E * DIFF_QK_DIM ** -0.5 if c * TN == (COL_Q * TN) % INPROJ_TN else 1.0

            @pl.when(j == last)
            def _():
                rope_store(c, acc, scale)

            @pl.when(j != last)
            def _():
                h_ref[:, c * TN:(c + 1) * TN] = acc.astype(BF16)
        else:
            h_ref[:, c * TN:(c + 1) * TN] = acc.astype(BF16)
        if c * TN == (COL_DT * TN) % INPROJ_TN:
            @pl.when(j == last)
            def _():
                dt_ref[...] = acc[:, 0:LANES]


def _in_projection(x, w_tiles, cos_t, sin_t, seq, t):
    d = x.shape[1]
    tm = min(1024, seq)
    nseq = seq // tm
    return pl.pallas_call(
        _inproj_kernel,
        out_shape=(jax.ShapeDtypeStruct((t, H_COLS), BF16),
                   jax.ShapeDtypeStruct((t, LANES), F32)),
        grid=(t // tm, INPROJ_TILES),
        in_specs=[pl.BlockSpec((tm, d), lambda i, j: (i, 0)),
                  pl.BlockSpec((None, d, INPROJ_TN), lambda i, j: (j, 0, 0)),
                  pl.BlockSpec((tm, LANES), lambda i, j: (i % nseq, 0)),
                  pl.BlockSpec((tm, LANES), lambda i, j: (i % nseq, 0))],
        out_specs=(pl.BlockSpec((tm, INPROJ_TN), lambda i, j: (i, j)),
                   pl.BlockSpec((tm, LANES), lambda i, j: (i, 0))),
        scratch_shapes=[pltpu.VMEM((tm, d), BF16)],
        compiler_params=_cparams("arbitrary", "arbitrary"),
        name="in_proj",
    )(x, w_tiles, cos_t, sin_t)


def _ssd_kernel(xbc_ref, z_ref, dt_ref, cw_ref, cb_ref, dtb_ref, alog_ref, dtbT_ref, alogT_ref,
                dsk_ref, ng_ref, e64_ref, e128_ref, tri_ref, triT_ref, y_ref, xpad_ref, hst_ref):
    c = pl.program_id(1)
    L = SSD_L

    @pl.when(c == 0)
    def _():
        xpad_ref[0:SUBLANES, :] = jnp.zeros((SUBLANES, SSD_XBC), F32)
        hst_ref[...] = jnp.zeros(hst_ref.shape, F32)

    cur = xbc_ref[...].astype(F32)
    xpad_ref[SUBLANES:SUBLANES + L, :] = cur
    conv = cb_ref[...]
    for k in range(SSD_CONV):
        conv = conv + cw_ref[k:k + 1, :] * xpad_ref[pl.ds(SUBLANES - (SSD_CONV - 1) + k, L), :]
    xpad_ref[0:SUBLANES, :] = cur[L - SUBLANES:L, :]
    xbc = _silu(conv)
    X = xbc[:, 0:GROUP_W]
    Bm = xbc[:, GROUP_W:GROUP_W + 2 * SSD_STATE]
    Cm = xbc[:, GROUP_W + 2 * SSD_STATE:SSD_XBC]

    dt_raw = dt_ref[...]
    dt = _softplus(dt_raw + dtb_ref[...])
    acs = _dot_exact_lhs(tri_ref[...], dt * (-jnp.exp(alog_ref[...])))
    dtT = _softplus(dt_raw.T[0:16, :] + dtbT_ref[...])
    acsT = _dot_exact_rhs(dtT * (-jnp.exp(alogT_ref[...])), triT_ref[...])
    dt64 = _dot_exact_rhs(dt, e64_ref[...])
    acs64 = _dot_exact_rhs(acs, e64_ref[...])
    acs128 = _dot_exact_rhs(acs, e128_ref[...])

    last = acs64[L - 1:L, :]
    in_decay = jnp.exp(acs64)
    decay = jnp.exp(last - acs64)
    chunk_decay = jnp.exp(last)
    Xdt = X * dt64
    Xd = (Xdt * decay).astype(BF16)
    Xdt_b = Xdt.astype(BF16)

    row = lax.broadcasted_iota(I32, (L, L), 0)
    col = lax.broadcasted_iota(I32, (L, L), 1)
    causal = row >= col
    lane = lax.broadcasted_iota(I32, (L, LANES), 1)
    low_half = lane < 64

    y_parts = []
    for g in range(2):
        Bg = Bm[:, g * SSD_STATE:(g + 1) * SSD_STATE]
        Cg = Cm[:, g * SSD_STATE:(g + 1) * SSD_STATE].astype(BF16)
        cb = _dot_nt(Cg, Bg.astype(BF16))
        hprev = hst_ref[g]
        y_off = jnp.dot(Cg, hprev.astype(BF16), preferred_element_type=F32) * in_decay[:, g * 256:(g + 1) * 256]
        for pr in range(2):
            xp = Xdt_b[:, (2 * g + pr) * LANES:(2 * g + pr + 1) * LANES]
            ys = []
            for hh in range(2):
                h = 4 * g + 2 * pr + hh
                seg = acs128[:, h * LANES:(h + 1) * LANES] - acsT[h:h + 1, :]
                lmat = jnp.exp(jnp.where(causal, seg, -jnp.inf))
                w = (cb * lmat).astype(BF16)
                ys.append(jnp.dot(w, xp, preferred_element_type=F32))
            y_parts.append(jnp.where(low_half, ys[0], ys[1]) + y_off[:, pr * LANES:(pr + 1) * LANES])
        states = jnp.dot(Bg.T.astype(BF16), Xd[:, g * 256:(g + 1) * 256], preferred_element_type=F32)
        hst_ref[g] = hprev * chunk_decay[:, g * 256:(g + 1) * 256] + states

    z = z_ref[...].astype(F32)
    gate = _silu(z)
    for g in range(2):
        yg = jnp.concatenate(y_parts[2 * g:2 * g + 2], axis=1) + dsk_ref[:, g * 256:(g + 1) * 256] * X[:, g * 256:(g + 1) * 256]
        yg = yg * gate[:, g * 256:(g + 1) * 256]
        ms = jnp.mean(yg * yg, axis=-1, keepdims=True)
        y_ref[:, g * 256:(g + 1) * 256] = (yg * lax.rsqrt(ms + LN_EPS) * ng_ref[:, g * 256:(g + 1) * 256]).astype(BF16)


def _ssd_constants():
    e64 = np.zeros((LANES, 512), np.float32)
    e128 = np.zeros((LANES, 1024), np.float32)
    for h in range(SSD_HEADS):
        e64[h, 64 * h:64 * (h + 1)] = 1.0
        e128[h, 128 * h:128 * (h + 1)] = 1.0
    tri = np.tril(np.ones((SSD_L, SSD_L), np.float32))
    return (jnp.asarray(e64, BF16), jnp.asarray(e128, BF16), jnp.asarray(tri, BF16), jnp.asarray(tri.T, BF16))


def _ssd_mixer(h, dt_raw, conv_w, conv_b, dt_bias, a_log, d_skip, norm_g, batch, seq):
    t = h.shape[0]
    L = SSD_L
    nc = seq // L
    e64, e128, tri, triT = _ssd_constants()
    pad_l = lambda v: jnp.pad(v.reshape(1, SSD_HEADS), ((0, 0), (0, LANES - SSD_HEADS)))
    pad_t = lambda v: jnp.pad(jnp.broadcast_to(v.reshape(SSD_HEADS, 1), (SSD_HEADS, L)), ((0, 16 - SSD_HEADS), (0, 0)))
    rb = lambda b, c: b * nc + c
    const = lambda shape: pl.BlockSpec(shape, lambda b, c: (0,) * len(shape))
    return pl.pallas_call(
        _ssd_kernel,
        out_shape=jax.ShapeDtypeStruct((t, GROUP_W), BF16),
        grid=(batch, nc),
        in_specs=[pl.BlockSpec((L, SSD_XBC), lambda b, c: (rb(b, c), COL_XBC // 2)),
                  pl.BlockSpec((L, GROUP_W), lambda b, c: (rb(b, c), COL_Z)),
                  pl.BlockSpec((L, LANES), lambda b, c: (rb(b, c), 0)),
                  const((SSD_CONV, SSD_XBC)), const((1, SSD_XBC)),
                  const((1, LANES)), const((1, LANES)), const((16, L)), const((16, L)),
                  const((1, GROUP_W)), const((1, GROUP_W)),
                  const((LANES, 512)), const((LANES, 1024)), const((L, L)), const((L, L))],
        out_specs=pl.BlockSpec((L, GROUP_W), lambda b, c: (rb(b, c), 0)),
        scratch_shapes=[pltpu.VMEM((SUBLANES + L, SSD_XBC), F32),
                        pltpu.VMEM((2, SSD_STATE, 256), F32)],
        compiler_params=_cparams("arbitrary", "arbitrary"),
        name="ssd_mixer",
    )(h, h, dt_raw, conv_w, conv_b.reshape(1, SSD_XBC),
      pad_l(dt_bias), pad_l(a_log), pad_t(dt_bias), pad_t(a_log),
      jnp.repeat(d_skip, 64).reshape(1, GROUP_W), norm_g.reshape(1, GROUP_W), e64, e128, tri, triT)


def _attn_kernel(q_ref, k_ref, v_ref, bias_ref, lq1_ref, lk1_ref, lq2_ref, lk2_ref, g_ref, o_ref,
                 m_ref, l_ref, acc_ref, *, lambda_init):
    qi = pl.program_id(2)
    tq = q_ref.shape[0]
    tk = tq
    lane = lax.broadcasted_iota(I32, (tq, LANES), 1)
    qs = []
    for hh in range(ATTN_HEADS):
        q = q_ref[:, hh * LANES:(hh + 1) * LANES]
        zero = jnp.zeros_like(q)
        qs += [jnp.where(lane < DIFF_QK_DIM, q, zero), jnp.where(lane >= DIFF_QK_DIM, q, zero)]
    m_ref[...] = jnp.full(m_ref.shape, -jnp.inf, F32)
    l_ref[...] = jnp.zeros(l_ref.shape, F32)
    acc_ref[...] = jnp.zeros(acc_ref.shape, F32)

    def tile(off, bias):
        for st in range(2 * ATTN_HEADS):
            hh = st // 2
            k = k_ref[pl.ds(off, tk), hh * LANES:(hh + 1) * LANES]
            v = v_ref[pl.ds(off, tk), hh * LANES:(hh + 1) * LANES]
            s = _dot_nt(qs[st], k)
            if bias is not None:
                s = s + bias
            m_prev = m_ref[st]
            m_new = jnp.maximum(m_prev, jnp.max(s, axis=-1, keepdims=True))
            alpha = jnp.exp2(m_prev - m_new)
            p = jnp.exp2(s - jnp.concatenate([m_new] * (tk // LANES), axis=1))
            psum = p[:, 0:LANES]
            for c in range(1, tk // LANES):
                psum = psum + p[:, c * LANES:(c + 1) * LANES]
            l_ref[st] = alpha * l_ref[st] + psum
            acc_ref[st] = alpha * acc_ref[st] + jnp.dot(p.astype(BF16), v, preferred_element_type=F32)
            m_ref[st] = m_new

    def body(j, carry):
        tile(pl.multiple_of(j * tk, tk), None)
        return carry

    lax.fori_loop(0, qi, body, 0)
    tile(pl.multiple_of(qi * tk, tk), bias_ref[...])

    lam = (jnp.exp(jnp.sum(lq1_ref[...] * lk1_ref[...], axis=-1, keepdims=True))
           - jnp.exp(jnp.sum(lq2_ref[...] * lk2_ref[...], axis=-1, keepdims=True)) + lambda_init)
    for hh in range(ATTN_HEADS):
        l0 = jnp.sum(l_ref[2 * hh], axis=-1, keepdims=True)
        l1 = jnp.sum(l_ref[2 * hh + 1], axis=-1, keepdims=True)
        o = acc_ref[2 * hh] / l0 - lam * (acc_ref[2 * hh + 1] / l1)
        ms = jnp.mean(o * o, axis=-1, keepdims=True)
        o_ref[:, hh * LANES:(hh + 1) * LANES] = (o * lax.rsqrt(ms + LN_EPS) * g_ref[...]
                                                 * (1.0 - lambda_init)).astype(BF16)


def _chunk_mask_bias(tq):
    r = np.arange(tq)[:, None] // CHUNK
    c = np.arange(tq)[None, :] // CHUNK
    return jnp.asarray(np.where(c <= r, 0.0, -np.inf), F32)


def _diff_attention(h, lq1, lk1, lq2, lk2, norm_g, lambda_init, batch, seq):
    t = h.shape[0]
    tq = min(512, seq)
    nq = seq // tq
    w = ATTN_HEADS * LANES
    qcol = COL_Q * TN // w
    kcol = COL_K * TN // w
    vcol = COL_V * TN // w
    vec = lambda n: pl.BlockSpec((1, n), lambda b, hd, qi: (0, 0))
    return pl.pallas_call(
        functools.partial(_attn_kernel, lambda_init=lambda_init),
        out_shape=jax.ShapeDtypeStruct((t, GROUP_W), BF16),
        grid=(batch, DIFF_HEADS // ATTN_HEADS, nq),
        in_specs=[pl.BlockSpec((tq, w), lambda b, hd, qi: (b * nq + qi, qcol + hd)),
                  pl.BlockSpec((seq, w), lambda b, hd, qi: (b, kcol + hd)),
                  pl.BlockSpec((seq, w), lambda b, hd, qi: (b, vcol + hd)),
                  pl.BlockSpec((tq, tq), lambda b, hd, qi: (0, 0)),
                  vec(DIFF_QK_DIM), vec(DIFF_QK_DIM), vec(DIFF_QK_DIM), vec(DIFF_QK_DIM), vec(DIFF_V_DIM)],
        out_specs=pl.BlockSpec((tq, w), lambda b, hd, qi: (b * nq + qi, hd)),
        scratch_shapes=[pltpu.VMEM((2 * ATTN_HEADS, tq, LANES), F32), pltpu.VMEM((2 * ATTN_HEADS, tq, LANES), F32),
                        pltpu.VMEM((2 * ATTN_HEADS, tq, DIFF_V_DIM), F32)],
        compiler_params=_cparams("arbitrary", "arbitrary", "arbitrary"),
        name="diff_attn",
    )(h, h, h, _chunk_mask_bias(tq), lq1.reshape(1, -1), lk1.reshape(1, -1), lq2.reshape(1, -1),
      lk2.reshape(1, -1), norm_g.reshape(1, -1))


CONF_HALO = 32
SC_HALO = 16
CONV_ROWS = 64


def _conv_kernel(u_ref, uh_ref, bg_ref, cg_ref, hh_ref, cgh_ref, hhh_ref,
                 dww_ref, dwb_ref, lng_ref, lnb_ref, pww_ref, pwb_ref, scw_ref,
                 yc_ref, yd_ref, hbuf_ref, pbuf_ref, cbuf_ref, sbuf_ref):
    i = pl.program_id(1)
    tm = u_ref.shape[0]

    def glu(u):
        u = u.astype(F32)
        return u[:, 0:GROUP_W] * _sigmoid(u[:, GROUP_W:2 * GROUP_W])

    first = (i == 0)
    hbuf_ref[0:CONF_HALO, :] = jnp.where(first, 0.0, glu(uh_ref[...]))
    hbuf_ref[CONF_HALO:CONF_HALO + tm, :] = glu(u_ref[...])
    pbuf_ref[0:SC_HALO, :] = jnp.where(first, 0.0, cgh_ref[...].astype(F32) * hhh_ref[...].astype(F32))
    pbuf_ref[SC_HALO:SC_HALO + tm, :] = cg_ref[...].astype(F32) * hh_ref[...].astype(F32)

    for b in range(1, SUBLANES):
        sbuf_ref[b - 1, 0:tm + CONF_HALO - SUBLANES, :] = hbuf_ref[pl.ds(b, tm + CONF_HALO - SUBLANES), :]
    for r0 in range(0, tm, CONV_ROWS):
        acc = jnp.zeros((CONV_ROWS, GROUP_W), F32) + dwb_ref[...]
        for k in range(CONF_KERNEL):
            off = CONF_HALO - (CONF_KERNEL - 1) + k
            phase, base = off % SUBLANES, r0 + off - off % SUBLANES
            src = hbuf_ref[pl.ds(base, CONV_ROWS), :] if phase == 0 else sbuf_ref[phase - 1, pl.ds(base, CONV_ROWS), :]
            acc = acc + dww_ref[k:k + 1, :] * src
        cbuf_ref[r0:r0 + CONV_ROWS, :] = acc
    hc = cbuf_ref[...]
    mu = jnp.mean(hc, axis=-1, keepdims=True)
    xc = hc - mu
    var = jnp.mean(xc * xc, axis=-1, keepdims=True)
    hn = _silu(xc * lax.rsqrt(var + LN_EPS) * lng_ref[...] + lnb_ref[...])
    yc = jnp.dot(hn.astype(BF16), pww_ref[...], preferred_element_type=F32) + pwb_ref[...]
    yc_ref[...] = yc.astype(BF16)

    sc = jnp.zeros((tm, GROUP_W), F32)
    for k in range(SC_KERNEL):
        sc = sc + scw_ref[k:k + 1, :] * pbuf_ref[pl.ds(SC_HALO - (SC_KERNEL - 1) + k, tm), :]
    yd_ref[...] = (bg_ref[...].astype(F32) * sc).astype(BF16)


def _conv_mixers(h, dw_w, dw_b, ln_g, ln_b, pw_w, pw_b, sc_w, batch, seq):
    t = h.shape[0]
    tm = min(512, seq)
    nt = seq // tm
    rb = lambda b, i: b * nt + i
    halo = lambda rows: (lambda b, i: jnp.maximum(rb(b, i) * (tm // rows) - 1, 0))
    hc, hs = halo(CONF_HALO), halo(SC_HALO)
    const = lambda shape: pl.BlockSpec(shape, lambda b, i: (0,) * len(shape))
    return pl.pallas_call(
        _conv_kernel,
        out_shape=(jax.ShapeDtypeStruct((t, GROUP_W), BF16), jax.ShapeDtypeStruct((t, GROUP_W), BF16)),
        grid=(batch, nt),
        in_specs=[pl.BlockSpec((tm, 2 * GROUP_W), lambda b, i: (rb(b, i), COL_CONF // 2)),
                  pl.BlockSpec((CONF_HALO, 2 * GROUP_W), lambda b, i: (hc(b, i), COL_CONF // 2)),
                  pl.BlockSpec((tm, GROUP_W), lambda b, i: (rb(b, i), COL_BG)),
                  pl.BlockSpec((tm, GROUP_W), lambda b, i: (rb(b, i), COL_CG)),
                  pl.BlockSpec((tm, GROUP_W), lambda b, i: (rb(b, i), COL_HH)),
                  pl.BlockSpec((SC_HALO, GROUP_W), lambda b, i: (hs(b, i), COL_CG)),
                  pl.BlockSpec((SC_HALO, GROUP_W), lambda b, i: (hs(b, i), COL_HH)),
                  const((CONF_KERNEL, GROUP_W)), const((1, GROUP_W)), const((1, GROUP_W)), const((1, GROUP_W)),
                  const((GROUP_W, GROUP_W)), const((1, GROUP_W)), const((SC_KERNEL, GROUP_W))],
        out_specs=(pl.BlockSpec((tm, GROUP_W), lambda b, i: (rb(b, i), 0)),
                   pl.BlockSpec((tm, GROUP_W), lambda b, i: (rb(b, i), 0))),
        scratch_shapes=[pltpu.VMEM((CONF_HALO + tm, GROUP_W), F32),
                        pltpu.VMEM((SC_HALO + tm, GROUP_W), F32),
                        pltpu.VMEM((tm, GROUP_W), F32),
                        pltpu.VMEM((SUBLANES - 1, CONF_HALO + tm, GROUP_W), F32)],
        compiler_params=_cparams("arbitrary", "arbitrary"),
        name="conv_mixers",
    )(h, h, h, h, h, h, h, dw_w, dw_b.reshape(1, -1), ln_g.reshape(1, -1), ln_b.reshape(1, -1),
      pw_w.astype(BF16), pw_b.reshape(1, -1), sc_w)


def _first_max4(vals):
    m1 = jnp.maximum(jnp.maximum(vals[0], vals[1]), jnp.maximum(vals[2], vals[3]))
    i1 = jnp.where(vals[0] == m1, 0, jnp.where(vals[1] == m1, 1, jnp.where(vals[2] == m1, 2, 3)))
    rest = [jnp.where(i1 == j, -jnp.inf, vals[j]) for j in range(4)]
    m2 = jnp.maximum(jnp.maximum(rest[0], rest[1]), jnp.maximum(rest[2], rest[3]))
    i2 = jnp.where(rest[0] == m2, 0, jnp.where(rest[1] == m2, 1, jnp.where(rest[2] == m2, 2, 3)))
    return m1, i1, m2, i2


def _outproj_kernel(ya_ref, yb_ref, yc_ref, yd_ref, x_ref, w_ref, g_ref, b_ref, x1_ref):
    mix = jnp.dot(ya_ref[...], w_ref[0:GROUP_W, :], preferred_element_type=F32)
    mix = mix + jnp.dot(yb_ref[...], w_ref[GROUP_W:2 * GROUP_W, :], preferred_element_type=F32)
    mix = mix + jnp.dot(yc_ref[...], w_ref[2 * GROUP_W:3 * GROUP_W, :], preferred_element_type=F32)
    mix = mix + jnp.dot(yd_ref[...], w_ref[3 * GROUP_W:4 * GROUP_W, :], preferred_element_type=F32)
    y = ALPHA * x_ref[...] + mix
    mu = jnp.mean(y, axis=-1, keepdims=True)
    yc = y - mu
    var = jnp.mean(yc * yc, axis=-1, keepdims=True)
    x1_ref[...] = yc * lax.rsqrt(var + LN_EPS) * g_ref[...] + b_ref[...]


def _router_kernel(x_ref, w2_ref, w1_ref, rb_ref, eid_ref, gate_ref):
    tm = x_ref.shape[0]
    x = x_ref[...]
    xh = x.astype(BF16)
    xm = (x - xh.astype(F32)).astype(BF16)
    r = (jnp.dot(xh, w2_ref[...], preferred_element_type=F32)
         + jnp.dot(xm, w1_ref[...], preferred_element_type=F32))
    r = r + pltpu.roll(r, LANES - N_EXPERTS, 1)
    logits = r.T[0:N_EXPERTS, :]
    aff = _sigmoid(logits)
    sel = aff + jnp.concatenate([rb_ref[...]] * (tm // LANES), axis=1)
    rows = [sel[e:e + 1, :] for e in range(N_EXPERTS)]
    arow = [aff[e:e + 1, :] for e in range(N_EXPERTS)]
    tops = [_first_max4(rows[4 * g:4 * g + 4]) for g in range(N_GROUPS)]
    score = [tp[0] + tp[2] for tp in tops]
    best = jnp.maximum(jnp.maximum(score[0], score[1]), jnp.maximum(score[2], score[3]))
    grp = jnp.where(score[0] == best, 0, jnp.where(score[1] == best, 1, jnp.where(score[2] == best, 2, 3)))
    pick = lambda idx: jnp.where(grp == 0, tops[0][idx], jnp.where(grp == 1, tops[1][idx],
                                 jnp.where(grp == 2, tops[2][idx], tops[3][idx])))
    e1 = grp * EXPERTS_PER_GROUP + pick(1)
    e2 = grp * EXPERTS_PER_GROUP + pick(3)
    a1 = jnp.zeros_like(best)
    a2 = jnp.zeros_like(best)
    for e in range(N_EXPERTS):
        a1 = jnp.where(e1 == e, arow[e], a1)
        a2 = jnp.where(e2 == e, arow[e], a2)
    den = a1 + a2
    zi = jnp.zeros((SUBLANES - 2, tm), I32)
    zf = jnp.zeros((SUBLANES - 2, tm), F32)
    eid_ref[...] = jnp.concatenate([e1.astype(I32), e2.astype(I32), zi], axis=0)
    gate_ref[...] = jnp.concatenate([a1 / den, a2 / den, zf], axis=0)


def _out_projection(ya, yb, yc, yd, x, w_out, ln_g, ln_b, layer):
    t, d = ya.shape[0], x.shape[1]
    tm = min(512, t)
    act = lambda: pl.BlockSpec((tm, GROUP_W), lambda i: (i, 0))
    const = lambda shape: pl.BlockSpec(shape, lambda i: (0,) * len(shape))
    return pl.pallas_call(
        _outproj_kernel,
        out_shape=jax.ShapeDtypeStruct((t, d), F32),
        grid=(t // tm,),
        in_specs=[act(), act(), act(), act(),
                  pl.BlockSpec((tm, d), lambda i: (i, 0)),
                  pl.BlockSpec((None, d, d), lambda i: (layer, 0, 0)), const((1, d)), const((1, d))],
        out_specs=pl.BlockSpec((tm, d), lambda i: (i, 0)),
        compiler_params=_cparams("parallel"),
        name="out_proj",
    )(ya, yb, yc, yd, x, w_out, ln_g.reshape(1, d), ln_b.reshape(1, d))


def _router(x1, router_w, router_bias):
    t, d = x1.shape
    tm = min(512, t)
    w_hi = router_w.astype(BF16)
    w_mid = (router_w - w_hi.astype(F32)).astype(BF16)
    w2 = jnp.concatenate([w_hi, w_mid, jnp.zeros((d, LANES - 2 * N_EXPERTS), BF16)], axis=1)
    w1 = jnp.concatenate([w_hi, jnp.zeros((d, LANES - N_EXPERTS), BF16)], axis=1)
    router_b = jnp.broadcast_to(router_bias.reshape(N_EXPERTS, 1), (N_EXPERTS, LANES))
    const = lambda shape: pl.BlockSpec(shape, lambda i: (0,) * len(shape))
    return pl.pallas_call(
        _router_kernel,
        out_shape=(jax.ShapeDtypeStruct((SUBLANES, t), I32), jax.ShapeDtypeStruct((SUBLANES, t), F32)),
        grid=(t // tm,),
        in_specs=[pl.BlockSpec((tm, d), lambda i: (i, 0)),
                  const((d, LANES)), const((d, LANES)), const((N_EXPERTS, LANES))],
        out_specs=(pl.BlockSpec((SUBLANES, tm), lambda i: (0, i)),
                   pl.BlockSpec((SUBLANES, tm), lambda i: (0, i))),
        compiler_params=_cparams("parallel"),
        name="router",
    )(x1, w2, w1, router_b)


PAIR_A = (0, 0, 0, 1, 1, 3)
PAIR_B = (1, 2, 3, 3, 2, 2)
N_PAIRS = len(PAIR_A)
MOE_STAGES = 6
N_BUCKETS = N_GROUPS * N_PAIRS


def _moe_pair_kernel(ea_ref, eb_ref, nused_ref, nv_ref, src_ref, srcn_ref, dst_ref, dstp_ref, dstpp_ref, gt_ref, x_hbm,
                     wga_ref, wua_ref, wda_ref, wgb_ref, wub_ref, wdb_ref, g_ref, b_ref, o_hbm,
                     xbuf0_ref, xbuf1_ref, obuf0_ref, obuf1_ref, gsem_ref, ssem_ref):
    i = pl.program_id(0)
    nused = nused_ref[0]
    xbufs = (xbuf0_ref, xbuf1_ref)
    obufs = (obuf0_ref, obuf1_ref)

    def gather(rows_ref, s):
        return [pltpu.make_async_copy(x_hbm.at[pl.ds(rows_ref[0, 0, r], 1), :], xbufs[s].at[pl.ds(r, 1), :],
                                      gsem_ref.at[s]) for r in range(MOE_BLOCK)]

    def scatter(rows_ref, s):
        return [pltpu.make_async_copy(obufs[s].at[pl.ds(r, 1), :], o_hbm.at[pl.ds(rows_ref[0, 0, r], 1), :],
                                      ssem_ref.at[s]) for r in range(MOE_BLOCK)]

    def start(copies):
        for r, c in enumerate(copies):
            c.start(priority=r % 2)

    def wait(copies):
        for c in copies:
            c.wait()

    def scatter_rows(rows_ref, s, n, go):
        def one(r):
            return pltpu.make_async_copy(obufs[s].at[pl.ds(r, 1), :], o_hbm.at[pl.ds(rows_ref[0, 0, r], 1), :],
                                         ssem_ref.at[s])

        @pl.when(n == MOE_BLOCK)
        def _():
            (start if go else wait)(scatter(rows_ref, s))

        @pl.when(n < MOE_BLOCK)
        def _():
            def body(r, carry):
                if go:
                    one(r).start()
                else:
                    one(r).wait()
                return carry
            lax.fori_loop(0, n, body, 0)

    def zero_after(v):
        bits = pltpu.bitcast(v[v.shape[0] - SUBLANES:, v.shape[1] - LANES:], I32)
        return ((bits[SUBLANES - 1, LANES - 1] >> 31) + 1) >> 1

    def compute(s, with_scatter):
        per = -(-MOE_BLOCK // MOE_STAGES)

        def issue(k, z):
            for r in range(k * per, min((k + 1) * per, MOE_BLOCK)):
                pltpu.make_async_copy(x_hbm.at[pl.ds(srcn_ref[0, 0, r] + z, 1), :],
                                      xbufs[1 - s].at[pl.ds(r, 1), :], gsem_ref.at[1 - s]).start(priority=r % 2)
                if with_scatter:
                    pltpu.make_async_copy(obufs[1 - s].at[pl.ds(r, 1), :],
                                          o_hbm.at[pl.ds(dstp_ref[0, 0, r] + z, 1), :],
                                          ssem_ref.at[1 - s]).start(priority=(r + 1) % 2)

        x = xbufs[s][...]
        xh = x.astype(BF16)
        gt = gt_ref[...]
        issue(0, 0)
        hg = jnp.dot(xh, wga_ref[...], preferred_element_type=F32)
        issue(1, zero_after(hg))
        hu = jnp.dot(xh, wua_ref[...], preferred_element_type=F32)
        issue(2, zero_after(hu))
        ya = jnp.dot((_silu(hg) * hu).astype(BF16), wda_ref[...], preferred_element_type=F32)
        issue(3, zero_after(ya))
        hg = jnp.dot(xh, wgb_ref[...], preferred_element_type=F32)
        issue(4, zero_after(hg))
        hu = jnp.dot(xh, wub_ref[...], preferred_element_type=F32)
        issue(5, zero_after(hu))
        yb = jnp.dot((_silu(hg) * hu).astype(BF16), wdb_ref[...], preferred_element_type=F32)
        y = ALPHA * x + ya * gt[:, 0:1] + yb * gt[:, 1:2]
        mu = jnp.mean(y, axis=-1, keepdims=True)
        yc = y - mu
        var = jnp.mean(yc * yc, axis=-1, keepdims=True)
        obufs[s][...] = yc * lax.rsqrt(var + LN_EPS) * g_ref[...] + b_ref[...]

    @pl.when(i == 0)
    def _():
        start(gather(src_ref, 0))

    nv_prev = nv_ref[jnp.maximum(i - 1, 0)]
    full_prev = jnp.logical_and(i >= 1, nv_prev == MOE_BLOCK)
    for s in range(2):
        mine = jnp.logical_and(i < nused, i % 2 == s)

        @pl.when(mine)
        def _():
            wait(gather(src_ref, s))

        @pl.when(jnp.logical_and(mine, i >= 2))
        def _():
            scatter_rows(dstpp_ref, s, nv_ref[jnp.maximum(i - 2, 0)], False)

        @pl.when(jnp.logical_and(mine, jnp.logical_and(i >= 1, jnp.logical_not(full_prev))))
        def _():
            scatter_rows(dstp_ref, 1 - s, nv_prev, True)

        @pl.when(jnp.logical_and(mine, full_prev))
        def _():
            compute(s, True)

        @pl.when(jnp.logical_and(mine, jnp.logical_not(full_prev)))
        def _():
            compute(s, False)

        last = jnp.logical_and(mine, i + 1 == nused)

        @pl.when(last)
        def _():
            scatter_rows(dst_ref, s, nv_ref[i], True)
            scatter_rows(dst_ref, s, nv_ref[i], False)
            wait(gather(srcn_ref, 1 - s))

        @pl.when(jnp.logical_and(last, i >= 1))
        def _():
            scatter_rows(dstp_ref, 1 - s, nv_ref[jnp.maximum(i - 1, 0)], False)


def _moe_pairs(x1, tables, gates_sorted, wg, wu, wd, ln_g, ln_b, layer):
    rows, ea, eb, nused, nvalid = tables
    t, d = x1.shape
    n_blk = rows.shape[0]
    rows_spec = lambda off, clip_hi: pl.BlockSpec(
        (1, 1, MOE_BLOCK),
        lambda i, a, b, nu, nv: (jnp.clip(i + off, 0, nu[0] - 1) if clip_hi else jnp.maximum(i + off, 0), 0, 0),
        memory_space=pltpu.SMEM)
    wspec = lambda shape, sel: pl.BlockSpec(
        (None, None) + shape, lambda i, a, b, nu, nv: (layer, (a if sel == 0 else b)[i], 0, 0),
        pipeline_mode=pl.Buffered(1))
    vec = pl.BlockSpec((1, d), lambda i, a, b, nu, nv: (0, 0))
    return pl.pallas_call(
        _moe_pair_kernel,
        out_shape=jax.ShapeDtypeStruct((t, d), F32),
        grid_spec=pltpu.PrefetchScalarGridSpec(
            num_scalar_prefetch=4,
            grid=(n_blk,),
            in_specs=[rows_spec(0, True), rows_spec(1, True), rows_spec(0, True), rows_spec(-1, False),
                      rows_spec(-2, False),
                      pl.BlockSpec((MOE_BLOCK, TOP_K), lambda i, a, b, nu, nv: (i, 0)),
                      pl.BlockSpec(memory_space=pl.ANY),
                      wspec((d, D_EXPERT), 0), wspec((d, D_EXPERT), 0), wspec((D_EXPERT, d), 0),
                      wspec((d, D_EXPERT), 1), wspec((d, D_EXPERT), 1), wspec((D_EXPERT, d), 1),
                      vec, vec],
            out_specs=pl.BlockSpec(memory_space=pl.ANY),
            scratch_shapes=[pltpu.VMEM((MOE_BLOCK, d), F32), pltpu.VMEM((MOE_BLOCK, d), F32),
                            pltpu.VMEM((MOE_BLOCK, d), F32), pltpu.VMEM((MOE_BLOCK, d), F32),
                            pltpu.SemaphoreType.DMA((2,)), pltpu.SemaphoreType.DMA((2,))]),
        compiler_params=_cparams("arbitrary"),
        name="moe_pairs",
    )(ea, eb, nused, nvalid, rows, rows, rows, rows, rows, gates_sorted, x1, wg, wu, wd, wg, wu, wd,
      ln_g.reshape(1, d), ln_b.reshape(1, d))


def _pair_tables(eid, gate, t):
    e1, e2 = eid[0], eid[1]
    grp = e1 // EXPERTS_PER_GROUP
    a, b = e1 % EXPERTS_PER_GROUP, e2 % EXPERTS_PER_GROUP
    pidx = jnp.zeros_like(a)
    a_of_pair = jnp.zeros_like(a)
    for p in range(N_PAIRS):
        hit = jnp.logical_or(jnp.logical_and(a == PAIR_A[p], b == PAIR_B[p]),
                             jnp.logical_and(a == PAIR_B[p], b == PAIR_A[p]))
        pidx = jnp.where(hit, p, pidx)
        a_of_pair = jnp.where(hit, PAIR_A[p], a_of_pair)
    bucket = grp * N_PAIRS + pidx
    in_a = a == a_of_pair
    gate_a = jnp.where(in_a, gate[0], gate[1])
    gate_b = jnp.where(in_a, gate[1], gate[0])

    onehot = (bucket[:, None] == jnp.arange(N_BUCKETS, dtype=I32)[None, :]).astype(I32)
    csum = jnp.cumsum(onehot, axis=0)
    counts = csum[-1]
    padded = ((counts + MOE_BLOCK - 1) // MOE_BLOCK) * MOE_BLOCK
    pends = jnp.cumsum(padded)
    pos = jnp.sum(onehot * (csum - 1 + (pends - padded)[None, :]), axis=1)
    n_blk = -(-t // MOE_BLOCK) + N_BUCKETS
    n_pad = n_blk * MOE_BLOCK
    rows = jnp.zeros((n_pad,), I32).at[pos].set(jnp.arange(t, dtype=I32))
    nused = (pends[-1] // MOE_BLOCK).astype(I32)
    blk = jnp.minimum(jnp.arange(n_blk, dtype=I32), nused - 1)
    blk_bucket = jnp.sum((pends[None, :] <= (blk * MOE_BLOCK)[:, None]).astype(I32), axis=1)
    base = (blk_bucket // N_PAIRS) * EXPERTS_PER_GROUP
    pair = blk_bucket % N_PAIRS
    ea, eb = base, base
    for p in range(N_PAIRS):
        ea = ea + jnp.where(pair == p, PAIR_A[p], 0)
        eb = eb + jnp.where(pair == p, PAIR_B[p], 0)
    ea, eb = ea.astype(I32), eb.astype(I32)
    valid_end = pends - padded + counts
    blk_onehot = (blk_bucket[:, None] == jnp.arange(N_BUCKETS, dtype=I32)[None, :]).astype(I32)
    nvalid = jnp.clip(jnp.sum(blk_onehot * valid_end[None, :], axis=1) - blk * MOE_BLOCK, 0, MOE_BLOCK).astype(I32)
    tables = (rows.reshape(n_blk, 1, MOE_BLOCK), ea, eb, nused.reshape(1), nvalid)
    return tables, jnp.stack([gate_a[rows], gate_b[rows]], axis=1)


def _rope_tables(seq):
    half = DIFF_QK_DIM // 2
    inv = 1.0 / (ROPE_THETA ** (jnp.arange(0, DIFF_QK_DIM, 2, dtype=F32) / DIFF_QK_DIM))
    ang = jnp.arange(seq, dtype=F32)[:, None] * inv[None, :]
    cos, sin = jnp.cos(ang), jnp.sin(ang)
    cos_t = jnp.concatenate([cos, cos, cos, cos], axis=1)
    sin_t = jnp.concatenate([-sin, sin, -sin, sin], axis=1)
    return cos_t, sin_t


W_IN_SPLITS = (0, 512, 1536, 1544, 2056, 2568, 3080, 4104, 5640)


def _wprep_kernel(w_ref, o_ref):
    w = w_ref[...]
    z, xbc, dt, q, k, v, conf, sc = [w[:, a:b] for a, b in zip(W_IN_SPLITS[:-1], W_IN_SPLITS[1:])]
    rows = w.shape[0]
    o_ref[0] = jnp.concatenate([xbc, conf], axis=1).astype(BF16)
    o_ref[1] = jnp.concatenate([sc, z], axis=1).astype(BF16)
    o_ref[2] = jnp.concatenate([q, k, v, dt, jnp.zeros((rows, TN - SSD_HEADS), F32)], axis=1).astype(BF16)


def _prep_w_in(w_all, layer):
    _, d, n = w_all.shape
    rows = 256
    return pl.pallas_call(
        _wprep_kernel,
        out_shape=jax.ShapeDtypeStruct((INPROJ_TILES, d, INPROJ_TN), BF16),
        grid=(d // rows,),
        in_specs=[pl.BlockSpec((None, rows, n), lambda i: (layer, i, 0))],
        out_specs=pl.BlockSpec((INPROJ_TILES, rows, INPROJ_TN), lambda i: (0, i, 0)),
        compiler_params=_cparams("parallel"),
        name="w_in_prep",
    )(w_all)


def kernel(x, ln_in_g, ln_in_b, w_in, ssd_conv_w, ssd_conv_b, ssd_dt_bias, ssd_a_log, ssd_d, ssd_norm_g, diff_lq1, diff_lk1, diff_lq2, diff_lk2, diff_norm_g, conf_dw_w, conf_dw_b, conf_ln_g, conf_ln_b, conf_pw_w, conf_pw_b, sc_conv_w, w_out, ln1_g, ln1_b, router_w, router_bias, moe_w_gate, moe_w_up, moe_w_down, ln2_g, ln2_b):
    b, s, d = x.shape
    t = b * s
    cos_t, sin_t = _rope_tables(s)
    xf = _layer_norm(x.reshape(t, d), ln_in_g, ln_in_b)
    w_out_b = w_out.astype(BF16)
    wg_b, wu_b, wd_b = moe_w_gate.astype(BF16), moe_w_up.astype(BF16), moe_w_down.astype(BF16)
    for l in range(DEPTH):
        lambda_init = 0.8 - 0.6 * math.exp(-0.3 * l)
        h, dt_raw = _in_projection(xf, _prep_w_in(w_in, l), cos_t, sin_t, s, t)
        ya = _ssd_mixer(h, dt_raw, ssd_conv_w[l], ssd_conv_b[l], ssd_dt_bias[l], ssd_a_log[l],
                        ssd_d[l], ssd_norm_g[l], b, s)
        yb = _diff_attention(h, diff_lq1[l], diff_lk1[l], diff_lq2[l], diff_lk2[l], diff_norm_g[l],
                             lambda_init, b, s)
        yc, yd = _conv_mixers(h, conf_dw_w[l], conf_dw_b[l], conf_ln_g[l], conf_ln_b[l], conf_pw_w[l],
                              conf_pw_b[l], sc_conv_w[l], b, s)
        x1 = _out_projection(ya, yb, yc, yd, xf, w_out_b, ln1_g[l], ln1_b[l], l)
        eid, gate = _router(x1, router_w, router_bias)
        tables, gates_sorted = _pair_tables(eid, gate, t)
        xf = _moe_pairs(x1, tables, gates_sorted, wg_b, wu_b, wd_b, ln2_g[l], ln2_b[l], l)
    return xf.reshape(b, s, d)
```

```python
import functools
import math

import numpy as np
import jax
import jax.numpy as jnp
from jax import lax
from jax.experimental import pallas as pl
from jax.experimental.pallas import tpu as pltpu

F32 = jnp.float32
BF16 = jnp.bfloat16
I32 = jnp.int32

D_MODEL = 2048
DEPTH = 2
CHUNK = 64
GROUP_W = 512
SSD_HEADS = 8
SSD_STATE = 128
SSD_CONV = 4
SSD_XBC = 1024
DIFF_HEADS = 4
DIFF_QK_DIM = 64
DIFF_V_DIM = 128
ROPE_THETA = 10000.0
CONF_KERNEL = 31
SC_KERNEL = 3
N_EXPERTS = 16
N_GROUPS = 4
EXPERTS_PER_GROUP = 4
TOP_K = 2
D_EXPERT = 1024
MOE_BLOCK = 256
ALPHA = (2 * DEPTH) ** 0.25
LN_EPS = 1e-5
LOG2E = 1.4426950408889634

LANES = 128
SUBLANES = 8
VMEM_LIMIT_BYTES = 56 * 1024 * 1024

COL_XBC, COL_CONF, COL_BG, COL_CG, COL_HH, COL_Z, COL_Q, COL_K, COL_V, COL_DT = 0, 2, 4, 5, 6, 7, 8, 9, 10, 11
H_COLS = 12 * 512
TN = 512
INPROJ_TN = 2048
INPROJ_TILES = H_COLS // INPROJ_TN
SSD_L = 128
ATTN_HEADS = 4


def _cparams(*sem):
    return pltpu.CompilerParams(dimension_semantics=tuple(sem), vmem_limit_bytes=VMEM_LIMIT_BYTES)


def _sigmoid(x):
    return 1.0 / (1.0 + jnp.exp(-x))


def _silu(x):
    return x * _sigmoid(x)


def _softplus(x):
    return jnp.maximum(x, 0.0) + jnp.log(1.0 + jnp.exp(-jnp.abs(x)))


def _split3(v):
    hi = v.astype(BF16)
    r = v - hi.astype(F32)
    mid = r.astype(BF16)
    lo = (r - mid.astype(F32)).astype(BF16)
    return hi, mid, lo


def _dot_exact_rhs(v, m):
    return sum(jnp.dot(p, m, preferred_element_type=F32) for p in _split3(v))


def _dot_exact_lhs(m, v):
    return sum(jnp.dot(m, p, preferred_element_type=F32) for p in _split3(v))


def _dot_nt(a, b):
    return lax.dot_general(a, b, (((1,), (1,)), ((), ())), preferred_element_type=F32)


def _ln_kernel(x_ref, g_ref, b_ref, o_ref):
    x = x_ref[...]
    mu = jnp.mean(x, axis=-1, keepdims=True)
    xc = x - mu
    var = jnp.mean(xc * xc, axis=-1, keepdims=True)
    o_ref[...] = xc * lax.rsqrt(var + LN_EPS) * g_ref[...] + b_ref[...]


def _layer_norm(x, g, b):
    t, d = x.shape
    tm = min(512, t)
    return pl.pallas_call(
        _ln_kernel,
        out_shape=jax.ShapeDtypeStruct((t, d), F32),
        grid=(t // tm,),
        in_specs=[pl.BlockSpec((tm, d), lambda i: (i, 0)),
                  pl.BlockSpec((1, d), lambda i: (0, 0)),
                  pl.BlockSpec((1, d), lambda i: (0, 0))],
        out_specs=pl.BlockSpec((tm, d), lambda i: (i, 0)),
        compiler_params=_cparams("parallel"),
        name="entry_ln",
    )(x, g.reshape(1, d), b.reshape(1, d))


def _inproj_kernel(x_ref, w_ref, cos_ref, sin_ref, h_ref, dt_ref, xb_ref):
    j = pl.program_id(1)
    tm = x_ref.shape[0]

    @pl.when(j == 0)
    def _():
        xb_ref[...] = x_ref[...].astype(BF16)

    x = xb_ref[...]
    last = INPROJ_TILES - 1

    def rope_store(c, acc, scale):
        cs = cos_ref[...] * scale
        sn = sin_ref[...] * scale
        lane = lax.broadcasted_iota(I32, (tm, LANES), 1)
        first_half = (lane % DIFF_QK_DIM) < (DIFF_QK_DIM // 2)
        for hh in range(TN // LANES):
            a = acc[:, hh * LANES:(hh + 1) * LANES]
            rot = jnp.where(first_half, pltpu.roll(a, LANES - 32, 1), pltpu.roll(a, 32, 1))
            h_ref[:, c * TN + hh * LANES:c * TN + (hh + 1) * LANES] = (a * cs + rot * sn).astype(BF16)

    for c in range(INPROJ_TN // TN):
        acc = jnp.dot(x, w_ref[:, c * TN:(c + 1) * TN], preferred_element_type=F32)
        if c * TN == (COL_Q * TN) % INPROJ_TN or c * TN == (COL_K * TN) % INPROJ_TN:
            scale = LOG2E * DIFF_QK_DIM ** -0.5 if c * TN == (COL_Q * TN) % INPROJ_TN else 1.0

            @pl.when(j == last)
            def _():
                rope_store(c, acc, scale)

            @pl.when(j != last)
            def _():
                h_ref[:, c * TN:(c + 1) * TN] = acc.astype(BF16)
        else:
            h_ref[:, c * TN:(c + 1) * TN] = acc.astype(BF16)
        if c * TN == (COL_DT * TN) % INPROJ_TN:
            @pl.when(j == last)
            def _():
                dt_ref[...] = acc[:, 0:LANES]


def _in_projection(x, w_tiles, cos_t, sin_t, seq, t):
    d = x.shape[1]
    tm = min(1024, seq)
    nseq = seq // tm
    return pl.pallas_call(
        _inproj_kernel,
        out_shape=(jax.ShapeDtypeStruct((t, H_COLS), BF16),
                   jax.ShapeDtypeStruct((t, LANES), F32)),
        grid=(t // tm, INPROJ_TILES),
        in_specs=[pl.BlockSpec((tm, d), lambda i, j: (i, 0)),
                  pl.BlockSpec((None, d, INPROJ_TN), lambda i, j: (j, 0, 0)),
                  pl.BlockSpec((tm, LANES), lambda i, j: (i % nseq, 0)),
                  pl.BlockSpec((tm, LANES), lambda i, j: (i % nseq, 0))],
        out_specs=(pl.BlockSpec((tm, INPROJ_TN), lambda i, j: (i, j)),
                   pl.BlockSpec((tm, LANES), lambda i, j: (i, 0))),
        scratch_shapes=[pltpu.VMEM((tm, d), BF16)],
        compiler_params=_cparams("arbitrary", "arbitrary"),
        name="in_proj",
    )(x, w_tiles, cos_t, sin_t)


def _ssd_kernel(xbc_ref, z_ref, dt_ref, cw_ref, cb_ref, dtb_ref, alog_ref, dtbT_ref, alogT_ref,
                dsk_ref, ng_ref, e64_ref, e128_ref, tri_ref, triT_ref, y_ref, xpad_ref, hst_ref):
    c = pl.program_id(1)
    L = SSD_L

    @pl.when(c == 0)
    def _():
        xpad_ref[0:SUBLANES, :] = jnp.zeros((SUBLANES, SSD_XBC), F32)
        hst_ref[...] = jnp.zeros(hst_ref.shape, F32)

    cur = xbc_ref[...].astype(F32)
    xpad_ref[SUBLANES:SUBLANES + L, :] = cur
    conv = cb_ref[...]
    for k in range(SSD_CONV):
        conv = conv + cw_ref[k:k + 1, :] * xpad_ref[pl.ds(SUBLANES - (SSD_CONV - 1) + k, L), :]
    xpad_ref[0:SUBLANES, :] = cur[L - SUBLANES:L, :]
    xbc = _silu(conv)
    X = xbc[:, 0:GROUP_W]
    Bm = xbc[:, GROUP_W:GROUP_W + 2 * SSD_STATE]
    Cm = xbc[:, GROUP_W + 2 * SSD_STATE:SSD_XBC]

    dt_raw = dt_ref[...]
    dt = _softplus(dt_raw + dtb_ref[...])
    acs = _dot_exact_lhs(tri_ref[...], dt * (-jnp.exp(alog_ref[...])))
    dtT = _softplus(dt_raw.T[0:16, :] + dtbT_ref[...])
    acsT = _dot_exact_rhs(dtT * (-jnp.exp(alogT_ref[...])), triT_ref[...])
    dt64 = _dot_exact_rhs(dt, e64_ref[...])
    acs64 = _dot_exact_rhs(acs, e64_ref[...])
    acs128 = _dot_exact_rhs(acs, e128_ref[...])

    last = acs64[L - 1:L, :]
    in_decay = jnp.exp(acs64)
    decay = jnp.exp(last - acs64)
    chunk_decay = jnp.exp(last)
    Xdt = X * dt64
    Xd = (Xdt * decay).astype(BF16)
    Xdt_b = Xdt.astype(BF16)

    row = lax.broadcasted_iota(I32, (L, L), 0)
    col = lax.broadcasted_iota(I32, (L, L), 1)
    causal = row >= col
    lane = lax.broadcasted_iota(I32, (L, LANES), 1)
    low_half = lane < 64

    y_parts = []
    for g in range(2):
        Bg = Bm[:, g * SSD_STATE:(g + 1) * SSD_STATE]
        Cg = Cm[:, g * SSD_STATE:(g + 1) * SSD_STATE].astype(BF16)
        cb = _dot_nt(Cg, Bg.astype(BF16))
        hprev = hst_ref[g]
        y_off = jnp.dot(Cg, hprev.astype(BF16), preferred_element_type=F32) * in_decay[:, g * 256:(g + 1) * 256]
        for pr in range(2):
            xp = Xdt_b[:, (2 * g + pr) * LANES:(2 * g + pr + 1) * LANES]
            ys = []
            for hh in range(2):
                h = 4 * g + 2 * pr + hh
                seg = acs128[:, h * LANES:(h + 1) * LANES] - acsT[h:h + 1, :]
                lmat = jnp.exp(jnp.where(causal, seg, -jnp.inf))
                w = (cb * lmat).astype(BF16)
                ys.append(jnp.dot(w, xp, preferred_element_type=F32))
            y_parts.append(jnp.where(low_half, ys[0], ys[1]) + y_off[:, pr * LANES:(pr + 1) * LANES])
        states = jnp.dot(Bg.T.astype(BF16), Xd[:, g * 256:(g + 1) * 256], preferred_element_type=F32)
        hst_ref[g] = hprev * chunk_decay[:, g * 256:(g + 1) * 256] + states

    z = z_ref[...].astype(F32)
    gate = _silu(z)
    for g in range(2):
        yg = jnp.concatenate(y_parts[2 * g:2 * g + 2], axis=1) + dsk_ref[:, g * 256:(g + 1) * 256] * X[:, g * 256:(g + 1) * 256]
        yg = yg * gate[:, g * 256:(g + 1) * 256]
        ms = jnp.mean(yg * yg, axis=-1, keepdims=True)
        y_ref[:, g * 256:(g + 1) * 256] = (yg * lax.rsqrt(ms + LN_EPS) * ng_ref[:, g * 256:(g + 1) * 256]).astype(BF16)


def _ssd_constants():
    e64 = np.zeros((LANES, 512), np.float32)
    e128 = np.zeros((LANES, 1024), np.float32)
    for h in range(SSD_HEADS):
        e64[h, 64 * h:64 * (h + 1)] = 1.0
        e128[h, 128 * h:128 * (h + 1)] = 1.0
    tri = np.tril(np.ones((SSD_L, SSD_L), np.float32))
    return (jnp.asarray(e64, BF16), jnp.asarray(e128, BF16), jnp.asarray(tri, BF16), jnp.asarray(tri.T, BF16))


def _ssd_mixer(h, dt_raw, conv_w, conv_b, dt_bias, a_log, d_skip, norm_g, batch, seq):
    t = h.shape[0]
    L = SSD_L
    nc = seq // L
    e64, e128, tri, triT = _ssd_constants()
    pad_l = lambda v: jnp.pad(v.reshape(1, SSD_HEADS), ((0, 0), (0, LANES - SSD_HEADS)))
    pad_t = lambda v: jnp.pad(jnp.broadcast_to(v.reshape(SSD_HEADS, 1), (SSD_HEADS, L)), ((0, 16 - SSD_HEADS), (0, 0)))
    rb = lambda b, c: b * nc + c
    const = lambda shape: pl.BlockSpec(shape, lambda b, c: (0,) * len(shape))
    return pl.pallas_call(
        _ssd_kernel,
        out_shape=jax.ShapeDtypeStruct((t, GROUP_W), BF16),
        grid=(batch, nc),
        in_specs=[pl.BlockSpec((L, SSD_XBC), lambda b, c: (rb(b, c), COL_XBC // 2)),
                  pl.BlockSpec((L, GROUP_W), lambda b, c: (rb(b, c), COL_Z)),
                  pl.BlockSpec((L, LANES), lambda b, c: (rb(b, c), 0)),
                  const((SSD_CONV, SSD_XBC)), const((1, SSD_XBC)),
                  const((1, LANES)), const((1, LANES)), const((16, L)), const((16, L)),
                  const((1, GROUP_W)), const((1, GROUP_W)),
                  const((LANES, 512)), const((LANES, 1024)), const((L, L)), const((L, L))],
        out_specs=pl.BlockSpec((L, GROUP_W), lambda b, c: (rb(b, c), 0)),
        scratch_shapes=[pltpu.VMEM((SUBLANES + L, SSD_XBC), F32),
                        pltpu.VMEM((2, SSD_STATE, 256), F32)],
        compiler_params=_cparams("arbitrary", "arbitrary"),
        name="ssd_mixer",
    )(h, h, dt_raw, conv_w, conv_b.reshape(1, SSD_XBC),
      pad_l(dt_bias), pad_l(a_log), pad_t(dt_bias), pad_t(a_log),
      jnp.repeat(d_skip, 64).reshape(1, GROUP_W), norm_g.reshape(1, GROUP_W), e64, e128, tri, triT)


def _attn_kernel(q_ref, k_ref, v_ref, bias_ref, lq1_ref, lk1_ref, lq2_ref, lk2_ref, g_ref, o_ref,
                 m_ref, l_ref, acc_ref, *, lambda_init):
    qi = pl.program_id(2)
    tq = q_ref.shape[0]
    tk = tq
    lane = lax.broadcasted_iota(I32, (tq, LANES), 1)
    qs = []
    for hh in range(ATTN_HEADS):
        q = q_ref[:, hh * LANES:(hh + 1) * LANES]
        zero = jnp.zeros_like(q)
        qs += [jnp.where(lane < DIFF_QK_DIM, q, zero), jnp.where(lane >= DIFF_QK_DIM, q, zero)]
    m_ref[...] = jnp.full(m_ref.shape, -jnp.inf, F32)
    l_ref[...] = jnp.zeros(l_ref.shape, F32)
    acc_ref[...] = jnp.zeros(acc_ref.shape, F32)

    def tile(off, bias):
        for st in range(2 * ATTN_HEADS):
            hh = st // 2
            k = k_ref[pl.ds(off, tk), hh * LANES:(hh + 1) * LANES]
            v = v_ref[pl.ds(off, tk), hh * LANES:(hh + 1) * LANES]
            s = _dot_nt(qs[st], k)
            if bias is not None:
                s = s + bias
            m_prev = m_ref[st]
            m_new = jnp.maximum(m_prev, jnp.max(s, axis=-1, keepdims=True))
            alpha = jnp.exp2(m_prev - m_new)
            p = jnp.exp2(s - jnp.concatenate([m_new] * (tk // LANES), axis=1))
            psum = p[:, 0:LANES]
            for c in range(1, tk // LANES):
                psum = psum + p[:, c * LANES:(c + 1) * LANES]
            l_ref[st] = alpha * l_ref[st] + psum
            acc_ref[st] = alpha * acc_ref[st] + jnp.dot(p.astype(BF16), v, preferred_element_type=F32)
            m_ref[st] = m_new

    def body(j, carry):
        tile(pl.multiple_of(j * tk, tk), None)
        return carry

    lax.fori_loop(0, qi, body, 0)
    tile(pl.multiple_of(qi * tk, tk), bias_ref[...])

    lam = (jnp.exp(jnp.sum(lq1_ref[...] * lk1_ref[...], axis=-1, keepdims=True))
           - jnp.exp(jnp.sum(lq2_ref[...] * lk2_ref[...], axis=-1, keepdims=True)) + lambda_init)
    for hh in range(ATTN_HEADS):
        l0 = jnp.sum(l_ref[2 * hh], axis=-1, keepdims=True)
        l1 = jnp.sum(l_ref[2 * hh + 1], axis=-1, keepdims=True)
        o = acc_ref[2 * hh] / l0 - lam * (acc_ref[2 * hh + 1] / l1)
        ms = jnp.mean(o * o, axis=-1, keepdims=True)
        o_ref[:, hh * LANES:(hh + 1) * LANES] = (o * lax.rsqrt(ms + LN_EPS) * g_ref[...]
                                                 * (1.0 - lambda_init)).astype(BF16)


def _chunk_mask_bias(tq):
    r = np.arange(tq)[:, None] // CHUNK
    c = np.arange(tq)[None, :] // CHUNK
    return jnp.asarray(np.where(c <= r, 0.0, -np.inf), F32)


def _diff_attention(h, lq1, lk1, lq2, lk2, norm_g, lambda_init, batch, seq):
    t = h.shape[0]
    tq = min(512, seq)
    nq = seq // tq
    w = ATTN_HEADS * LANES
    qcol = COL_Q * TN // w
    kcol = COL_K * TN // w
    vcol = COL_V * TN // w
    vec = lambda n: pl.BlockSpec((1, n), lambda b, hd, qi: (0, 0))
    return pl.pallas_call(
        functools.partial(_attn_kernel, lambda_init=lambda_init),
        out_shape=jax.ShapeDtypeStruct((t, GROUP_W), BF16),
        grid=(batch, DIFF_HEADS // ATTN_HEADS, nq),
        in_specs=[pl.BlockSpec((tq, w), lambda b, hd, qi: (b * nq + qi, qcol + hd)),
                  pl.BlockSpec((seq, w), lambda b, hd, qi: (b, kcol + hd)),
                  pl.BlockSpec((seq, w), lambda b, hd, qi: (b, vcol + hd)),
                  pl.BlockSpec((tq, tq), lambda b, hd, qi: (0, 0)),
                  vec(DIFF_QK_DIM), vec(DIFF_QK_DIM), vec(DIFF_QK_DIM), vec(DIFF_QK_DIM), vec(DIFF_V_DIM)],
        out_specs=pl.BlockSpec((tq, w), lambda b, hd, qi: (b * nq + qi, hd)),
        scratch_shapes=[pltpu.VMEM((2 * ATTN_HEADS, tq, LANES), F32), pltpu.VMEM((2 * ATTN_HEADS, tq, LANES), F32),
                        pltpu.VMEM((2 * ATTN_HEADS, tq, DIFF_V_DIM), F32)],
        compiler_params=_cparams("arbitrary", "arbitrary", "arbitrary"),
        name="diff_attn",
    )(h, h, h, _chunk_mask_bias(tq), lq1.reshape(1, -1), lk1.reshape(1, -1), lq2.reshape(1, -1),
      lk2.reshape(1, -1), norm_g.reshape(1, -1))


CONF_HALO = 32
SC_HALO = 16
CONV_ROWS = 64


def _conv_kernel(u_ref, uh_ref, bg_ref, cg_ref, hh_ref, cgh_ref, hhh_ref,
                 dww_ref, dwb_ref, lng_ref, lnb_ref, pww_ref, pwb_ref, scw_ref,
                 yc_ref, yd_ref, hbuf_ref, pbuf_ref, cbuf_ref, sbuf_ref):
    i = pl.program_id(1)
    tm = u_ref.shape[0]

    def glu(u):
        u = u.astype(F32)
        return u[:, 0:GROUP_W] * _sigmoid(u[:, GROUP_W:2 * GROUP_W])

    first = (i == 0)
    hbuf_ref[0:CONF_HALO, :] = jnp.where(first, 0.0, glu(uh_ref[...]))
    hbuf_ref[CONF_HALO:CONF_HALO + tm, :] = glu(u_ref[...])
    pbuf_ref[0:SC_HALO, :] = jnp.where(first, 0.0, cgh_ref[...].astype(F32) * hhh_ref[...].astype(F32))
    pbuf_ref[SC_HALO:SC_HALO + tm, :] = cg_ref[...].astype(F32) * hh_ref[...].astype(F32)

    for b in range(1, SUBLANES):
        sbuf_ref[b - 1, 0:tm + CONF_HALO - SUBLANES, :] = hbuf_ref[pl.ds(b, tm + CONF_HALO - SUBLANES), :]
    for r0 in range(0, tm, CONV_ROWS):
        acc = jnp.zeros((CONV_ROWS, GROUP_W), F32) + dwb_ref[...]
        for k in range(CONF_KERNEL):
            off = CONF_HALO - (CONF_KERNEL - 1) + k
            phase, base = off % SUBLANES, r0 + off - off % SUBLANES
            src = hbuf_ref[pl.ds(base, CONV_ROWS), :] if phase == 0 else sbuf_ref[phase - 1, pl.ds(base, CONV_ROWS), :]
            acc = acc + dww_ref[k:k + 1, :] * src
        cbuf_ref[r0:r0 + CONV_ROWS, :] = acc
    hc = cbuf_ref[...]
    mu = jnp.mean(hc, axis=-1, keepdims=True)
    xc = hc - mu
    var = jnp.mean(xc * xc, axis=-1, keepdims=True)
    hn = _silu(xc * lax.rsqrt(var + LN_EPS) * lng_ref[...] + lnb_ref[...])
    yc = jnp.dot(hn.astype(BF16), pww_ref[...], preferred_element_type=F32) + pwb_ref[...]
    yc_ref[...] = yc.astype(BF16)

    sc = jnp.zeros((tm, GROUP_W), F32)
    for k in range(SC_KERNEL):
        sc = sc + scw_ref[k:k + 1, :] * pbuf_ref[pl.ds(SC_HALO - (SC_KERNEL - 1) + k, tm), :]
    yd_ref[...] = (bg_ref[...].astype(F32) * sc).astype(BF16)


def _conv_mixers(h, dw_w, dw_b, ln_g, ln_b, pw_w, pw_b, sc_w, batch, seq):
    t = h.shape[0]
    tm = min(512, seq)
    nt = seq // tm
    rb = lambda b, i: b * nt + i
    halo = lambda rows: (lambda b, i: jnp.maximum(rb(b, i) * (tm // rows) - 1, 0))
    hc, hs = halo(CONF_HALO), halo(SC_HALO)
    const = lambda shape: pl.BlockSpec(shape, lambda b, i: (0,) * len(shape))
    return pl.pallas_call(
        _conv_kernel,
        out_shape=(jax.ShapeDtypeStruct((t, GROUP_W), BF16), jax.ShapeDtypeStruct((t, GROUP_W), BF16)),
        grid=(batch, nt),
        in_specs=[pl.BlockSpec((tm, 2 * GROUP_W), lambda b, i: (rb(b, i), COL_CONF // 2)),
                  pl.BlockSpec((CONF_HALO, 2 * GROUP_W), lambda b, i: (hc(b, i), COL_CONF // 2)),
                  pl.BlockSpec((tm, GROUP_W), lambda b, i: (rb(b, i), COL_BG)),
                  pl.BlockSpec((tm, GROUP_W), lambda b, i: (rb(b, i), COL_CG)),
                  pl.BlockSpec((tm, GROUP_W), lambda b, i: (rb(b, i), COL_HH)),
                  pl.BlockSpec((SC_HALO, GROUP_W), lambda b, i: (hs(b, i), COL_CG)),
                  pl.BlockSpec((SC_HALO, GROUP_W), lambda b, i: (hs(b, i), COL_HH)),
                  const((CONF_KERNEL, GROUP_W)), const((1, GROUP_W)), const((1, GROUP_W)), const((1, GROUP_W)),
                  const((GROUP_W, GROUP_W)), const((1, GROUP_W)), const((SC_KERNEL, GROUP_W))],
        out_specs=(pl.BlockSpec((tm, GROUP_W), lambda b, i: (rb(b, i), 0)),
                   pl.BlockSpec((tm, GROUP_W), lambda b, i: (rb(b, i), 0))),
        scratch_shapes=[pltpu.VMEM((CONF_HALO + tm, GROUP_W), F32),
                        pltpu.VMEM((SC_HALO + tm, GROUP_W), F32),
                        pltpu.VMEM((tm, GROUP_W), F32),
                        pltpu.VMEM((SUBLANES - 1, CONF_HALO + tm, GROUP_W), F32)],
        compiler_params=_cparams("arbitrary", "arbitrary"),
        name="conv_mixers",
    )(h, h, h, h, h, h, h, dw_w, dw_b.reshape(1, -1), ln_g.reshape(1, -1), ln_b.reshape(1, -1),
      pw_w.astype(BF16), pw_b.reshape(1, -1), sc_w)


def _first_max4(vals):
    m1 = jnp.maximum(jnp.maximum(vals[0], vals[1]), jnp.maximum(vals[2], vals[3]))
    i1 = jnp.where(vals[0] == m1, 0, jnp.where(vals[1] == m1, 1, jnp.where(vals[2] == m1, 2, 3)))
    rest = [jnp.where(i1 == j, -jnp.inf, vals[j]) for j in range(4)]
    m2 = jnp.maximum(jnp.maximum(rest[0], rest[1]), jnp.maximum(rest[2], rest[3]))
    i2 = jnp.where(rest[0] == m2, 0, jnp.where(rest[1] == m2, 1, jnp.where(rest[2] == m2, 2, 3)))
    return m1, i1, m2, i2


def _outproj_kernel(ya_ref, yb_ref, yc_ref, yd_ref, x_ref, w_ref, g_ref, b_ref, x1_ref):
    mix = jnp.dot(ya_ref[...], w_ref[0:GROUP_W, :], preferred_element_type=F32)
    mix = mix + jnp.dot(yb_ref[...], w_ref[GROUP_W:2 * GROUP_W, :], preferred_element_type=F32)
    mix = mix + jnp.dot(yc_ref[...], w_ref[2 * GROUP_W:3 * GROUP_W, :], preferred_element_type=F32)
    mix = mix + jnp.dot(yd_ref[...], w_ref[3 * GROUP_W:4 * GROUP_W, :], preferred_element_type=F32)
    y = ALPHA * x_ref[...] + mix
    mu = jnp.mean(y, axis=-1, keepdims=True)
    yc = y - mu
    var = jnp.mean(yc * yc, axis=-1, keepdims=True)
    x1_ref[...] = yc * lax.rsqrt(var + LN_EPS) * g_ref[...] + b_ref[...]


def _router_kernel(x_ref, w2_ref, w1_ref, rb_ref, eid_ref, gate_ref):
    tm = x_ref.shape[0]
    x = x_ref[...]
    xh = x.astype(BF16)
    xm = (x - xh.astype(F32)).astype(BF16)
    r = (jnp.dot(xh, w2_ref[...], preferred_element_type=F32)
         + jnp.dot(xm, w1_ref[...], preferred_element_type=F32))
    r = r + pltpu.roll(r, LANES - N_EXPERTS, 1)
    logits = r.T[0:N_EXPERTS, :]
    aff = _sigmoid(logits)
    sel = aff + jnp.concatenate([rb_ref[...]] * (tm // LANES), axis=1)
    rows = [sel[e:e + 1, :] for e in range(N_EXPERTS)]
    arow = [aff[e:e + 1, :] for e in range(N_EXPERTS)]
    tops = [_first_max4(rows[4 * g:4 * g + 4]) for g in range(N_GROUPS)]
    score = [tp[0] + tp[2] for tp in tops]
    best = jnp.maximum(jnp.maximum(score[0], score[1]), jnp.maximum(score[2], score[3]))
    grp = jnp.where(score[0] == best, 0, jnp.where(score[1] == best, 1, jnp.where(score[2] == best, 2, 3)))
    pick = lambda idx: jnp.where(grp == 0, tops[0][idx], jnp.where(grp == 1, tops[1][idx],
                                 jnp.where(grp == 2, tops[2][idx], tops[3][idx])))
    e1 = grp * EXPERTS_PER_GROUP + pick(1)
    e2 = grp * EXPERTS_PER_GROUP + pick(3)
    a1 = jnp.zeros_like(best)
    a2 = jnp.zeros_like(best)
    for e in range(N_EXPERTS):
        a1 = jnp.where(e1 == e, arow[e], a1)
        a2 = jnp.where(e2 == e, arow[e], a2)
    den = a1 + a2
    zi = jnp.zeros((SUBLANES - 2, tm), I32)
    zf = jnp.zeros((SUBLANES - 2, tm), F32)
    eid_ref[...] = jnp.concatenate([e1.astype(I32), e2.astype(I32), zi], axis=0)
    gate_ref[...] = jnp.concatenate([a1 / den, a2 / den, zf], axis=0)


def _out_projection(ya, yb, yc, yd, x, w_out, ln_g, ln_b, layer):
    t, d = ya.shape[0], x.shape[1]
    tm = min(512, t)
    act = lambda: pl.BlockSpec((tm, GROUP_W), lambda i: (i, 0))
    const = lambda shape: pl.BlockSpec(shape, lambda i: (0,) * len(shape))
    return pl.pallas_call(
        _outproj_kernel,
        out_shape=jax.ShapeDtypeStruct((t, d), F32),
        grid=(t // tm,),
        in_specs=[act(), act(), act(), act(),
                  pl.BlockSpec((tm, d), lambda i: (i, 0)),
                  pl.BlockSpec((None, d, d), lambda i: (layer, 0, 0)), const((1, d)), const((1, d))],
        out_specs=pl.BlockSpec((tm, d), lambda i: (i, 0)),
        compiler_params=_cparams("parallel"),
        name="out_proj",
    )(ya, yb, yc, yd, x, w_out, ln_g.reshape(1, d), ln_b.reshape(1, d))


def _router(x1, router_w, router_bias):
    t, d = x1.shape
    tm = min(512, t)
    w_hi = router_w.astype(BF16)
    w_mid = (router_w - w_hi.astype(F32)).astype(BF16)
    w2 = jnp.concatenate([w_hi, w_mid, jnp.zeros((d, LANES - 2 * N_EXPERTS), BF16)], axis=1)
    w1 = jnp.concatenate([w_hi, jnp.zeros((d, LANES - N_EXPERTS), BF16)], axis=1)
    router_b = jnp.broadcast_to(router_bias.reshape(N_EXPERTS, 1), (N_EXPERTS, LANES))
    const = lambda shape: pl.BlockSpec(shape, lambda i: (0,) * len(shape))
    return pl.pallas_call(
        _router_kernel,
        out_shape=(jax.ShapeDtypeStruct((SUBLANES, t), I32), jax.ShapeDtypeStruct((SUBLANES, t), F32)),
        grid=(t // tm,),
        in_specs=[pl.BlockSpec((tm, d), lambda i: (i, 0)),
                  const((d, LANES)), const((d, LANES)), const((N_EXPERTS, LANES))],
        out_specs=(pl.BlockSpec((SUBLANES, tm), lambda i: (0, i)),
                   pl.BlockSpec((SUBLANES, tm), lambda i: (0, i))),
        compiler_params=_cparams("parallel"),
        name="router",
    )(x1, w2, w1, router_b)


PAIR_A = (0, 0, 0, 1, 1, 3)
PAIR_B = (1, 2, 3, 3, 2, 2)
N_PAIRS = len(PAIR_A)
MOE_STAGES = 6
N_BUCKETS = N_GROUPS * N_PAIRS


def _moe_pair_kernel(ea_ref, eb_ref, nused_ref, nv_ref, src_ref, srcn_ref, dst_ref, dstp_ref, dstpp_ref, gt_ref, x_hbm,
                     wga_ref, wua_ref, wda_ref, wgb_ref, wub_ref, wdb_ref, g_ref, b_ref, o_hbm,
                     xbuf0_ref, xbuf1_ref, obuf0_ref, obuf1_ref, gsem_ref, ssem_ref):
    i = pl.program_id(0)
    nused = nused_ref[0]
    xbufs = (xbuf0_ref, xbuf1_ref)
    obufs = (obuf0_ref, obuf1_ref)

    def gather(rows_ref, s):
        return [pltpu.make_async_copy(x_hbm.at[pl.ds(rows_ref[0, 0, r], 1), :], xbufs[s].at[pl.ds(r, 1), :],
                                      gsem_ref.at[s]) for r in range(MOE_BLOCK)]

    def scatter(rows_ref, s):
        return [pltpu.make_async_copy(obufs[s].at[pl.ds(r, 1), :], o_hbm.at[pl.ds(rows_ref[0, 0, r], 1), :],
                                      ssem_ref.at[s]) for r in range(MOE_BLOCK)]

    def start(copies):
        for r, c in enumerate(copies):
            c.start(priority=r % 2)

    def wait(copies):
        for c in copies:
            c.wait()

    def scatter_rows(rows_ref, s, n, go):
        def one(r):
            return pltpu.make_async_copy(obufs[s].at[pl.ds(r, 1), :], o_hbm.at[pl.ds(rows_ref[0, 0, r], 1), :],
                                         ssem_ref.at[s])

        @pl.when(n == MOE_BLOCK)
        def _():
            (start if go else wait)(scatter(rows_ref, s))

        @pl.when(n < MOE_BLOCK)
        def _():
            def body(r, carry):
                if go:
                    one(r).start()
                else:
                    one(r).wait()
                return carry
            lax.fori_loop(0, n, body, 0)

    def zero_after(v):
        bits = pltpu.bitcast(v[v.shape[0] - SUBLANES:, v.shape[1] - LANES:], I32)
        return ((bits[SUBLANES - 1, LANES - 1] >> 31) + 1) >> 1

    def compute(s, with_scatter):
        per = -(-MOE_BLOCK // MOE_STAGES)

        def issue(k, z):
            for r in range(k * per, min((k + 1) * per, MOE_BLOCK)):
                pltpu.make_async_copy(x_hbm.at[pl.ds(srcn_ref[0, 0, r] + z, 1), :],
                                      xbufs[1 - s].at[pl.ds(r, 1), :], gsem_ref.at[1 - s]).start(priority=r % 2)
                if with_scatter:
                    pltpu.make_async_copy(obufs[1 - s].at[pl.ds(r, 1), :],
                                          o_hbm.at[pl.ds(dstp_ref[0, 0, r] + z, 1), :],
                                          ssem_ref.at[1 - s]).start(priority=(r + 1) % 2)

        x = xbufs[s][...]
        xh = x.astype(BF16)
        gt = gt_ref[...]
        issue(0, 0)
        hg = jnp.dot(xh, wga_ref[...], preferred_element_type=F32)
        issue(1, zero_after(hg))
        hu = jnp.dot(xh, wua_ref[...], preferred_element_type=F32)
        issue(2, zero_after(hu))
        ya = jnp.dot((_silu(hg) * hu).astype(BF16), wda_ref[...], preferred_element_type=F32)
        issue(3, zero_after(ya))
        hg = jnp.dot(xh, wgb_ref[...], preferred_element_type=F32)
        issue(4, zero_after(hg))
        hu = jnp.dot(xh, wub_ref[...], preferred_element_type=F32)
        issue(5, zero_after(hu))
        yb = jnp.dot((_silu(hg) * hu).astype(BF16), wdb_ref[...], preferred_element_type=F32)
        y = ALPHA * x + ya * gt[:, 0:1] + yb * gt[:, 1:2]
        mu = jnp.mean(y, axis=-1, keepdims=True)
        yc = y - mu
        var = jnp.mean(yc * yc, axis=-1, keepdims=True)
        obufs[s][...] = yc * lax.rsqrt(var + LN_EPS) * g_ref[...] + b_ref[...]

    @pl.when(i == 0)
    def _():
        start(gather(src_ref, 0))

    nv_prev = nv_ref[jnp.maximum(i - 1, 0)]
    full_prev = jnp.logical_and(i >= 1, nv_prev == MOE_BLOCK)
    for s in range(2):
        mine = jnp.logical_and(i < nused, i % 2 == s)

        @pl.when(mine)
        def _():
            wait(gather(src_ref, s))

        @pl.when(jnp.logical_and(mine, i >= 2))
        def _():
            scatter_rows(dstpp_ref, s, nv_ref[jnp.maximum(i - 2, 0)], False)

        @pl.when(jnp.logical_and(mine, jnp.logical_and(i >= 1, jnp.logical_not(full_prev))))
        def _():
            scatter_rows(dstp_ref, 1 - s, nv_prev, True)

        @pl.when(jnp.logical_and(mine, full_prev))
        def _():
            compute(s, True)

        @pl.when(jnp.logical_and(mine, jnp.logical_not(full_prev)))
        def _():
            compute(s, False)

        last = jnp.logical_and(mine, i + 1 == nused)

        @pl.when(last)
        def _():
            scatter_rows(dst_ref, s, nv_ref[i], True)
            scatter_rows(dst_ref, s, nv_ref[i], False)
            wait(gather(srcn_ref, 1 - s))

        @pl.when(jnp.logical_and(last, i >= 1))
        def _():
            scatter_rows(dstp_ref, 1 - s, nv_ref[jnp.maximum(i - 1, 0)], False)


def _moe_pairs(x1, tables, gates_sorted, wg, wu, wd, ln_g, ln_b, layer):
    rows, ea, eb, nused, nvalid = tables
    t, d = x1.shape
    n_blk = rows.shape[0]
    rows_spec = lambda off, clip_hi: pl.BlockSpec(
        (1, 1, MOE_BLOCK),
        lambda i, a, b, nu, nv: (jnp.clip(i + off, 0, nu[0] - 1) if clip_hi else jnp.maximum(i + off, 0), 0, 0),
        memory_space=pltpu.SMEM)
    wspec = lambda shape, sel: pl.BlockSpec(
        (None, None) + shape, lambda i, a, b, nu, nv: (layer, (a if sel == 0 else b)[i], 0, 0),
        pipeline_mode=pl.Buffered(1))
    vec = pl.BlockSpec((1, d), lambda i, a, b, nu, nv: (0, 0))
    return pl.pallas_call(
        _moe_pair_kernel,
        out_shape=jax.ShapeDtypeStruct((t, d), F32),
        grid_spec=pltpu.PrefetchScalarGridSpec(
            num_scalar_prefetch=4,
            grid=(n_blk,),
            in_specs=[rows_spec(0, True), rows_spec(1, True), rows_spec(0, True), rows_spec(-1, False),
                      rows_spec(-2, False),
                      pl.BlockSpec((MOE_BLOCK, TOP_K), lambda i, a, b, nu, nv: (i, 0)),
                      pl.BlockSpec(memory_space=pl.ANY),
                      wspec((d, D_EXPERT), 0), wspec((d, D_EXPERT), 0), wspec((D_EXPERT, d), 0),
                      wspec((d, D_EXPERT), 1), wspec((d, D_EXPERT), 1), wspec((D_EXPERT, d), 1),
                      vec, vec],
            out_specs=pl.BlockSpec(memory_space=pl.ANY),
            scratch_shapes=[pltpu.VMEM((MOE_BLOCK, d), F32), pltpu.VMEM((MOE_BLOCK, d), F32),
                            pltpu.VMEM((MOE_BLOCK, d), F32), pltpu.VMEM((MOE_BLOCK, d), F32),
                            pltpu.SemaphoreType.DMA((2,)), pltpu.SemaphoreType.DMA((2,))]),
        compiler_params=_cparams("arbitrary"),
        name="moe_pairs",
    )(ea, eb, nused, nvalid, rows, rows, rows, rows, rows, gates_sorted, x1, wg, wu, wd, wg, wu, wd,
      ln_g.reshape(1, d), ln_b.reshape(1, d))


def _pair_tables(eid, gate, t):
    e1, e2 = eid[0], eid[1]
    grp = e1 // EXPERTS_PER_GROUP
    a, b = e1 % EXPERTS_PER_GROUP, e2 % EXPERTS_PER_GROUP
    pidx = jnp.zeros_like(a)
    a_of_pair = jnp.zeros_like(a)
    for p in range(N_PAIRS):
        hit = jnp.logical_or(jnp.logical_and(a == PAIR_A[p], b == PAIR_B[p]),
                             jnp.logical_and(a == PAIR_B[p], b == PAIR_A[p]))
        pidx = jnp.where(hit, p, pidx)
        a_of_pair = jnp.where(hit, PAIR_A[p], a_of_pair)
    bucket = grp * N_PAIRS + pidx
    in_a = a == a_of_pair
    gate_a = jnp.where(in_a, gate[0], gate[1])

    onehot = (bucket[:, None] == jnp.arange(N_BUCKETS, dtype=I32)[None, :]).astype(I32)
    csum = jnp.cumsum(onehot, axis=0)
    counts = csum[-1]
    padded = ((counts + MOE_BLOCK - 1) // MOE_BLOCK) * MOE_BLOCK
    pends = jnp.cumsum(padded)
    pos = jnp.sum(onehot * (csum - 1 + (pends - padded)[None, :]), axis=1)
    n_blk = -(-t // MOE_BLOCK) + N_BUCKETS
    n_pad = n_blk * MOE_BLOCK
    rows = jnp.zeros((n_pad,), I32).at[pos].set(jnp.arange(t, dtype=I32))
    nused = (pends[-1] // MOE_BLOCK).astype(I32)
    blk = jnp.minimum(jnp.arange(n_blk, dtype=I32), nused - 1)
    blk_bucket = jnp.sum((pends[None, :] <= (blk * MOE_BLOCK)[:, None]).astype(I32), axis=1)
    base = (blk_bucket // N_PAIRS) * EXPERTS_PER_GROUP
    pair = blk_bucket % N_PAIRS
    ea, eb = base, base
    for p in range(N_PAIRS):
        ea = ea + jnp.where(pair == p, PAIR_A[p], 0)
        eb = eb + jnp.where(pair == p, PAIR_B[p], 0)
    ea, eb = ea.astype(I32), eb.astype(I32)
    valid_end = pends - padded + counts
    blk_onehot = (blk_bucket[:, None] == jnp.arange(N_BUCKETS, dtype=I32)[None, :]).astype(I32)
    nvalid = jnp.clip(jnp.sum(blk_onehot * valid_end[None, :], axis=1) - blk * MOE_BLOCK, 0, MOE_BLOCK).astype(I32)
    tables = (rows.reshape(n_blk, 1, MOE_BLOCK), ea, eb, nused.reshape(1), nvalid)
    ga = gate_a[rows]
    return tables, jnp.stack([ga, 1.0 - ga], axis=1)


def _rope_tables(seq):
    half = DIFF_QK_DIM // 2
    inv = 1.0 / (ROPE_THETA ** (jnp.arange(0, DIFF_QK_DIM, 2, dtype=F32) / DIFF_QK_DIM))
    ang = jnp.arange(seq, dtype=F32)[:, None] * inv[None, :]
    cos, sin = jnp.cos(ang), jnp.sin(ang)
    cos_t = jnp.concatenate([cos, cos, cos, cos], axis=1)
    sin_t = jnp.concatenate([-sin, sin, -sin, sin], axis=1)
    return cos_t, sin_t


W_IN_SPLITS = (0, 512, 1536, 1544, 2056, 2568, 3080, 4104, 5640)


def _wprep_kernel(w_ref, o_ref):
    w = w_ref[...]
    z, xbc, dt, q, k, v, conf, sc = [w[:, a:b] for a, b in zip(W_IN_SPLITS[:-1], W_IN_SPLITS[1:])]
    rows = w.shape[0]
    o_ref[0] = jnp.concatenate([xbc, conf], axis=1).astype(BF16)
    o_ref[1] = jnp.concatenate([sc, z], axis=1).astype(BF16)
    o_ref[2] = jnp.concatenate([q, k, v, dt, jnp.zeros((rows, TN - SSD_HEADS), F32)], axis=1).astype(BF16)


def _prep_w_in(w_all, layer):
    _, d, n = w_all.shape
    rows = 256
    return pl.pallas_call(
        _wprep_kernel,
        out_shape=jax.ShapeDtypeStruct((INPROJ_TILES, d, INPROJ_TN), BF16),
        grid=(d // rows,),
        in_specs=[pl.BlockSpec((None, rows, n), lambda i: (layer, i, 0))],
        out_specs=pl.BlockSpec((INPROJ_TILES, rows, INPROJ_TN), lambda i: (0, i, 0)),
        compiler_params=_cparams("parallel"),
        name="w_in_prep",
    )(w_all)


def kernel(x, ln_in_g, ln_in_b, w_in, ssd_conv_w, ssd_conv_b, ssd_dt_bias, ssd_a_log, ssd_d, ssd_norm_g, diff_lq1, diff_lk1, diff_lq2, diff_lk2, diff_norm_g, conf_dw_w, conf_dw_b, conf_ln_g, conf_ln_b, conf_pw_w, conf_pw_b, sc_conv_w, w_out, ln1_g, ln1_b, router_w, router_bias, moe_w_gate, moe_w_up, moe_w_down, ln2_g, ln2_b):
    b, s, d = x.shape
    t = b * s
    cos_t, sin_t = _rope_tables(s)
    xf = _layer_norm(x.reshape(t, d), ln_in_g, ln_in_b)
    w_out_b = w_out.astype(BF16)
    wg_b, wu_b, wd_b = moe_w_gate.astype(BF16), moe_w_up.astype(BF16), moe_w_down.astype(BF16)
    for l in range(DEPTH):
        lambda_init = 0.8 - 0.6 * math.exp(-0.3 * l)
        h, dt_raw = _in_projection(xf, _prep_w_in(w_in, l), cos_t, sin_t, s, t)
        ya = _ssd_mixer(h, dt_raw, ssd_conv_w[l], ssd_conv_b[l], ssd_dt_bias[l], ssd_a_log[l],
                        ssd_d[l], ssd_norm_g[l], b, s)
        yb = _diff_attention(h, diff_lq1[l], diff_lk1[l], diff_lq2[l], diff_lk2[l], diff_norm_g[l],
                             lambda_init, b, s)
        yc, yd = _conv_mixers(h, conf_dw_w[l], conf_dw_b[l], conf_ln_g[l], conf_ln_b[l], conf_pw_w[l],
                              conf_pw_b[l], sc_conv_w[l], b, s)
        x1 = _out_projection(ya, yb, yc, yd, xf, w_out_b, ln1_g[l], ln1_b[l], l)
        eid, gate = _router(x1, router_w, router_bias)
        tables, gates_sorted = _pair_tables(eid, gate, t)
        xf = _moe_pairs(x1, tables, gates_sorted, wg_b, wu_b, wd_b, ln2_g[l], ln2_b[l], l)
    return xf.reshape(b, s, d)
```
